```python
import math
import jax
import jax.numpy as jnp
from jax import lax
import numpy as np

D_MODEL = 1024
BATCH = 8
SEQ = 8192
DEPTH = 2

CTX_LEN = 256
GRID_W = 64
EPS = 1e-6
N_MOD = 6
MLP_HIDDEN = 4 * D_MODEL

CONV_CH = D_MODEL // 2
CONV_W = 31
DIFF_HEADS = 4
DIFF_DQK = 64
DIFF_DV = 2 * DIFF_DQK
Q_BLOCK = 128
ROPE_BASE = 10000.0
EV_QK = DIFF_HEADS * 2 * DIFF_DQK
EV_VW = DIFF_HEADS * DIFF_DV
EV_Q0 = 2 * CONV_CH
EV_K0 = EV_Q0 + EV_QK
EV_V0 = EV_K0 + EV_QK
EVEN_IN = EV_V0 + EV_VW
EVEN_MIX = CONV_CH + EV_VW

HY_CH = D_MODEL // 2
HY_ORDER = 2
HY_SHORT_W = 3
HY_EMB = 33
HY_FFN = 64
HY_FAST = 0.3
HY_SLOW = 1.5
HY_TARGET = 1e-2
HY_MAX_DECAY = math.log(HY_TARGET) / HY_FAST
HY_MIN_DECAY = math.log(HY_TARGET) / HY_SLOW
HY_IN = (HY_ORDER + 1) * HY_CH
DN_HEADS = 4
DN_DK = 128
DN_DV = 128
DN_WIDTH = DN_HEADS * DN_DK
DN_CONV_W = 5
DN_CHUNK = 64
OD_G0 = HY_IN
OD_Q0 = OD_G0 + DN_HEADS * DN_DV
OD_K0 = OD_Q0 + DN_WIDTH
ODD_IN = OD_K0 + 2 * DN_WIDTH + 4 * DN_HEADS
ODD_MIX = HY_CH + DN_HEADS * DN_DV

N_EVEN = (DEPTH + 1) // 2
N_ODD = DEPTH // 2

kernel_name = 'hybrid_conformer_diffattn_hyena_gdn_dit'

F32 = jnp.float32


def rms_norm(x, g):
    xf = x.astype(F32)
    y = xf * lax.rsqrt(jnp.mean(xf * xf, axis=-1, keepdims=True) + EPS)
    return (y * g.astype(F32)).astype(x.dtype)


def layer_norm(x, g, b):
    xf = x.astype(F32)
    mu = jnp.mean(xf, axis=-1, keepdims=True)
    var = jnp.mean(jnp.square(xf - mu), axis=-1, keepdims=True)
    y = (xf - mu) * lax.rsqrt(var + EPS)
    return (y * g.astype(F32) + b.astype(F32)).astype(x.dtype)


def l2norm(t):
    return t * lax.rsqrt(jnp.sum(t * t, axis=-1, keepdims=True) + 1e-6)


def modulate(h, shift, scale):
    return h * (1.0 + scale) + shift


def squared_relu_mlp(h, w1, w2):
    return jnp.square(jax.nn.relu(h @ w1)) @ w2


def depthwise_conv(x, w, b=None):
    width = w.shape[0]
    left = (width - 1) // 2
    y = lax.conv_general_dilated(x, w[:, None, :].astype(x.dtype), (1,), [(left, width - 1 - left)],
                                 dimension_numbers=('NWC', 'WIO', 'NWC'), feature_group_count=x.shape[-1])
    return y if b is None else y + b


def axial_rope_tables(length):
    rows = length // GRID_W
    row = jnp.repeat(jnp.arange(rows, dtype=F32), GRID_W)
    col = jnp.tile(jnp.arange(GRID_W, dtype=F32), rows)
    n_freq = DIFF_DQK // 4
    inv = ROPE_BASE ** (-jnp.arange(n_freq, dtype=F32) / n_freq)
    ang_r = row[:, None] * inv
    ang_c = col[:, None] * inv
    return (jnp.cos(ang_r), jnp.sin(ang_r), jnp.cos(ang_c), jnp.sin(ang_c))


def rotate(x, cos, sin):
    x1, x2 = jnp.split(x, 2, axis=-1)
    return jnp.concatenate([x1 * cos - x2 * sin, x1 * sin + x2 * cos], axis=-1)


def apply_axial_rope(x, rope):
    cr, sr, cc, sc = rope
    xf = x.astype(F32)
    half = x.shape[-1] // 2
    return jnp.concatenate([rotate(xf[..., :half], cr, sr), rotate(xf[..., half:], cc, sc)], axis=-1).astype(x.dtype)


def conformer_conv(p_glu, conv_w, conv_b, ln_g, ln_b):
    a, gate = jnp.split(p_glu, 2, axis=-1)
    u = depthwise_conv(a * jax.nn.sigmoid(gate), conv_w, conv_b)
    return jax.nn.silu(layer_norm(u, ln_g, ln_b))


def qk_heads(t):
    b, l, _ = t.shape
    return t.reshape(b, l, DIFF_HEADS, 2, DIFF_DQK).transpose(0, 2, 3, 1, 4)


def v_heads(t):
    b, l, _ = t.shape
    return t.reshape(b, l, DIFF_HEADS, DIFF_DV).transpose(0, 2, 1, 3)


def diff_attend(q, k, v, lam):
    s = jnp.einsum('bhmqd,bhmkd->bhmqk', q, k).astype(F32) * (DIFF_DQK ** -0.5)
    p = jax.nn.softmax(s, axis=-1)
    p = p[:, :, 0] - lam * p[:, :, 1]
    return jnp.einsum('bhqk,bhkv->bhqv', p.astype(v.dtype), v)


def diff_merge(o, subln_g, lambda_init):
    b, h, l, dv = o.shape
    o = rms_norm(o, subln_g) * (1.0 - lambda_init)
    return o.transpose(0, 2, 1, 3).reshape(b, l, h * dv)


def even_mixer(h_lat, h_ctx, rope, w_in, conv_w, conv_b, ln_g, ln_b, lq1, lk1, lq2, lk2, subln_g, w_out,
               lambda_init, ctx_out):
    lam = (jnp.exp(jnp.sum(lq1.astype(F32) * lk1.astype(F32)))
           - jnp.exp(jnp.sum(lq2.astype(F32) * lk2.astype(F32))) + lambda_init)
    b, length = h_lat.shape[:2]
    p_lat = h_lat @ w_in
    conv_lat = conformer_conv(p_lat[..., :EV_Q0], conv_w, conv_b, ln_g, ln_b)
    q_lat = apply_axial_rope(qk_heads(p_lat[..., EV_Q0:EV_K0]), rope)
    k_lat = apply_axial_rope(qk_heads(p_lat[..., EV_K0:EV_V0]), rope)
    v_lat = v_heads(p_lat[..., EV_V0:])
    if ctx_out:
        p_ctx = h_ctx @ w_in
        kv_ctx = p_ctx[..., EV_K0:]
    else:
        kv_ctx = h_ctx @ w_in[:, EV_K0:]
    k_ctx = qk_heads(kv_ctx[..., :EV_QK])
    v_ctx = v_heads(kv_ctx[..., EV_QK:])
    k_all = jnp.concatenate([k_lat, k_ctx], axis=3)
    v_all = jnp.concatenate([v_lat, v_ctx], axis=2)
    nb = length // Q_BLOCK
    q_blocks = jnp.moveaxis(q_lat.reshape(b, DIFF_HEADS, 2, nb, Q_BLOCK, DIFF_DQK), 3, 0)
    o = lax.map(lambda qb: diff_attend(qb, k_all, v_all, lam), q_blocks)
    o = jnp.moveaxis(o, 0, 2).reshape(b, DIFF_HEADS, length, DIFF_DV)
    y_lat = jnp.concatenate([conv_lat, diff_merge(o, subln_g, lambda_init)], axis=-1) @ w_out
    if not ctx_out:
        return y_lat, None
    conv_ctx = conformer_conv(p_ctx[..., :EV_Q0], conv_w, conv_b, ln_g, ln_b)
    o_ctx = diff_attend(qk_heads(p_ctx[..., EV_Q0:EV_K0]), k_ctx, v_ctx, lam)
    y_ctx = jnp.concatenate([conv_ctx, diff_merge(o_ctx, subln_g, lambda_init)], axis=-1) @ w_out
    return y_lat, y_ctx


def hyena_filters(length, w1, b1, freq1, w2, b2, freq2, w3):
    t = jnp.linspace(0.0, 1.0, length, dtype=F32)[:, None]
    bands = (HY_EMB - 1) // 2
    omega = 2.0 * math.pi * jnp.arange(length, dtype=F32)[:, None] / length
    ang = omega * jnp.linspace(1e-4, bands - 1, bands, dtype=F32)
    z = jnp.concatenate([t, jnp.cos(ang), -jnp.sin(ang)], axis=-1)
    hid = jnp.sin(freq1 * (z @ w1 + b1))
    hid = jnp.sin(freq2 * (hid @ w2 + b2))
    filt = (hid @ w3).astype(F32).reshape(length, HY_ORDER, 2, HY_CH)
    deltas = jnp.abs(jnp.linspace(HY_MIN_DECAY, HY_MAX_DECAY, HY_CH, dtype=F32))
    filt = filt * jnp.exp(-t * deltas)[:, None, None, :]
    return filt / jnp.sum(jnp.abs(filt), axis=(0, 2), keepdims=True)


def two_sided(h_fwd, h_bwd):
    return jnp.concatenate([h_fwd, jnp.zeros((1, h_fwd.shape[1]), h_fwd.dtype), h_bwd[:0:-1]], axis=0)


def fft_long_conv(u, filt2, skip):
    length = u.shape[1]
    n = 2 * length
    y = jnp.fft.irfft(jnp.fft.rfft(u, n=n, axis=1) * jnp.fft.rfft(filt2, n=n, axis=0), n=n, axis=1)[:, :length]
    return y + u * skip.astype(F32)


def hyena(p_hy, short_w, short_b, w1, b1, freq1, w2, b2, freq2, w3, skip):
    length = p_hy.shape[1]
    u = depthwise_conv(p_hy, short_w, short_b).astype(F32)
    v, x1, x2 = jnp.split(u, 3, axis=-1)
    filt = hyena_filters(length, w1, b1, freq1, w2, b2, freq2, w3)
    z = v
    for order, gate in enumerate((x1, x2)):
        z = gate * fft_long_conv(z, two_sided(filt[:, order, 0], filt[:, order, 1]), skip[order])
    return z.astype(p_hy.dtype)


def delta_features(p, conv_w, a_log_f, a_log_b, dtb_f, dtb_b, with_q):
    n_qkv = (3 if with_q else 2) * DN_WIDTH
    qkv = jax.nn.silu(depthwise_conv(p[..., :n_qkv], conv_w[:, -n_qkv:])).astype(F32)
    b, l = qkv.shape[:2]
    parts = [t.reshape(b, l, DN_HEADS, -1).transpose(0, 2, 1, 3) for t in jnp.split(qkv, n_qkv // DN_WIDTH, axis=-1)]
    if with_q:
        q, k, v = parts
        q = l2norm(q)
    else:
        k, v = parts
        q = None
    k = l2norm(k)
    rest = p[..., n_qkv:].astype(F32).transpose(0, 2, 1)
    h = DN_HEADS
    beta_f = jax.nn.sigmoid(rest[:, 0:h])
    beta_b = jax.nn.sigmoid(rest[:, h:2 * h])
    g_f = -jnp.exp(a_log_f.astype(F32))[None, :, None] * jax.nn.softplus(rest[:, 2 * h:3 * h] + dtb_f.astype(F32)[None, :, None])
    g_b = -jnp.exp(a_log_b.astype(F32))[None, :, None] * jax.nn.softplus(rest[:, 3 * h:4 * h] + dtb_b.astype(F32)[None, :, None])
    return q, k, v, beta_f, beta_b, g_f, g_b


def gated_delta_chunked(q, k, v, beta, g, state):
    b, h, length, dk = k.shape
    n = length // DN_CHUNK
    chunk = lambda t: t.reshape(b, h, n, DN_CHUNK, *t.shape[3:])
    k, v, beta, g = chunk(k), chunk(v), chunk(beta), chunk(g)
    gcum = jnp.cumsum(g, axis=-1)
    pos = jnp.arange(DN_CHUNK)
    incl = pos[:, None] >= pos[None, :]
    strict = pos[:, None] > pos[None, :]
    decay = jnp.exp(jnp.where(incl, gcum[..., :, None] - gcum[..., None, :], -jnp.inf))
    kb = k * beta[..., None]
    lower = jnp.where(strict, jnp.einsum('bhncd,bhnsd->bhncs', kb, k) * decay, 0.0)
    eye = jnp.eye(DN_CHUNK, dtype=F32)
    tmat = lax.linalg.triangular_solve(eye + lower, jnp.broadcast_to(eye, lower.shape),
                                       left_side=True, lower=True, unit_diagonal=True)
    u = tmat @ (v * beta[..., None])
    w = tmat @ (kb * jnp.exp(gcum)[..., None])
    g_last = gcum[..., -1]
    k_dec = k * jnp.exp(g_last[..., None] - gcum)[..., None]
    front = lambda t: jnp.moveaxis(t, 2, 0)
    if q is None:
        def step_state(s, xs):
            u_c, w_c, kd_c, gl_c = xs
            v_new = u_c - w_c @ s
            return s * jnp.exp(gl_c)[..., None, None] + jnp.swapaxes(kd_c, -1, -2) @ v_new, None
        state, _ = lax.scan(step_state, state, (front(u), front(w), front(k_dec), front(g_last)))
        return None, state
    q = chunk(q) * (dk ** -0.5)
    attn = jnp.einsum('bhncd,bhnsd->bhncs', q, k) * decay
    q_dec = q * jnp.exp(gcum)[..., None]

    def step(s, xs):
        u_c, w_c, kd_c, gl_c, a_c, qd_c = xs
        v_new = u_c - w_c @ s
        o = qd_c @ s + a_c @ v_new
        return s * jnp.exp(gl_c)[..., None, None] + jnp.swapaxes(kd_c, -1, -2) @ v_new, o
    state, o = lax.scan(step, state, (front(u), front(w), front(k_dec), front(g_last), front(attn), front(q_dec)))
    o = jnp.moveaxis(o, 0, 2).reshape(b, h, length, -1)
    return o, state


def flip_seq(t):
    return None if t is None else jnp.flip(t, axis=2)


def delta_output(o, gate, norm_g):
    b, h, l, dv = o.shape
    o = rms_norm(o.transpose(0, 2, 1, 3), norm_g)
    return (o * jax.nn.silu(gate.reshape(b, l, h, dv).astype(F32))).reshape(b, l, h * dv).astype(gate.dtype)


def odd_mixer(h_lat, h_ctx, w_in, hy_short_w, hy_short_b, hy_w1, hy_b1, hy_freq1, hy_w2, hy_b2, hy_freq2, hy_w3,
              hy_skip, dn_conv_w, a_log_f, a_log_b, dtb_f, dtb_b, dn_norm_g, w_out, ctx_out):
    hy_args = (hy_short_w, hy_short_b, hy_w1, hy_b1, hy_freq1, hy_w2, hy_b2, hy_freq2, hy_w3, hy_skip)
    dn_args = (a_log_f, a_log_b, dtb_f, dtb_b)
    p_lat = h_lat @ w_in
    hy_lat = hyena(p_lat[..., :HY_IN], *hy_args)
    q_l, k_l, v_l, bf_l, bb_l, gf_l, gb_l = delta_features(p_lat[..., OD_Q0:], dn_conv_w, *dn_args, True)
    if ctx_out:
        p_ctx = h_ctx @ w_in
        feats_ctx = delta_features(p_ctx[..., OD_Q0:], dn_conv_w, *dn_args, True)
    else:
        feats_ctx = delta_features(h_ctx @ w_in[:, OD_K0:], dn_conv_w, *dn_args, False)
    q_c, k_c, v_c, bf_c, bb_c, gf_c, gb_c = feats_ctx
    s0 = jnp.zeros((h_lat.shape[0], DN_HEADS, DN_DK, DN_DV), F32)
    o_cf, s_cf = gated_delta_chunked(q_c, k_c, v_c, bf_c, gf_c, s0)
    o_cb, s_cb = gated_delta_chunked(flip_seq(q_c), flip_seq(k_c), flip_seq(v_c), flip_seq(bb_c), flip_seq(gb_c), s0)
    o_lf, _ = gated_delta_chunked(q_l, k_l, v_l, bf_l, gf_l, s_cf)
    o_lb, _ = gated_delta_chunked(flip_seq(q_l), flip_seq(k_l), flip_seq(v_l), flip_seq(bb_l), flip_seq(gb_l), s_cb)
    dn_lat = delta_output(o_lf + flip_seq(o_lb), p_lat[..., OD_G0:OD_Q0], dn_norm_g)
    y_lat = jnp.concatenate([hy_lat, dn_lat], axis=-1) @ w_out
    if not ctx_out:
        return y_lat, None
    hy_ctx = hyena(p_ctx[..., :HY_IN], *hy_args)
    dn_ctx = delta_output(o_cf + flip_seq(o_cb), p_ctx[..., OD_G0:OD_Q0], dn_norm_g)
    y_ctx = jnp.concatenate([hy_ctx, dn_ctx], axis=-1) @ w_out
    return y_lat, y_ctx


def setup_inputs(seed: int = 0) -> dict:
    key = jax.random.key(seed)
    ks = list(jax.random.split(key, 48))
    nrm = lambda shape, scale: scale * jax.random.normal(ks.pop(), shape, F32)
    gain = lambda shape: 1.0 + 0.05 * jax.random.normal(ks.pop(), shape, F32)
    d = D_MODEL
    a_log = lambda: jnp.log(jax.random.uniform(ks.pop(), (N_ODD, DN_HEADS), F32, 1.0, 16.0))

    def dt_bias():
        dt = jnp.exp(jax.random.uniform(ks.pop(), (N_ODD, DN_HEADS), F32, math.log(1e-3), math.log(1e-1)))
        return dt + jnp.log(-jnp.expm1(-dt))
    return {
        'x': nrm((BATCH, SEQ, d), 1.0),
        'c': nrm((BATCH, d), 1.0),
        'ctx': nrm((BATCH, CTX_LEN, d), 1.0),
        'c_ctx': nrm((d,), 1.0),
        'ada_w': nrm((DEPTH, d, N_MOD * d), 0.5 * d ** -0.5),
        'ada_b': nrm((DEPTH, N_MOD * d), 0.02),
        'norm1_g': gain((DEPTH, d)),
        'norm2_g': gain((DEPTH, d)),
        'mlp_w1': nrm((DEPTH, d, MLP_HIDDEN), d ** -0.5),
        'mlp_w2': nrm((DEPTH, MLP_HIDDEN, d), MLP_HIDDEN ** -0.5),
        'ev_w_in': nrm((N_EVEN, d, EVEN_IN), d ** -0.5),
        'ev_conv_w': nrm((N_EVEN, CONV_W, CONV_CH), CONV_W ** -0.5),
        'ev_conv_b': nrm((N_EVEN, CONV_CH), 0.02),
        'ev_ln_g': gain((N_EVEN, CONV_CH)),
        'ev_ln_b': nrm((N_EVEN, CONV_CH), 0.02),
        'ev_lq1': nrm((N_EVEN, DIFF_DQK), 0.1),
        'ev_lk1': nrm((N_EVEN, DIFF_DQK), 0.1),
        'ev_lq2': nrm((N_EVEN, DIFF_DQK), 0.1),
        'ev_lk2': nrm((N_EVEN, DIFF_DQK), 0.1),
        'ev_subln_g': gain((N_EVEN, DIFF_DV)),
        'ev_w_out': nrm((N_EVEN, EVEN_MIX, d), EVEN_MIX ** -0.5),
        'od_w_in': nrm((N_ODD, d, ODD_IN), d ** -0.5),
        'od_hy_short_w': nrm((N_ODD, HY_SHORT_W, HY_IN), HY_SHORT_W ** -0.5),
        'od_hy_short_b': nrm((N_ODD, HY_IN), 0.02),
        'od_hy_w1': nrm((N_ODD, HY_EMB, HY_FFN), HY_EMB ** -0.5),
        'od_hy_b1': nrm((N_ODD, HY_FFN), 0.02),
        'od_hy_freq1': gain((N_ODD, HY_FFN)),
        'od_hy_w2': nrm((N_ODD, HY_FFN, HY_FFN), HY_FFN ** -0.5),
        'od_hy_b2': nrm((N_ODD, HY_FFN), 0.02),
        'od_hy_freq2': gain((N_ODD, HY_FFN)),
        'od_hy_w3': nrm((N_ODD, HY_FFN, HY_ORDER * 2 * HY_CH), HY_FFN ** -0.5),
        'od_hy_skip': nrm((N_ODD, HY_ORDER, HY_CH), 0.5),
        'od_dn_conv_w': nrm((N_ODD, DN_CONV_W, 3 * DN_WIDTH), DN_CONV_W ** -0.5),
        'od_dn_alog_f': a_log(),
        'od_dn_alog_b': a_log(),
        'od_dn_dtb_f': dt_bias(),
        'od_dn_dtb_b': dt_bias(),
        'od_dn_norm_g': gain((N_ODD, DN_DV)),
        'od_w_out': nrm((N_ODD, ODD_MIX, d), ODD_MIX ** -0.5),
        'final_g': gain((d,)),
    }


def reference(x, c, ctx, c_ctx, ada_w, ada_b, norm1_g, norm2_g, mlp_w1, mlp_w2,
              ev_w_in, ev_conv_w, ev_conv_b, ev_ln_g, ev_ln_b, ev_lq1, ev_lk1, ev_lq2, ev_lk2, ev_subln_g, ev_w_out,
              od_w_in, od_hy_short_w, od_hy_short_b, od_hy_w1, od_hy_b1, od_hy_freq1, od_hy_w2, od_hy_b2, od_hy_freq2,
              od_hy_w3, od_hy_skip, od_dn_conv_w, od_dn_alog_f, od_dn_alog_b, od_dn_dtb_f, od_dn_dtb_b, od_dn_norm_g,
              od_w_out, final_g):
    rope = axial_rope_tables(x.shape[1])
    cond_lat = jax.nn.silu(c)
    cond_ctx = jax.nn.silu(c_ctx)
    for i in range(DEPTH):
        last = i == DEPTH - 1
        mods = jnp.split(cond_lat @ ada_w[i] + ada_b[i], N_MOD, axis=-1)
        sh1, sc1, gt1, sh2, sc2, gt2 = [m[:, None, :] for m in mods]
        n_ctx_mod = 2 if last else N_MOD
        mods_c = jnp.split(cond_ctx @ ada_w[i][:, :n_ctx_mod * D_MODEL] + ada_b[i][:n_ctx_mod * D_MODEL], n_ctx_mod)
        h_lat = modulate(rms_norm(x, norm1_g[i]), sh1, sc1)
        h_ctx = modulate(rms_norm(ctx, norm1_g[i]), mods_c[0], mods_c[1])
        j = i // 2
        if i % 2 == 0:
            y_lat, y_ctx = even_mixer(h_lat, h_ctx, rope, ev_w_in[j], ev_conv_w[j], ev_conv_b[j], ev_ln_g[j], ev_ln_b[j],
                                      ev_lq1[j], ev_lk1[j], ev_lq2[j], ev_lk2[j], ev_subln_g[j], ev_w_out[j],
                                      0.8 - 0.6 * math.exp(-0.3 * i), not last)
        else:
            y_lat, y_ctx = odd_mixer(h_lat, h_ctx, od_w_in[j], od_hy_short_w[j], od_hy_short_b[j], od_hy_w1[j],
                                     od_hy_b1[j], od_hy_freq1[j], od_hy_w2[j], od_hy_b2[j], od_hy_freq2[j], od_hy_w3[j],
                                     od_hy_skip[j], od_dn_conv_w[j], od_dn_alog_f[j], od_dn_alog_b[j], od_dn_dtb_f[j],
                                     od_dn_dtb_b[j], od_dn_norm_g[j], od_w_out[j], not last)
        x = x + gt1 * y_lat
        x = x + gt2 * squared_relu_mlp(modulate(rms_norm(x, norm2_g[i]), sh2, sc2), mlp_w1[i], mlp_w2[i])
        if not last:
            ctx = ctx + mods_c[2] * y_ctx
            ctx = ctx + mods_c[5] * squared_relu_mlp(modulate(rms_norm(ctx, norm2_g[i]), mods_c[3], mods_c[4]),
                                                     mlp_w1[i], mlp_w2[i])
    return rms_norm(x, final_g)
```

```python
import functools
import math

import jax
import jax.numpy as jnp
import numpy as np
from jax import lax
from jax.experimental import pallas as pl
from jax.experimental.pallas import tpu as pltpu

F32 = jnp.float32
BF16 = jnp.bfloat16

EPS = 1e-6
N_MOD = 6
GRID_W = 64
ROPE_BASE = 10000.0
CONV_W = 31
DIFF_HEADS = 4
DIFF_DQK = 64
DIFF_DV = 128
HY_ORDER = 2
HY_EMB = 33
HY_FFN = 64
HY_MAX_DECAY = math.log(1e-2) / 0.3
HY_MIN_DECAY = math.log(1e-2) / 1.5
DN_HEADS = 4
DN_DK = 128
DN_CONV_W = 5
DN_CHUNK = 64

LANE = 128
SUBLANE = 8
VMEM_LIMIT = 56 * 1024 * 1024


def _cparams(*sem):
    return pltpu.CompilerParams(dimension_semantics=sem, vmem_limit_bytes=VMEM_LIMIT)


def _tile(n, pref, mult=SUBLANE):
    if n <= pref:
        return n
    t = (pref // mult) * mult
    while t > mult and n % t:
        t -= mult
    assert n % t == 0, (n, pref, mult)
    return t


def _const_spec(shape):
    nd = len(shape)
    return pl.BlockSpec(shape, lambda *_: (0,) * nd, pipeline_mode=pl.Buffered(1))


def _silu(x):
    return x * jax.nn.sigmoid(x)


def _norm_mod(x, g, sh, sc):
    y = x * lax.rsqrt(jnp.mean(x * x, axis=-1, keepdims=True) + EPS)
    return (y * g) * (1.0 + sc) + sh


def _mods_kernel(c_ref, w_ref, b_ref, o_ref):
    cond = _silu(c_ref[...])
    o_ref[...] = jnp.dot(cond, w_ref[...], preferred_element_type=F32,
                         precision=lax.Precision.HIGHEST) + b_ref[...]


def ada_mods(cvec, w, b):
    r, d = cvec.shape
    n = w.shape[1]
    tn = _tile(n, 512, LANE)
    return pl.pallas_call(
        _mods_kernel,
        grid=(n // tn,),
        in_specs=[pl.BlockSpec((r, d), lambda j: (0, 0)),
                  pl.BlockSpec((d, tn), lambda j: (0, j)),
                  pl.BlockSpec((1, tn), lambda j: (0, j))],
        out_specs=pl.BlockSpec((r, tn), lambda j: (0, j)),
        out_shape=jax.ShapeDtypeStruct((r, n), F32),
        compiler_params=_cparams("arbitrary"),
        name="ada_mods",
    )(cvec, w, b.reshape(1, n))


def _swap16(y):
    n = y.shape[-1]
    lane = lax.broadcasted_iota(jnp.int32, y.shape, y.ndim - 1)
    fwd = pltpu.roll(y, n - 16, y.ndim - 1)
    bwd = pltpu.roll(y, 16, y.ndim - 1)
    return jnp.where((lane % 32) < 16, fwd, bwd)


def _proj_kernel(x_ref, g_ref, sh_ref, sc_ref, w_ref, *rest, splits, rope_cols, tc):
    n_out = len(splits)
    if rope_cols:
        cq_ref, sq_ref, ck_ref, sk_ref = rest[:4]
        rest = rest[4:]
    o_refs = rest[:n_out]
    h = _norm_mod(x_ref[0], g_ref[...], sh_ref[0], sc_ref[0]).astype(BF16)
    for o_ref, (s0, s1) in zip(o_refs, splits):
        for c0 in range(s0, s1, tc):
            y = jnp.dot(h, w_ref[:, c0:c0 + tc], preferred_element_type=F32)
            if rope_cols and rope_cols[0] <= c0 < rope_cols[2]:
                is_q = c0 < rope_cols[1]
                cos = (cq_ref if is_q else ck_ref)[...]
                sin = (sq_ref if is_q else sk_ref)[...]
                reps = tc // LANE
                cos = jnp.concatenate([cos] * reps, axis=1)
                sin = jnp.concatenate([sin] * reps, axis=1)
                y = y * cos + _swap16(y) * sin
            o_ref[0, :, c0 - s0:c0 - s0 + tc] = y.astype(o_ref.dtype)


def norm_mod_proj(x, g, sh, sc, w, splits, dtypes, rope=None, rope_cols=None):
    b, l, d = x.shape
    n = w.shape[1]
    tm = _tile(l, 512)
    tc = 512 if all(s % 512 == 0 for sp in splits for s in sp) else LANE
    assert all(s % tc == 0 for sp in splits for s in sp)
    if rope_cols:
        assert all(c % tc == 0 for c in rope_cols)
    row = lambda bi, i: (bi, i, 0)
    in_specs = [pl.BlockSpec((1, tm, d), row),
                _const_spec((1, d)),
                pl.BlockSpec((1, 1, d), lambda bi, i: (bi, 0, 0)),
                pl.BlockSpec((1, 1, d), lambda bi, i: (bi, 0, 0)),
                _const_spec((d, n))]
    args = [x, g.reshape(1, d), sh, sc, w]
    if rope_cols:
        in_specs += [pl.BlockSpec((tm, LANE), lambda bi, i: (i, 0))] * 4
        args += list(rope)
    return pl.pallas_call(
        functools.partial(_proj_kernel, splits=tuple(splits), rope_cols=rope_cols, tc=tc),
        grid=(b, l // tm),
        in_specs=in_specs,
        out_specs=[pl.BlockSpec((1, tm, s1 - s0), row) for s0, s1 in splits],
        out_shape=[jax.ShapeDtypeStruct((b, l, s1 - s0), dt) for (s0, s1), dt in zip(splits, dtypes)],
        compiler_params=_cparams("parallel", "parallel"),
        name="norm_mod_proj",
    )(*args)


CONV_HALO = 16


def _conformer_kernel(ac_ref, gc_ref, ap_ref, gp_ref, an_ref, gn_ref, w_ref, b_ref, lg_ref, lb_ref, o_ref,
                      u_ref, y_ref, *, tl, ch):
    i = pl.program_id(1)
    last = pl.num_programs(1) - 1
    left = (CONV_W - 1) // 2
    glu = lambda a, g: a * jax.nn.sigmoid(g)
    u_ref[CONV_HALO:CONV_HALO + tl, :] = glu(ac_ref[0], gc_ref[0])
    prev = glu(ap_ref[0], gp_ref[0])
    nxt = glu(an_ref[0], gn_ref[0])
    u_ref[0:CONV_HALO, :] = jnp.where(i > 0, prev, 0.0)
    u_ref[CONV_HALO + tl:CONV_HALO + tl + CONV_HALO, :] = jnp.where(i < last, nxt, 0.0)
    rows = min(tl, 128)
    for r0 in range(0, tl, rows):
        for c0 in range(0, ch, LANE):
            acc = jnp.zeros((rows, LANE), F32) + b_ref[:, c0:c0 + LANE]
            for k in range(CONV_W):
                s = CONV_HALO - left + k + r0
                acc = acc + w_ref[k:k + 1, c0:c0 + LANE] * u_ref[s:s + rows, c0:c0 + LANE]
            y_ref[r0:r0 + rows, c0:c0 + LANE] = acc
    y = y_ref[...]
    mu = jnp.mean(y, axis=-1, keepdims=True)
    yc = y - mu
    var = jnp.mean(yc * yc, axis=-1, keepdims=True)
    z = yc * lax.rsqrt(var + EPS) * lg_ref[...] + lb_ref[...]
    o_ref[0] = _silu(z).astype(o_ref.dtype)


def conformer_conv(p, conv_w, conv_b, ln_g, ln_b):
    b, l, _ = p.shape
    ch = conv_w.shape[1]
    tl = _tile(l, 256, CONV_HALO)
    nh = tl // CONV_HALO
    nblk = l // CONV_HALO
    cur = lambda col: pl.BlockSpec((1, tl, ch), lambda bi, i: (bi, i, col))
    prv = lambda col: pl.BlockSpec((1, CONV_HALO, ch), lambda bi, i: (bi, jnp.maximum(i * nh - 1, 0), col))
    nxt = lambda col: pl.BlockSpec((1, CONV_HALO, ch), lambda bi, i: (bi, jnp.minimum((i + 1) * nh, nblk - 1), col))
    return pl.pallas_call(
        functools.partial(_conformer_kernel, tl=tl, ch=ch),
        grid=(b, l // tl),
        in_specs=[cur(0), cur(1), prv(0), prv(1), nxt(0), nxt(1),
                  _const_spec((CONV_W, ch)), _const_spec((1, ch)), _const_spec((1, ch)), _const_spec((1, ch))],
        out_specs=pl.BlockSpec((1, tl, ch), lambda bi, i: (bi, i, 0)),
        out_shape=jax.ShapeDtypeStruct((b, l, ch), BF16),
        scratch_shapes=[pltpu.VMEM((tl + 2 * CONV_HALO, ch), F32), pltpu.VMEM((tl, ch), F32)],
        compiler_params=_cparams("parallel", "parallel"),
        name="conformer_conv",
    )(p, p, p, p, p, p, conv_w, conv_b.reshape(1, ch), ln_g.reshape(1, ch), ln_b.reshape(1, ch))


def _diff_attn_kernel(q_ref, k_ref, v_ref, lam_ref, g_ref, o_ref, q1_ref, q2_ref, m_ref, l_ref, acc_ref,
                      *, lambda_init):
    kv = pl.program_id(3)

    @pl.when(kv == 0)
    def _():
        q = q_ref[0]
        lane = lax.broadcasted_iota(jnp.int32, q.shape, 1)
        zero = jnp.zeros_like(q)
        q1_ref[...] = jnp.where(lane < DIFF_DQK, q, zero)
        q2_ref[...] = jnp.where(lane < DIFF_DQK, zero, q)
        m_ref[...] = jnp.full(m_ref.shape, -jnp.inf, F32)
        l_ref[...] = jnp.zeros(l_ref.shape, F32)
        acc_ref[...] = jnp.zeros(acc_ref.shape, F32)

    k = k_ref[0]
    v = v_ref[0]
    for mi, qm_ref in enumerate((q1_ref, q2_ref)):
        s = lax.dot_general(qm_ref[...], k, (((1,), (1,)), ((), ())), preferred_element_type=F32)
        m_old = m_ref[mi]
        m_new = jnp.maximum(m_old, jnp.max(s, axis=-1, keepdims=True))
        alpha = jnp.exp(m_old - m_new)
        e = jnp.exp(s - m_new)
        l_ref[mi] = alpha * l_ref[mi] + jnp.sum(e, axis=-1, keepdims=True)
        acc_ref[mi] = alpha * acc_ref[mi] + jnp.dot(e.astype(BF16), v, preferred_element_type=F32)
        m_ref[mi] = m_new

    @pl.when(kv == pl.num_programs(3) - 1)
    def _():
        lam = (jnp.exp(jnp.sum(lam_ref[0:1, :] * lam_ref[1:2, :], axis=-1, keepdims=True))
               - jnp.exp(jnp.sum(lam_ref[2:3, :] * lam_ref[3:4, :], axis=-1, keepdims=True)) + lambda_init)
        o = acc_ref[0] / l_ref[0] - lam * (acc_ref[1] / l_ref[1])
        o = o * lax.rsqrt(jnp.mean(o * o, axis=-1, keepdims=True) + EPS) * g_ref[...]
        o_ref[0] = (o * (1.0 - lambda_init)).astype(o_ref.dtype)


def diff_attention(q, k, v, lam_vecs, subln_g, lambda_init, cols):
    b, lq, _ = q.shape
    lk = k.shape[1]
    tq = _tile(lq, 512)
    tk = _tile(lk, 1024, LANE)
    qb, kb, vb = cols
    return pl.pallas_call(
        functools.partial(_diff_attn_kernel, lambda_init=lambda_init),
        grid=(b, DIFF_HEADS, lq // tq, lk // tk),
        in_specs=[pl.BlockSpec((1, tq, LANE), lambda bi, h, i, j: (bi, i, qb + h)),
                  pl.BlockSpec((1, tk, LANE), lambda bi, h, i, j: (bi, j, kb + h)),
                  pl.BlockSpec((1, tk, LANE), lambda bi, h, i, j: (bi, j, vb + h)),
                  _const_spec((4, DIFF_DQK)), _const_spec((1, DIFF_DV))],
        out_specs=pl.BlockSpec((1, tq, LANE), lambda bi, h, i, j: (bi, i, h)),
        out_shape=jax.ShapeDtypeStruct((b, lq, DIFF_HEADS * DIFF_DV), BF16),
        scratch_shapes=[pltpu.VMEM((tq, LANE), BF16), pltpu.VMEM((tq, LANE), BF16),
                        pltpu.VMEM((2, tq, 1), F32), pltpu.VMEM((2, tq, 1), F32),
                        pltpu.VMEM((2, tq, DIFF_DV), F32)],
        compiler_params=_cparams("parallel", "parallel", "parallel", "arbitrary"),
        name="diff_attention",
    )(q, k, v, lam_vecs, subln_g.reshape(1, DIFF_DV))


def _out_mlp_kernel(x_ref, a_ref, b_ref, wo_ref, gt1_ref, g2_ref, sh2_ref, sc2_ref, gt2_ref, w1_ref, w2_ref,
                    fg_ref, o_ref, *, a_transposed, hidden_chunk, final_norm):
    half = wo_ref.shape[0] // 2
    if a_transposed:
        ya = lax.dot_general(a_ref[0], wo_ref[0:half, :], (((0,), (0,)), ((), ())), preferred_element_type=F32)
    else:
        ya = jnp.dot(a_ref[0], wo_ref[0:half, :], preferred_element_type=F32)
    y = ya + jnp.dot(b_ref[0], wo_ref[half:, :], preferred_element_type=F32)
    x1 = x_ref[0] + gt1_ref[0] * y
    h = _norm_mod(x1, g2_ref[...], sh2_ref[0], sc2_ref[0]).astype(BF16)
    acc = jnp.zeros_like(x1)
    for c0 in range(0, w1_ref.shape[1], hidden_chunk):
        t = jnp.maximum(jnp.dot(h, w1_ref[:, c0:c0 + hidden_chunk], preferred_element_type=F32), 0.0)
        acc = acc + jnp.dot((t * t).astype(BF16), w2_ref[c0:c0 + hidden_chunk, :], preferred_element_type=F32)
    x2 = x1 + gt2_ref[0] * acc
    if final_norm:
        x2 = x2 * lax.rsqrt(jnp.mean(x2 * x2, axis=-1, keepdims=True) + EPS) * fg_ref[...]
    o_ref[0] = x2


def out_proj_mlp(x, mix_a, mix_b, w_out, gt1, g2, sh2, sc2, gt2, w1, w2, final_g=None, a_transposed=False):
    b, l, d = x.shape
    c = mix_b.shape[2]
    hid = w1.shape[1]
    tm = _tile(l, 512, LANE)
    row = lambda bi, i: (bi, i, 0)
    vec = lambda bi, i: (bi, 0, 0)
    a_spec = (pl.BlockSpec((1, c, tm), lambda bi, i: (bi, 0, i)) if a_transposed
              else pl.BlockSpec((1, tm, c), row))
    fg = jnp.ones((1, d), F32) if final_g is None else final_g.reshape(1, d)
    return pl.pallas_call(
        functools.partial(_out_mlp_kernel, a_transposed=a_transposed, hidden_chunk=min(hid, 1024),
                          final_norm=final_g is not None),
        grid=(b, l // tm),
        in_specs=[pl.BlockSpec((1, tm, d), row), a_spec, pl.BlockSpec((1, tm, c), row),
                  _const_spec((2 * c, d)),
                  pl.BlockSpec((1, 1, d), vec), _const_spec((1, d)),
                  pl.BlockSpec((1, 1, d), vec), pl.BlockSpec((1, 1, d), vec), pl.BlockSpec((1, 1, d), vec),
                  _const_spec((d, hid)), _const_spec((hid, d)), _const_spec((1, d))],
        out_specs=pl.BlockSpec((1, tm, d), row),
        out_shape=jax.ShapeDtypeStruct((b, l, d), F32),
        compiler_params=_cparams("parallel", "parallel"),
        name="out_proj_mlp",
    )(x, mix_a, mix_b, w_out, gt1, g2.reshape(1, d), sh2, sc2, gt2, w1, w2, fg)


def _rope_tables(length, scale):
    rows = length // GRID_W
    row = jnp.repeat(jnp.arange(rows, dtype=F32), GRID_W)
    col = jnp.tile(jnp.arange(GRID_W, dtype=F32), rows)
    n_freq = DIFF_DQK // 4
    inv = ROPE_BASE ** (-jnp.arange(n_freq, dtype=F32) / n_freq)
    ang_r = row[:, None] * inv
    ang_c = col[:, None] * inv
    cos = jnp.concatenate([jnp.cos(ang_r)] * 2 + [jnp.cos(ang_c)] * 2, axis=-1)
    sin = jnp.concatenate([-jnp.sin(ang_r), jnp.sin(ang_r), -jnp.sin(ang_c), jnp.sin(ang_c)], axis=-1)
    cos = jnp.concatenate([cos, cos], axis=-1) * scale
    sin = jnp.concatenate([sin, sin], axis=-1) * scale
    return cos, sin


def _flat_tables(length, scale):
    return jnp.full((length, LANE), scale, F32), jnp.zeros((length, LANE), F32)


def _proj_t_kernel(x_ref, g_ref, sh_ref, sc_ref, wt_ref, o_ref, *, rc):
    h = _norm_mod(x_ref[0], g_ref[...], sh_ref[0], sc_ref[0]).astype(BF16)
    for r0 in range(0, wt_ref.shape[0], rc):
        o_ref[0, r0:r0 + rc, :] = lax.dot_general(wt_ref[r0:r0 + rc, :], h, (((1,), (1,)), ((), ())),
                                                  preferred_element_type=F32)


def norm_mod_proj_t(x, g, sh, sc, wt):
    b, l, d = x.shape
    c = wt.shape[0]
    tl = _tile(l, 512, LANE)
    return pl.pallas_call(
        functools.partial(_proj_t_kernel, rc=_tile(c, 512)),
        grid=(b, l // tl),
        in_specs=[pl.BlockSpec((1, tl, d), lambda bi, i: (bi, i, 0)),
                  _const_spec((1, d)),
                  pl.BlockSpec((1, 1, d), lambda bi, i: (bi, 0, 0)),
                  pl.BlockSpec((1, 1, d), lambda bi, i: (bi, 0, 0)),
                  _const_spec((c, d))],
        out_specs=pl.BlockSpec((1, c, tl), lambda bi, i: (bi, 0, i)),
        out_shape=jax.ShapeDtypeStruct((b, c, l), F32),
        compiler_params=_cparams("parallel", "parallel"),
        name="norm_mod_proj_t",
    )(x, g.reshape(1, d), sh, sc, wt)


def _dot_hi(a, b):
    return jnp.dot(a, b, preferred_element_type=F32, precision=lax.Precision.HIGHEST)


def _hy_hidden_kernel(z_ref, w1_ref, b1_ref, f1_ref, w2_ref, b2_ref, f2_ref, o_ref):
    hid = jnp.sin(f1_ref[...] * (_dot_hi(w1_ref[...], z_ref[...]) + b1_ref[...]))
    o_ref[...] = jnp.sin(f2_ref[...] * (_dot_hi(w2_ref[...], hid) + b2_ref[...]))


def hyena_hidden(zt, w1t, b1, f1, w2t, b2, f2):
    e, n = zt.shape
    f = w1t.shape[0]
    tn = _tile(n, 2048, LANE)
    col = lambda v: v.reshape(f, 1)
    return pl.pallas_call(
        _hy_hidden_kernel,
        grid=(n // tn,),
        in_specs=[pl.BlockSpec((e, tn), lambda j: (0, j)),
                  _const_spec((f, e)), _const_spec((f, 1)), _const_spec((f, 1)),
                  _const_spec((f, f)), _const_spec((f, 1)), _const_spec((f, 1))],
        out_specs=pl.BlockSpec((f, tn), lambda j: (0, j)),
        out_shape=jax.ShapeDtypeStruct((f, n), F32),
        compiler_params=_cparams("parallel"),
        name="hyena_hidden",
    )(zt, w1t, col(b1), col(f1), w2t, col(b2), col(f2))


def _hy_filter_kernel(hid_ref, t_ref, wf_ref, wb_ref, dl_ref, o_ref, *, length, tn):
    n2 = hid_ref.shape[1]
    delta = dl_ref[...]

    def piece(c0):
        hid = hid_ref[:, c0:c0 + tn]
        dec = jnp.exp(-t_ref[:, c0:c0 + tn] * delta)
        return _dot_hi(wf_ref[...], hid) * dec, _dot_hi(wb_ref[...], hid) * dec

    norm = jnp.zeros((wf_ref.shape[0], 1), F32)
    for c0 in range(length, n2, tn):
        ff, fb = piece(c0)
        norm = norm + jnp.sum(jnp.abs(ff) + jnp.abs(fb), axis=-1, keepdims=True)
    inv = 1.0 / norm
    for c0 in range(0, n2, tn):
        ff, fb = piece(c0)
        if c0 >= length:
            o_ref[0, :, c0:c0 + tn] = ff * inv
        else:
            col = lax.broadcasted_iota(jnp.int32, fb.shape, 1) + c0
            o_ref[0, :, c0:c0 + tn] = jnp.where(col == 0, 0.0, fb * inv)


def hyena_filters(hid, t_row, w3t, deltas, length):
    f, n2 = hid.shape
    ch = deltas.shape[0]
    cb = LANE
    nblk = ch // cb
    tn = _tile(length, 2048, LANE)
    return pl.pallas_call(
        functools.partial(_hy_filter_kernel, length=length, tn=tn),
        grid=(HY_ORDER, nblk),
        in_specs=[_const_spec((f, n2)), _const_spec((1, n2)),
                  pl.BlockSpec((cb, f), lambda o, c: (o * 2 * nblk + c, 0)),
                  pl.BlockSpec((cb, f), lambda o, c: (o * 2 * nblk + nblk + c, 0)),
                  pl.BlockSpec((cb, 1), lambda o, c: (c, 0))],
        out_specs=pl.BlockSpec((1, cb, n2), lambda o, c: (o, c, 0)),
        out_shape=jax.ShapeDtypeStruct((HY_ORDER, ch, n2), F32),
        compiler_params=_cparams("parallel", "parallel"),
        name="hyena_filters",
    )(hid, t_row, w3t, w3t, deltas.reshape(ch, 1))


HY_T = 256
HY_CB = SUBLANE


def _hyena_kernel(v_ref, x1_ref, x2_ref, g_ref, sw_ref, sb_ref, sk_ref_s, o_ref, sk_ref, u_ref, acc_ref,
                  *, length, nch):
    nb = v_ref.shape[0]
    nj = length // HY_T
    cblk = pl.program_id(0)
    lane = lax.broadcasted_iota(jnp.int32, (nb, length), 1)

    def short(ref, cc, part):
        x = ref[:, cc, :]
        ch = part * nch + cblk * HY_CB + cc
        prev = jnp.where(lane == 0, 0.0, pltpu.roll(x, 1, 1))
        nxt = jnp.where(lane == length - 1, 0.0, pltpu.roll(x, length - 1, 1))
        return sw_ref[0, ch] * prev + sw_ref[1, ch] * x + sw_ref[2, ch] * nxt + sb_ref[ch]

    def long_conv(u, order, cc):
        g = jnp.broadcast_to(g_ref[order, pl.ds(cc, 1), :], (SUBLANE, 2 * length))
        half = HY_T // 2
        for a2 in range(half // (2 * SUBLANE)):
            lo = pltpu.roll(g, 2 * a2 * SUBLANE, 1, stride=1, stride_axis=0)
            hi = pltpu.roll(g, (2 * a2 + 1) * SUBLANE, 1, stride=1, stride_axis=0)
            pair = jnp.concatenate([lo, hi], axis=0).astype(BF16)
            r0 = 2 * a2 * SUBLANE
            sk_ref[r0:r0 + 2 * SUBLANE, :] = pair
            sk_ref[half + r0:half + r0 + 2 * SUBLANE, half:] = pair[:, :2 * length - half]
        for j in range(nj):
            u_ref[j * nb:(j + 1) * nb, :] = u[:, j * HY_T:(j + 1) * HY_T]
        acc_ref[...] = jnp.zeros(acc_ref.shape, F32)
        for d in range(-(nj - 1), nj):
            m = (nj - abs(d)) * nb
            lo_in, lo_out = max(0, -d) * nb, max(0, d) * nb
            tile = sk_ref[:, length + d * HY_T:length + (d + 1) * HY_T]
            acc_ref[lo_out:lo_out + m, :] += jnp.dot(u_ref[lo_in:lo_in + m, :].astype(BF16), tile,
                                                     preferred_element_type=F32)
        return jnp.concatenate([acc_ref[i * nb:(i + 1) * nb, :] for i in range(nj)], axis=1)

    def channel(cc, carry):
        v = short(v_ref, cc, 0)
        x1 = short(x1_ref, cc, 1)
        x2 = short(x2_ref, cc, 2)
        ch = cblk * HY_CB + cc
        z = x1 * (long_conv(v, 0, cc) + sk_ref_s[0, ch] * v)
        z = x2 * (long_conv(z, 1, cc) + sk_ref_s[1, ch] * z)
        o_ref[:, cc, :] = z
        return carry

    lax.fori_loop(0, HY_CB, channel, 0)


def hyena_mix(pt, filt, short_w, short_b, skip):
    b, c3, l = pt.shape
    nch = c3 // 3
    assert l % HY_T == 0 and nch % HY_CB == 0 and b == SUBLANE
    nblk = nch // HY_CB
    smem = pl.BlockSpec(memory_space=pltpu.SMEM)
    part = lambda k: pl.BlockSpec((b, HY_CB, l), lambda c: (0, k * nblk + c, 0))
    return pl.pallas_call(
        functools.partial(_hyena_kernel, length=l, nch=nch),
        grid=(nblk,),
        in_specs=[part(0), part(1), part(2),
                  pl.BlockSpec((HY_ORDER, HY_CB, 2 * l), lambda c: (0, c, 0)),
                  smem, smem, smem],
        out_specs=pl.BlockSpec((b, HY_CB, l), lambda c: (0, c, 0)),
        out_shape=jax.ShapeDtypeStruct((b, nch, l), F32),
        scratch_shapes=[pltpu.VMEM((HY_T, 2 * l), BF16), pltpu.VMEM((l // HY_T * b, HY_T), F32),
                        pltpu.VMEM((l // HY_T * b, HY_T), F32)],
        compiler_params=_cparams("parallel"),
        name="hyena_mix",
    )(pt, pt, pt, filt, short_w, short_b, skip)


DN_HALO = SUBLANE
DN_BETA_F, DN_BETA_B, DN_G_F, DN_G_B = 0, DN_HEADS, 2 * DN_HEADS, 3 * DN_HEADS


def _dn_feat_kernel(c_ref, p_ref, n_ref, s_ref, w_ref, al_ref, dt_ref, q_ref, k_ref, v_ref, f_ref, u_ref, *, tl):
    i = pl.program_id(1)
    last = pl.num_programs(1) - 1
    left = (DN_CONV_W - 1) // 2
    nc = c_ref.shape[2]
    u_ref[DN_HALO:DN_HALO + tl, :] = c_ref[0]
    u_ref[0:DN_HALO, :] = jnp.where(i > 0, p_ref[0], 0.0)
    u_ref[DN_HALO + tl:DN_HALO + tl + DN_HALO, :] = jnp.where(i < last, n_ref[0], 0.0)
    rows = min(tl, 128)
    outs = (q_ref, k_ref, v_ref)
    per = nc // len(outs)
    for r0 in range(0, tl, rows):
        for c0 in range(0, nc, LANE):
            acc = jnp.zeros((rows, LANE), F32)
            for k in range(DN_CONV_W):
                s = DN_HALO - left + k + r0
                acc = acc + w_ref[k:k + 1, c0:c0 + LANE] * u_ref[s:s + rows, c0:c0 + LANE]
            y = _silu(acc)
            which, off = divmod(c0, per)
            if which < 2:
                y = y * lax.rsqrt(jnp.sum(y * y, axis=-1, keepdims=True) + 1e-6)
            outs[which][0, r0:r0 + rows, off:off + LANE] = y
    s = s_ref[0]
    lane = lax.broadcasted_iota(jnp.int32, s.shape, 1)
    gate = -jnp.exp(al_ref[...]) * jax.nn.softplus(s + dt_ref[...])
    f_ref[0] = jnp.where(lane < DN_G_F, jax.nn.sigmoid(s), gate)


def dn_features(qkv, small, conv_w, alog_row, dtb_row):
    b, l, nc = qkv.shape
    tl = _tile(l, 256, DN_HALO)
    nh = tl // DN_HALO
    nblk = l // DN_HALO
    w = nc // 3
    row = lambda bi, i: (bi, i, 0)
    return pl.pallas_call(
        functools.partial(_dn_feat_kernel, tl=tl),
        grid=(b, l // tl),
        in_specs=[pl.BlockSpec((1, tl, nc), row),
                  pl.BlockSpec((1, DN_HALO, nc), lambda bi, i: (bi, jnp.maximum(i * nh - 1, 0), 0)),
                  pl.BlockSpec((1, DN_HALO, nc), lambda bi, i: (bi, jnp.minimum((i + 1) * nh, nblk - 1), 0)),
                  pl.BlockSpec((1, tl, LANE), row),
                  _const_spec((DN_CONV_W, nc)), _const_spec((1, LANE)), _const_spec((1, LANE))],
        out_specs=[pl.BlockSpec((1, tl, w), row)] * 3 + [pl.BlockSpec((1, tl, LANE), row)],
        out_shape=[jax.ShapeDtypeStruct((b, l, w), F32)] * 3 + [jax.ShapeDtypeStruct((b, l, LANE), F32)],
        scratch_shapes=[pltpu.VMEM((tl + 2 * DN_HALO, nc), F32)],
        compiler_params=_cparams("parallel", "parallel"),
        name="dn_features",
    )(qkv, qkv, qkv, small, conv_w, alog_row, dtb_row)


def _bdot(a, b):
    return jnp.dot(a.astype(BF16), b.astype(BF16), preferred_element_type=F32)


def _bdot_nt(a, b):
    return lax.dot_general(a.astype(BF16), b.astype(BF16), (((1,), (1,)), ((), ())), preferred_element_type=F32)


def _bdot_tn(a, b):
    return lax.dot_general(a.astype(BF16), b.astype(BF16), (((0,), (0,)), ((), ())), preferred_element_type=F32)


def _dot3(a, b):
    a_hi, b_hi = a.astype(BF16), b.astype(BF16)
    a_lo, b_lo = (a - a_hi.astype(F32)).astype(BF16), (b - b_hi.astype(F32)).astype(BF16)
    dot = lambda p, q: jnp.dot(p, q, preferred_element_type=F32)
    return dot(a_hi, b_hi) + (dot(a_hi, b_lo) + dot(a_lo, b_hi))


def _cumsum_rows(x, reverse):
    n = x.shape[0]
    row = lax.broadcasted_iota(jnp.int32, x.shape, 0)
    s = 1
    while s < n:
        if reverse:
            x = x + jnp.where(row < n - s, pltpu.roll(x, n - s, 0), 0.0)
        else:
            x = x + jnp.where(row >= s, pltpu.roll(x, s, 0), 0.0)
        s *= 2
    return x


def _dn_chunk(q, k, v, beta, gcol, grow, state, reverse):
    c = k.shape[0]
    ri = lax.broadcasted_iota(jnp.int32, (c, c), 0)
    ci = lax.broadcasted_iota(jnp.int32, (c, c), 1)
    incl = (ri <= ci) if reverse else (ri >= ci)
    strict = (ri < ci) if reverse else (ri > ci)
    decay = jnp.where(incl, jnp.exp(jnp.where(incl, gcol - grow, 0.0)), 0.0)
    kb = k * beta
    lower = jnp.where(strict, _bdot_nt(kb, k) * decay, 0.0)
    pw = -lower
    ymat = pw
    for _ in range(int(math.log2(c)) - 1):
        pw = _dot3(pw, pw)
        ymat = ymat + pw + _dot3(ymat, pw)
    e_g = jnp.exp(gcol)
    rhs = jnp.concatenate([v * beta, kb * e_g], axis=1)
    uw = rhs + _bdot(ymat, rhs)
    dv = v.shape[1]
    u, w = uw[:, :dv], uw[:, dv:]
    g_last = gcol[0:1, :] if reverse else gcol[c - 1:c, :]
    k_dec = k * jnp.exp(g_last - gcol)
    qs = q * (k.shape[1] ** -0.5)
    attn = _bdot_nt(qs, k) * decay
    ws = _bdot(jnp.concatenate([w, qs * e_g], axis=0), state)
    v_new = u - ws[:c]
    o = ws[c:] + _bdot(attn, v_new)
    new_state = state * jnp.exp(g_last) + _bdot_tn(k_dec, v_new)
    return o, new_state


def _dn_scan_kernel(qf_ref, kf_ref, vf_ref, ff_ref, qb_ref, kb_ref, vb_ref, fb_ref, s0f_ref, s0b_ref,
                    of_ref, ob_ref, sf_ref, sb_ref, st_ref):
    n = pl.program_id(1)

    @pl.when(n == 0)
    def _():
        st_ref[0] = s0f_ref[0]
        st_ref[1] = s0b_ref[0]

    dirs = ((qf_ref, kf_ref, vf_ref, ff_ref, of_ref, False), (qb_ref, kb_ref, vb_ref, fb_ref, ob_ref, True))
    for di, (q_ref, k_ref, v_ref, f_ref, o_ref, reverse) in enumerate(dirs):
        feats = f_ref[0]
        gc = _cumsum_rows(feats, reverse)
        gct = gc.T
        for h in range(DN_HEADS):
            cols = slice(h * DN_DK, (h + 1) * DN_DK)
            bl = (DN_BETA_B if reverse else DN_BETA_F) + h
            gl = (DN_G_B if reverse else DN_G_F) + h
            o, s_new = _dn_chunk(q_ref[0, :, cols], k_ref[0, :, cols], v_ref[0, :, cols],
                                 feats[:, bl:bl + 1], gc[:, gl:gl + 1], gct[gl:gl + 1, :], st_ref[di, h], reverse)
            o_ref[0, :, cols] = o
            st_ref[di, h] = s_new

    @pl.when(n == pl.num_programs(1) - 1)
    def _():
        sf_ref[0] = st_ref[0]
        sb_ref[0] = st_ref[1]


def dn_scan(q, k, v, feats, s0f, s0b):
    b, l, w = q.shape
    nchunk = l // DN_CHUNK
    fwd = lambda bi, n: (bi, n, 0)
    bwd = lambda bi, n: (bi, nchunk - 1 - n, 0)
    st = lambda bi, n: (bi, 0, 0, 0)
    blk = lambda im, width: pl.BlockSpec((1, DN_CHUNK, width), im)
    st_spec = pl.BlockSpec((1, DN_HEADS, DN_DK, DN_DK), st)
    st_shape = jax.ShapeDtypeStruct((b, DN_HEADS, DN_DK, DN_DK), F32)
    return pl.pallas_call(
        _dn_scan_kernel,
        grid=(b, nchunk),
        in_specs=[blk(fwd, w), blk(fwd, w), blk(fwd, w), blk(fwd, LANE),
                  blk(bwd, w), blk(bwd, w), blk(bwd, w), blk(bwd, LANE), st_spec, st_spec],
        out_specs=[blk(fwd, w), blk(bwd, w), st_spec, st_spec],
        out_shape=[jax.ShapeDtypeStruct((b, l, w), F32)] * 2 + [st_shape, st_shape],
        scratch_shapes=[pltpu.VMEM((2, DN_HEADS, DN_DK, DN_DK), F32)],
        compiler_params=_cparams("parallel", "arbitrary"),
        name="dn_scan",
    )(q, k, v, feats, q, k, v, feats, s0f, s0b)


def _dn_out_kernel(of_ref, ob_ref, gate_ref, g_ref, o_ref):
    o = of_ref[0] + ob_ref[0]
    for c0 in range(0, o.shape[1], DN_DK):
        t = o[:, c0:c0 + DN_DK]
        t = t * lax.rsqrt(jnp.mean(t * t, axis=-1, keepdims=True) + EPS) * g_ref[...]
        o_ref[0, :, c0:c0 + DN_DK] = (t * _silu(gate_ref[0, :, c0:c0 + DN_DK])).astype(o_ref.dtype)


def dn_output(o_f, o_b, gate, norm_g):
    b, l, w = o_f.shape
    tl = _tile(l, 512)
    row = lambda bi, i: (bi, i, 0)
    return pl.pallas_call(
        _dn_out_kernel,
        grid=(b, l // tl),
        in_specs=[pl.BlockSpec((1, tl, w), row)] * 3 + [_const_spec((1, DN_DK))],
        out_specs=pl.BlockSpec((1, tl, w), row),
        out_shape=jax.ShapeDtypeStruct((b, l, w), BF16),
        compiler_params=_cparams("parallel", "parallel"),
        name="dn_output",
    )(o_f, o_b, gate, norm_g.reshape(1, DN_DK))


def _split_mods(mods, b, d):
    lat = [mods[:b, k * d:(k + 1) * d].reshape(b, 1, d) for k in range(N_MOD)]
    ctx = [jnp.broadcast_to(mods[b, k * d:(k + 1) * d].reshape(1, 1, d), (b, 1, d)) for k in range(N_MOD)]
    return lat, ctx


def _even_layer(x, ctx, lat_m, ctx_m, n1g, n2g, w_in, conv_w, conv_b, ln_g, ln_b, lam_vecs, subln_g, w_out,
                w1, w2, lambda_init):
    b, l, d = x.shape
    lc = ctx.shape[1]
    ch = conv_w.shape[1]
    qk = DIFF_HEADS * 2 * DIFF_DQK
    q0, k0, v0, n_in = 2 * ch, 2 * ch + qk, 2 * ch + 2 * qk, 2 * ch + 2 * qk + DIFF_HEADS * DIFF_DV
    scale = DIFF_DQK ** -0.5
    splits = [(0, q0), (q0, n_in)]
    w_in = w_in.astype(BF16)
    rope_lat = _rope_tables(l, scale) + _rope_tables(l, 1.0)
    rope_ctx = _flat_tables(lc, scale) + _flat_tables(lc, 1.0)
    pc_lat, qkv_lat = norm_mod_proj(x, n1g, lat_m[0], lat_m[1], w_in, splits, (F32, BF16), rope_lat, (q0, k0, v0))
    pc_ctx, qkv_ctx = norm_mod_proj(ctx, n1g, ctx_m[0], ctx_m[1], w_in, splits, (F32, BF16), rope_ctx, (q0, k0, v0))
    conv_lat = conformer_conv(pc_lat, conv_w, conv_b, ln_g, ln_b)
    conv_ctx = conformer_conv(pc_ctx, conv_w, conv_b, ln_g, ln_b)
    kv_all = jnp.concatenate([qkv_lat[:, :, qk:], qkv_ctx[:, :, qk:]], axis=1)
    nb = qk // LANE
    o_lat = diff_attention(qkv_lat, kv_all, kv_all, lam_vecs, subln_g, lambda_init, (0, 0, nb))
    o_ctx = diff_attention(qkv_ctx, qkv_ctx, qkv_ctx, lam_vecs, subln_g, lambda_init, (0, nb, 2 * nb))
    w_out, w1, w2 = w_out.astype(BF16), w1.astype(BF16), w2.astype(BF16)
    x = out_proj_mlp(x, conv_lat, o_lat, w_out, lat_m[2], n2g, lat_m[3], lat_m[4], lat_m[5], w1, w2)
    ctx = out_proj_mlp(ctx, conv_ctx, o_ctx, w_out, ctx_m[2], n2g, ctx_m[3], ctx_m[4], ctx_m[5], w1, w2)
    return x, ctx


def _hyena_position_tables(length):
    t = jnp.linspace(0.0, 1.0, length, dtype=F32)[:, None]
    bands = (HY_EMB - 1) // 2
    omega = 2.0 * math.pi * jnp.arange(length, dtype=F32)[:, None] / length
    ang = omega * jnp.linspace(1e-4, bands - 1, bands, dtype=F32)
    z = jnp.concatenate([t, jnp.cos(ang), -jnp.sin(ang)], axis=-1)
    pos = jnp.minimum(jnp.abs(jnp.arange(2 * length) - length), length - 1)
    pad = -HY_EMB % SUBLANE
    zt = jnp.pad(z[pos].T, ((0, pad), (0, 0)))
    return zt, t[pos].T


def _odd_layer_last(x, ctx, lat_m, ctx_m, n1g, n2g, w_in, hy_short_w, hy_short_b, hy_w1, hy_b1, hy_f1, hy_w2, hy_b2,
                    hy_f2, hy_w3, hy_skip, dn_conv_w, alog_f, alog_b, dtb_f, dtb_b, dn_norm_g, w_out, w1, w2,
                    final_g):
    b, l, d = x.shape
    hy_in = hy_short_w.shape[1]
    hy_ch = hy_in // (HY_ORDER + 1)
    dn_w = DN_HEADS * DN_DK
    n_rest = w_in.shape[1] - hy_in
    n_pad = -n_rest % LANE
    w_rest = jnp.pad(w_in[:, hy_in:], ((0, 0), (0, n_pad))).astype(BF16)
    splits = [(0, dn_w), (dn_w, 4 * dn_w), (4 * dn_w, n_rest + n_pad)]
    f32x3 = (F32, F32, F32)
    gate_lat, qkv_lat, small_lat = norm_mod_proj(x, n1g, lat_m[0], lat_m[1], w_rest, splits, f32x3)
    _, qkv_ctx, small_ctx = norm_mod_proj(ctx, n1g, ctx_m[0], ctx_m[1], w_rest, splits, f32x3)

    pt = norm_mod_proj_t(x, n1g, lat_m[0], lat_m[1], w_in[:, :hy_in].T.astype(BF16))
    zt, t_row = _hyena_position_tables(l)
    w1t = jnp.pad(hy_w1.T, ((0, 0), (0, zt.shape[0] - hy_w1.shape[0])))
    hid = hyena_hidden(zt, w1t, hy_b1, hy_f1, hy_w2.T, hy_b2, hy_f2)
    deltas = jnp.abs(jnp.linspace(HY_MIN_DECAY, HY_MAX_DECAY, hy_ch, dtype=F32))
    filt = hyena_filters(hid, t_row, hy_w3.T, deltas, l)
    hy = hyena_mix(pt, filt, hy_short_w, hy_short_b, hy_skip)

    lanes = jnp.zeros((LANE,), F32)
    alog_row = lanes.at[DN_G_F:DN_G_F + DN_HEADS].set(alog_f).at[DN_G_B:DN_G_B + DN_HEADS].set(alog_b).reshape(1, LANE)
    dtb_row = lanes.at[DN_G_F:DN_G_F + DN_HEADS].set(dtb_f).at[DN_G_B:DN_G_B + DN_HEADS].set(dtb_b).reshape(1, LANE)
    q_c, k_c, v_c, f_c = dn_features(qkv_ctx, small_ctx, dn_conv_w, alog_row, dtb_row)
    q_l, k_l, v_l, f_l = dn_features(qkv_lat, small_lat, dn_conv_w, alog_row, dtb_row)
    s0 = jnp.zeros((b, DN_HEADS, DN_DK, DN_DK), F32)
    _, _, s_cf, s_cb = dn_scan(q_c, k_c, v_c, f_c, s0, s0)
    o_f, o_b, _, _ = dn_scan(q_l, k_l, v_l, f_l, s_cf, s_cb)
    dn = dn_output(o_f, o_b, gate_lat, dn_norm_g)

    return out_proj_mlp(x, hy, dn, w_out.astype(BF16), lat_m[2], n2g, lat_m[3], lat_m[4], lat_m[5],
                        w1.astype(BF16), w2.astype(BF16), final_g=final_g, a_transposed=True)


def kernel(x, c, ctx, c_ctx, ada_w, ada_b, norm1_g, norm2_g, mlp_w1, mlp_w2, ev_w_in, ev_conv_w, ev_conv_b, ev_ln_g,
           ev_ln_b, ev_lq1, ev_lk1, ev_lq2, ev_lk2, ev_subln_g, ev_w_out, od_w_in, od_hy_short_w, od_hy_short_b,
           od_hy_w1, od_hy_b1, od_hy_freq1, od_hy_w2, od_hy_b2, od_hy_freq2, od_hy_w3, od_hy_skip, od_dn_conv_w,
           od_dn_alog_f, od_dn_alog_b, od_dn_dtb_f, od_dn_dtb_b, od_dn_norm_g, od_w_out, final_g):
    b, _, d = x.shape
    assert ada_w.shape[0] == 2, "layer 0 = conformer/diff-attention, layer 1 (last) = Hyena/DeltaNet"
    rows = -(-(b + 1) // SUBLANE) * SUBLANE
    cvec = jnp.zeros((rows, d), F32).at[:b].set(c).at[b].set(c_ctx)

    lat_m, ctx_m = _split_mods(ada_mods(cvec, ada_w[0], ada_b[0]), b, d)
    lam_vecs = jnp.stack([ev_lq1[0], ev_lk1[0], ev_lq2[0], ev_lk2[0]])
    x, ctx = _even_layer(x, ctx, lat_m, ctx_m, norm1_g[0], norm2_g[0], ev_w_in[0], ev_conv_w[0], ev_conv_b[0],
                         ev_ln_g[0], ev_ln_b[0], lam_vecs, ev_subln_g[0], ev_w_out[0], mlp_w1[0], mlp_w2[0],
                         0.8 - 0.6 * math.exp(-0.3 * 0))

    lat_m, ctx_m = _split_mods(ada_mods(cvec, ada_w[1], ada_b[1]), b, d)
    return _odd_layer_last(x, ctx, lat_m, ctx_m, norm1_g[1], norm2_g[1], od_w_in[0], od_hy_short_w[0],
                           od_hy_short_b[0], od_hy_w1[0], od_hy_b1[0], od_hy_freq1[0], od_hy_w2[0], od_hy_b2[0],
                           od_hy_freq2[0], od_hy_w3[0], od_hy_skip[0], od_dn_conv_w[0], od_dn_alog_f[0],
                           od_dn_alog_b[0], od_dn_dtb_f[0], od_dn_dtb_b[0], od_dn_norm_g[0], od_w_out[0],
                           mlp_w1[1], mlp_w2[1], final_g)
```

```python
import functools
import math

import jax
import jax.numpy as jnp
import numpy as np
from jax import lax
from jax.experimental import pallas as pl
from jax.experimental.pallas import tpu as pltpu

F32 = jnp.float32
BF16 = jnp.bfloat16

EPS = 1e-6
N_MOD = 6
GRID_W = 64
ROPE_BASE = 10000.0
CONV_W = 31
DIFF_HEADS = 4
DIFF_DQK = 64
DIFF_DV = 128
HY_ORDER = 2
HY_EMB = 33
HY_FFN = 64
HY_MAX_DECAY = math.log(1e-2) / 0.3
HY_MIN_DECAY = math.log(1e-2) / 1.5
DN_HEADS = 4
DN_DK = 128
DN_CONV_W = 5
DN_CHUNK = 64

LANE = 128
SUBLANE = 8
VMEM_LIMIT = 56 * 1024 * 1024


def _cparams(*sem):
    return pltpu.CompilerParams(dimension_semantics=sem, vmem_limit_bytes=VMEM_LIMIT)


def _tile(n, pref, mult=SUBLANE):
    if n <= pref:
        return n
    t = (pref // mult) * mult
    while t > mult and n % t:
        t -= mult
    assert n % t == 0, (n, pref, mult)
    return t


def _const_spec(shape):
    nd = len(shape)
    return pl.BlockSpec(shape, lambda *_: (0,) * nd, pipeline_mode=pl.Buffered(1))


def _silu(x):
    return x * jax.nn.sigmoid(x)


def _norm_mod(x, g, sh, sc):
    y = x * lax.rsqrt(jnp.mean(x * x, axis=-1, keepdims=True) + EPS)
    return (y * g) * (1.0 + sc) + sh


def _mods_kernel(c_ref, w_ref, b_ref, o_ref):
    cond = _silu(c_ref[...])
    o_ref[...] = jnp.dot(cond, w_ref[...], preferred_element_type=F32,
                         precision=lax.Precision.HIGHEST) + b_ref[...]


def ada_mods(cvec, w, b):
    r, d = cvec.shape
    n = w.shape[1]
    tn = _tile(n, 512, LANE)
    return pl.pallas_call(
        _mods_kernel,
        grid=(n // tn,),
        in_specs=[pl.BlockSpec((r, d), lambda j: (0, 0)),
                  pl.BlockSpec((d, tn), lambda j: (0, j)),
                  pl.BlockSpec((1, tn), lambda j: (0, j))],
        out_specs=pl.BlockSpec((r, tn), lambda j: (0, j)),
        out_shape=jax.ShapeDtypeStruct((r, n), F32),
        compiler_params=_cparams("arbitrary"),
        name="ada_mods",
    )(cvec, w, b.reshape(1, n))


def _swap16(y):
    n = y.shape[-1]
    lane = lax.broadcasted_iota(jnp.int32, y.shape, y.ndim - 1)
    fwd = pltpu.roll(y, n - 16, y.ndim - 1)
    bwd = pltpu.roll(y, 16, y.ndim - 1)
    return jnp.where((lane % 32) < 16, fwd, bwd)


def _proj_kernel(x_ref, g_ref, sh_ref, sc_ref, w_ref, *rest, splits, rope_cols, tc):
    n_out = len(splits)
    if rope_cols:
        cq_ref, sq_ref, ck_ref, sk_ref = rest[:4]
        rest = rest[4:]
    o_refs = rest[:n_out]
    h = _norm_mod(x_ref[0], g_ref[...], sh_ref[0], sc_ref[0]).astype(BF16)
    for o_ref, (s0, s1) in zip(o_refs, splits):
        for c0 in range(s0, s1, tc):
            y = jnp.dot(h, w_ref[:, c0:c0 + tc], preferred_element_type=F32)
            if rope_cols and rope_cols[0] <= c0 < rope_cols[2]:
                is_q = c0 < rope_cols[1]
                cos = (cq_ref if is_q else ck_ref)[...]
                sin = (sq_ref if is_q else sk_ref)[...]
                reps = tc // LANE
                cos = jnp.concatenate([cos] * reps, axis=1)
                sin = jnp.concatenate([sin] * reps, axis=1)
                y = y * cos + _swap16(y) * sin
            o_ref[0, :, c0 - s0:c0 - s0 + tc] = y.astype(o_ref.dtype)


def norm_mod_proj(x, g, sh, sc, w, splits, dtypes, rope=None, rope_cols=None):
    b, l, d = x.shape
    n = w.shape[1]
    tm = _tile(l, 512)
    tc = 512 if all(s % 512 == 0 for sp in splits for s in sp) else LANE
    assert all(s % tc == 0 for sp in splits for s in sp)
    if rope_cols:
        assert all(c % tc == 0 for c in rope_cols)
    row = lambda bi, i: (bi, i, 0)
    in_specs = [pl.BlockSpec((1, tm, d), row),
                _const_spec((1, d)),
                pl.BlockSpec((1, 1, d), lambda bi, i: (bi, 0, 0)),
                pl.BlockSpec((1, 1, d), lambda bi, i: (bi, 0, 0)),
                _const_spec((d, n))]
    args = [x, g.reshape(1, d), sh, sc, w]
    if rope_cols:
        in_specs += [pl.BlockSpec((tm, LANE), lambda bi, i: (i, 0))] * 4
        args += list(rope)
    return pl.pallas_call(
        functools.partial(_proj_kernel, splits=tuple(splits), rope_cols=rope_cols, tc=tc),
        grid=(b, l // tm),
        in_specs=in_specs,
        out_specs=[pl.BlockSpec((1, tm, s1 - s0), row) for s0, s1 in splits],
        out_shape=[jax.ShapeDtypeStruct((b, l, s1 - s0), dt) for (s0, s1), dt in zip(splits, dtypes)],
        compiler_params=_cparams("parallel", "parallel"),
        name="norm_mod_proj",
    )(*args)


CONV_HALO = 16


def _conformer_kernel(ac_ref, gc_ref, ap_ref, gp_ref, an_ref, gn_ref, w_ref, b_ref, lg_ref, lb_ref, o_ref,
                      u_ref, y_ref, *, tl, ch):
    i = pl.program_id(1)
    last = pl.num_programs(1) - 1
    left = (CONV_W - 1) // 2
    glu = lambda a, g: a * jax.nn.sigmoid(g)
    u_ref[CONV_HALO:CONV_HALO + tl, :] = glu(ac_ref[0], gc_ref[0])
    prev = glu(ap_ref[0], gp_ref[0])
    nxt = glu(an_ref[0], gn_ref[0])
    u_ref[0:CONV_HALO, :] = jnp.where(i > 0, prev, 0.0)
    u_ref[CONV_HALO + tl:CONV_HALO + tl + CONV_HALO, :] = jnp.where(i < last, nxt, 0.0)
    rows = min(tl, 128)
    for r0 in range(0, tl, rows):
        for c0 in range(0, ch, LANE):
            acc = jnp.zeros((rows, LANE), F32) + b_ref[:, c0:c0 + LANE]
            for k in range(CONV_W):
                s = CONV_HALO - left + k + r0
                acc = acc + w_ref[k:k + 1, c0:c0 + LANE] * u_ref[s:s + rows, c0:c0 + LANE]
            y_ref[r0:r0 + rows, c0:c0 + LANE] = acc
    y = y_ref[...]
    mu = jnp.mean(y, axis=-1, keepdims=True)
    yc = y - mu
    var = jnp.mean(yc * yc, axis=-1, keepdims=True)
    z = yc * lax.rsqrt(var + EPS) * lg_ref[...] + lb_ref[...]
    o_ref[0] = _silu(z).astype(o_ref.dtype)


def conformer_conv(p, conv_w, conv_b, ln_g, ln_b):
    b, l, _ = p.shape
    ch = conv_w.shape[1]
    tl = _tile(l, 256, CONV_HALO)
    nh = tl // CONV_HALO
    nblk = l // CONV_HALO
    cur = lambda col: pl.BlockSpec((1, tl, ch), lambda bi, i: (bi, i, col))
    prv = lambda col: pl.BlockSpec((1, CONV_HALO, ch), lambda bi, i: (bi, jnp.maximum(i * nh - 1, 0), col))
    nxt = lambda col: pl.BlockSpec((1, CONV_HALO, ch), lambda bi, i: (bi, jnp.minimum((i + 1) * nh, nblk - 1), col))
    return pl.pallas_call(
        functools.partial(_conformer_kernel, tl=tl, ch=ch),
        grid=(b, l // tl),
        in_specs=[cur(0), cur(1), prv(0), prv(1), nxt(0), nxt(1),
                  _const_spec((CONV_W, ch)), _const_spec((1, ch)), _const_spec((1, ch)), _const_spec((1, ch))],
        out_specs=pl.BlockSpec((1, tl, ch), lambda bi, i: (bi, i, 0)),
        out_shape=jax.ShapeDtypeStruct((b, l, ch), BF16),
        scratch_shapes=[pltpu.VMEM((tl + 2 * CONV_HALO, ch), F32), pltpu.VMEM((tl, ch), F32)],
        compiler_params=_cparams("parallel", "parallel"),
        name="conformer_conv",
    )(p, p, p, p, p, p, conv_w, conv_b.reshape(1, ch), ln_g.reshape(1, ch), ln_b.reshape(1, ch))


ATTN_QBLK = 256


def _diff_attn_kernel(qt_ref, k_ref, vt_ref, lam_ref, g_ref, o_ref, qm_ref, m_ref, l_ref, acc_ref,
                      *, lambda_init):
    kv = pl.program_id(3)

    @pl.when(kv == 0)
    def _():
        qt = qt_ref[0]
        row = lax.broadcasted_iota(jnp.int32, qt.shape, 0)
        zero = jnp.zeros_like(qt)
        qm_ref[0] = jnp.where(row < DIFF_DQK, qt, zero)
        qm_ref[1] = jnp.where(row < DIFF_DQK, zero, qt)
        m_ref[...] = jnp.full(m_ref.shape, -jnp.inf, F32)
        l_ref[...] = jnp.zeros(l_ref.shape, F32)
        acc_ref[...] = jnp.zeros(acc_ref.shape, F32)

    k = k_ref[0]
    vt = vt_ref[0]
    qblk = min(ATTN_QBLK, qm_ref.shape[2])
    chains = [(mi, slice(q0, q0 + qblk)) for q0 in range(0, qm_ref.shape[2], qblk) for mi in range(2)]
    scores = [jnp.dot(k, qm_ref[mi, :, qs], preferred_element_type=F32) for mi, qs in chains]
    for (mi, qs), st in zip(chains, scores):
        m_old = m_ref[mi, :, qs]
        m_new = jnp.maximum(m_old, jnp.max(st, axis=0, keepdims=True))
        alpha = jnp.exp2(m_old - m_new)
        e = jnp.exp2(st - m_new)
        l_ref[mi, :, qs] = alpha * l_ref[mi, :, qs] + jnp.sum(e, axis=0, keepdims=True)
        acc_ref[mi, :, qs] = alpha * acc_ref[mi, :, qs] + jnp.dot(vt, e.astype(BF16), preferred_element_type=F32)
        m_ref[mi, :, qs] = m_new

    @pl.when(kv == pl.num_programs(3) - 1)
    def _():
        lam = (jnp.exp(jnp.sum(lam_ref[0:1, :] * lam_ref[1:2, :], axis=-1, keepdims=True))
               - jnp.exp(jnp.sum(lam_ref[2:3, :] * lam_ref[3:4, :], axis=-1, keepdims=True)) + lambda_init)
        ot = acc_ref[0] / l_ref[0] - lam * (acc_ref[1] / l_ref[1])
        ot = ot * lax.rsqrt(jnp.mean(ot * ot, axis=0, keepdims=True) + EPS)
        o = ot.T * g_ref[...]
        o_ref[0] = (o * (1.0 - lambda_init)).astype(o_ref.dtype)


def diff_attention(qt, k, vt, lam_vecs, subln_g, lambda_init, k_blk):
    b, _, lq = qt.shape
    lk = k.shape[1]
    tq = _tile(lq, 1024, LANE)
    tk = _tile(lk, 1024, LANE)
    return pl.pallas_call(
        functools.partial(_diff_attn_kernel, lambda_init=lambda_init),
        grid=(b, DIFF_HEADS, lq // tq, lk // tk),
        in_specs=[pl.BlockSpec((1, LANE, tq), lambda bi, h, i, j: (bi, h, i)),
                  pl.BlockSpec((1, tk, LANE), lambda bi, h, i, j: (bi, j, k_blk + h)),
                  pl.BlockSpec((1, LANE, tk), lambda bi, h, i, j: (bi, h, j)),
                  _const_spec((4, DIFF_DQK)), _const_spec((1, DIFF_DV))],
        out_specs=pl.BlockSpec((1, tq, LANE), lambda bi, h, i, j: (bi, i, h)),
        out_shape=jax.ShapeDtypeStruct((b, lq, DIFF_HEADS * DIFF_DV), BF16),
        scratch_shapes=[pltpu.VMEM((2, LANE, tq), BF16),
                        pltpu.VMEM((2, 1, tq), F32), pltpu.VMEM((2, 1, tq), F32),
                        pltpu.VMEM((2, DIFF_DV, tq), F32)],
        compiler_params=_cparams("parallel", "parallel", "parallel", "arbitrary"),
        name="diff_attention",
    )(qt, k, vt, lam_vecs, subln_g.reshape(1, DIFF_DV))


def _out_mlp_kernel(x_ref, a_ref, b_ref, wo_ref, gt1_ref, g2_ref, sh2_ref, sc2_ref, gt2_ref, w1_ref, w2_ref,
                    fg_ref, o_ref, *, a_transposed, hidden_chunk, final_norm):
    half = wo_ref.shape[0] // 2
    if a_transposed:
        ya = lax.dot_general(a_ref[0], wo_ref[0:half, :], (((0,), (0,)), ((), ())), preferred_element_type=F32)
    else:
        ya = jnp.dot(a_ref[0], wo_ref[0:half, :], preferred_element_type=F32)
    y = ya + jnp.dot(b_ref[0], wo_ref[half:, :], preferred_element_type=F32)
    x1 = x_ref[0] + gt1_ref[0] * y
    h = _norm_mod(x1, g2_ref[...], sh2_ref[0], sc2_ref[0]).astype(BF16)
    acc = jnp.zeros_like(x1)
    for c0 in range(0, w1_ref.shape[1], hidden_chunk):
        t = jnp.maximum(jnp.dot(h, w1_ref[:, c0:c0 + hidden_chunk], preferred_element_type=F32), 0.0)
        acc = acc + jnp.dot((t * t).astype(BF16), w2_ref[c0:c0 + hidden_chunk, :], preferred_element_type=F32)
    x2 = x1 + gt2_ref[0] * acc
    if final_norm:
        x2 = x2 * lax.rsqrt(jnp.mean(x2 * x2, axis=-1, keepdims=True) + EPS) * fg_ref[...]
    o_ref[0] = x2


def out_proj_mlp(x, mix_a, mix_b, w_out, gt1, g2, sh2, sc2, gt2, w1, w2, final_g=None, a_transposed=False):
    b, l, d = x.shape
    c = mix_b.shape[2]
    hid = w1.shape[1]
    tm = _tile(l, 512, LANE)
    row = lambda bi, i: (bi, i, 0)
    vec = lambda bi, i: (bi, 0, 0)
    a_spec = (pl.BlockSpec((1, c, tm), lambda bi, i: (bi, 0, i)) if a_transposed
              else pl.BlockSpec((1, tm, c), row))
    fg = jnp.ones((1, d), F32) if final_g is None else final_g.reshape(1, d)
    return pl.pallas_call(
        functools.partial(_out_mlp_kernel, a_transposed=a_transposed, hidden_chunk=min(hid, 1024),
                          final_norm=final_g is not None),
        grid=(b, l // tm),
        in_specs=[pl.BlockSpec((1, tm, d), row), a_spec, pl.BlockSpec((1, tm, c), row),
                  _const_spec((2 * c, d)),
                  pl.BlockSpec((1, 1, d), vec), _const_spec((1, d)),
                  pl.BlockSpec((1, 1, d), vec), pl.BlockSpec((1, 1, d), vec), pl.BlockSpec((1, 1, d), vec),
                  _const_spec((d, hid)), _const_spec((hid, d)), _const_spec((1, d))],
        out_specs=pl.BlockSpec((1, tm, d), row),
        out_shape=jax.ShapeDtypeStruct((b, l, d), F32),
        compiler_params=_cparams("parallel", "parallel"),
        name="out_proj_mlp",
    )(x, mix_a, mix_b, w_out, gt1, g2.reshape(1, d), sh2, sc2, gt2, w1, w2, fg)


def _rope_tables(length, scale):
    rows = length // GRID_W
    row = jnp.repeat(jnp.arange(rows, dtype=F32), GRID_W)
    col = jnp.tile(jnp.arange(GRID_W, dtype=F32), rows)
    n_freq = DIFF_DQK // 4
    inv = ROPE_BASE ** (-jnp.arange(n_freq, dtype=F32) / n_freq)
    ang_r = row[:, None] * inv
    ang_c = col[:, None] * inv
    cos = jnp.concatenate([jnp.cos(ang_r)] * 2 + [jnp.cos(ang_c)] * 2, axis=-1)
    sin = jnp.concatenate([-jnp.sin(ang_r), jnp.sin(ang_r), -jnp.sin(ang_c), jnp.sin(ang_c)], axis=-1)
    cos = jnp.concatenate([cos, cos], axis=-1) * scale
    sin = jnp.concatenate([sin, sin], axis=-1) * scale
    return cos, sin


def _flat_tables(length, scale):
    return jnp.full((length, LANE), scale, F32), jnp.zeros((length, LANE), F32)


def _proj_t_kernel(x_ref, g_ref, sh_ref, sc_ref, wt_ref, o_ref, *, rc):
    h = _norm_mod(x_ref[0], g_ref[...], sh_ref[0], sc_ref[0]).astype(BF16)
    for r0 in range(0, wt_ref.shape[0], rc):
        o_ref[0, r0:r0 + rc, :] = lax.dot_general(wt_ref[r0:r0 + rc, :], h, (((1,), (1,)), ((), ())),
                                                  preferred_element_type=F32)


def norm_mod_proj_t(x, g, sh, sc, wt):
    b, l, d = x.shape
    c = wt.shape[0]
    tl = _tile(l, 512, LANE)
    return pl.pallas_call(
        functools.partial(_proj_t_kernel, rc=_tile(c, 512)),
        grid=(b, l // tl),
        in_specs=[pl.BlockSpec((1, tl, d), lambda bi, i: (bi, i, 0)),
                  _const_spec((1, d)),
                  pl.BlockSpec((1, 1, d), lambda bi, i: (bi, 0, 0)),
                  pl.BlockSpec((1, 1, d), lambda bi, i: (bi, 0, 0)),
                  _const_spec((c, d))],
        out_specs=pl.BlockSpec((1, c, tl), lambda bi, i: (bi, 0, i)),
        out_shape=jax.ShapeDtypeStruct((b, c, l), F32),
        compiler_params=_cparams("parallel", "parallel"),
        name="norm_mod_proj_t",
    )(x, g.reshape(1, d), sh, sc, wt)


def _dot_hi(a, b):
    return jnp.dot(a, b, preferred_element_type=F32, precision=lax.Precision.HIGHEST)


def _hy_hidden_kernel(z_ref, w1_ref, b1_ref, f1_ref, w2_ref, b2_ref, f2_ref, o_ref):
    hid = jnp.sin(f1_ref[...] * (_dot_hi(w1_ref[...], z_ref[...]) + b1_ref[...]))
    o_ref[...] = jnp.sin(f2_ref[...] * (_dot_hi(w2_ref[...], hid) + b2_ref[...]))


def hyena_hidden(zt, w1t, b1, f1, w2t, b2, f2):
    e, n = zt.shape
    f = w1t.shape[0]
    tn = _tile(n, 2048, LANE)
    col = lambda v: v.reshape(f, 1)
    return pl.pallas_call(
        _hy_hidden_kernel,
        grid=(n // tn,),
        in_specs=[pl.BlockSpec((e, tn), lambda j: (0, j)),
                  _const_spec((f, e)), _const_spec((f, 1)), _const_spec((f, 1)),
                  _const_spec((f, f)), _const_spec((f, 1)), _const_spec((f, 1))],
        out_specs=pl.BlockSpec((f, tn), lambda j: (0, j)),
        out_shape=jax.ShapeDtypeStruct((f, n), F32),
        compiler_params=_cparams("parallel"),
        name="hyena_hidden",
    )(zt, w1t, col(b1), col(f1), w2t, col(b2), col(f2))


def _hy_filter_kernel(hid_ref, t_ref, wf_ref, wb_ref, dl_ref, o_ref, *, length, tn):
    n2 = hid_ref.shape[1]
    delta = dl_ref[...]

    def piece(c0):
        hid = hid_ref[:, c0:c0 + tn]
        dec = jnp.exp(-t_ref[:, c0:c0 + tn] * delta)
        return _dot_hi(wf_ref[...], hid) * dec, _dot_hi(wb_ref[...], hid) * dec

    norm = jnp.zeros((wf_ref.shape[0], 1), F32)
    for c0 in range(length, n2, tn):
        ff, fb = piece(c0)
        norm = norm + jnp.sum(jnp.abs(ff) + jnp.abs(fb), axis=-1, keepdims=True)
    inv = 1.0 / norm
    for c0 in range(0, n2, tn):
        ff, fb = piece(c0)
        if c0 >= length:
            o_ref[0, :, c0:c0 + tn] = ff * inv
        else:
            col = lax.broadcasted_iota(jnp.int32, fb.shape, 1) + c0
            o_ref[0, :, c0:c0 + tn] = jnp.where(col == 0, 0.0, fb * inv)


def hyena_filters(hid, t_row, w3t, deltas, length):
    f, n2 = hid.shape
    ch = deltas.shape[0]
    cb = LANE
    nblk = ch // cb
    tn = _tile(length, 2048, LANE)
    return pl.pallas_call(
        functools.partial(_hy_filter_kernel, length=length, tn=tn),
        grid=(HY_ORDER, nblk),
        in_specs=[_const_spec((f, n2)), _const_spec((1, n2)),
                  pl.BlockSpec((cb, f), lambda o, c: (o * 2 * nblk + c, 0)),
                  pl.BlockSpec((cb, f), lambda o, c: (o * 2 * nblk + nblk + c, 0)),
                  pl.BlockSpec((cb, 1), lambda o, c: (c, 0))],
        out_specs=pl.BlockSpec((1, cb, n2), lambda o, c: (o, c, 0)),
        out_shape=jax.ShapeDtypeStruct((HY_ORDER, ch, n2), F32),
        compiler_params=_cparams("parallel", "parallel"),
        name="hyena_filters",
    )(hid, t_row, w3t, w3t, deltas.reshape(ch, 1))


HY_T = 256
HY_CB = SUBLANE


def _hyena_kernel(v_ref, x1_ref, x2_ref, g_ref, sw_ref, sb_ref, sk_ref_s, o_ref, sk_ref, u_ref, acc_ref,
                  *, length, nch):
    nb = v_ref.shape[0]
    nj = length // HY_T
    cblk = pl.program_id(0)
    lane = lax.broadcasted_iota(jnp.int32, (nb, length), 1)

    def short(ref, cc, part):
        x = ref[:, cc, :]
        ch = part * nch + cblk * HY_CB + cc
        prev = jnp.where(lane == 0, 0.0, pltpu.roll(x, 1, 1))
        nxt = jnp.where(lane == length - 1, 0.0, pltpu.roll(x, length - 1, 1))
        return sw_ref[0, ch] * prev + sw_ref[1, ch] * x + sw_ref[2, ch] * nxt + sb_ref[ch]

    def long_conv(u, order, cc):
        g = jnp.broadcast_to(g_ref[order, pl.ds(cc, 1), :], (SUBLANE, 2 * length))
        half = HY_T // 2
        for a2 in range(half // (2 * SUBLANE)):
            lo = pltpu.roll(g, 2 * a2 * SUBLANE, 1, stride=1, stride_axis=0)
            hi = pltpu.roll(g, (2 * a2 + 1) * SUBLANE, 1, stride=1, stride_axis=0)
            pair = jnp.concatenate([lo, hi], axis=0).astype(BF16)
            r0 = 2 * a2 * SUBLANE
            sk_ref[r0:r0 + 2 * SUBLANE, :] = pair
            sk_ref[half + r0:half + r0 + 2 * SUBLANE, half:] = pair[:, :2 * length - half]
        for j in range(nj):
            u_ref[j * nb:(j + 1) * nb, :] = u[:, j * HY_T:(j + 1) * HY_T]
        acc_ref[...] = jnp.zeros(acc_ref.shape, F32)
        for d in range(-(nj - 1), nj):
            m = (nj - abs(d)) * nb
            lo_in, lo_out = max(0, -d) * nb, max(0, d) * nb
            tile = sk_ref[:, length + d * HY_T:length + (d + 1) * HY_T]
            acc_ref[lo_out:lo_out + m, :] += jnp.dot(u_ref[lo_in:lo_in + m, :].astype(BF16), tile,
                                                     preferred_element_type=F32)
        return jnp.concatenate([acc_ref[i * nb:(i + 1) * nb, :] for i in range(nj)], axis=1)

    def channel(cc, carry):
        v = short(v_ref, cc, 0)
        x1 = short(x1_ref, cc, 1)
        x2 = short(x2_ref, cc, 2)
        ch = cblk * HY_CB + cc
        z = x1 * (long_conv(v, 0, cc) + sk_ref_s[0, ch] * v)
        z = x2 * (long_conv(z, 1, cc) + sk_ref_s[1, ch] * z)
        o_ref[:, cc, :] = z
        return carry

    lax.fori_loop(0, HY_CB, channel, 0)


def hyena_mix(pt, filt, short_w, short_b, skip):
    b, c3, l = pt.shape
    nch = c3 // 3
    assert l % HY_T == 0 and nch % HY_CB == 0 and b == SUBLANE
    nblk = nch // HY_CB
    smem = pl.BlockSpec(memory_space=pltpu.SMEM)
    part = lambda k: pl.BlockSpec((b, HY_CB, l), lambda c: (0, k * nblk + c, 0))
    return pl.pallas_call(
        functools.partial(_hyena_kernel, length=l, nch=nch),
        grid=(nblk,),
        in_specs=[part(0), part(1), part(2),
                  pl.BlockSpec((HY_ORDER, HY_CB, 2 * l), lambda c: (0, c, 0)),
                  smem, smem, smem],
        out_specs=pl.BlockSpec((b, HY_CB, l), lambda c: (0, c, 0)),
        out_shape=jax.ShapeDtypeStruct((b, nch, l), F32),
        scratch_shapes=[pltpu.VMEM((HY_T, 2 * l), BF16), pltpu.VMEM((l // HY_T * b, HY_T), F32),
                        pltpu.VMEM((l // HY_T * b, HY_T), F32)],
        compiler_params=_cparams("parallel"),
        name="hyena_mix",
    )(pt, pt, pt, filt, short_w, short_b, skip)


DN_HALO = SUBLANE
DN_BETA_F, DN_BETA_B, DN_G_F, DN_G_B = 0, DN_HEADS, 2 * DN_HEADS, 3 * DN_HEADS


def _dn_feat_kernel(c_ref, p_ref, n_ref, s_ref, w_ref, al_ref, dt_ref, q_ref, k_ref, v_ref, f_ref, u_ref, *, tl):
    i = pl.program_id(1)
    last = pl.num_programs(1) - 1
    left = (DN_CONV_W - 1) // 2
    nc = c_ref.shape[2]
    u_ref[DN_HALO:DN_HALO + tl, :] = c_ref[0]
    u_ref[0:DN_HALO, :] = jnp.where(i > 0, p_ref[0], 0.0)
    u_ref[DN_HALO + tl:DN_HALO + tl + DN_HALO, :] = jnp.where(i < last, n_ref[0], 0.0)
    rows = min(tl, 128)
    outs = (q_ref, k_ref, v_ref)
    per = nc // len(outs)
    for r0 in range(0, tl, rows):
        for c0 in range(0, nc, LANE):
            acc = jnp.zeros((rows, LANE), F32)
            for k in range(DN_CONV_W):
                s = DN_HALO - left + k + r0
                acc = acc + w_ref[k:k + 1, c0:c0 + LANE] * u_ref[s:s + rows, c0:c0 + LANE]
            y = _silu(acc)
            which, off = divmod(c0, per)
            if which < 2:
                y = y * lax.rsqrt(jnp.sum(y * y, axis=-1, keepdims=True) + 1e-6)
            outs[which][0, r0:r0 + rows, off:off + LANE] = y
    s = s_ref[0]
    lane = lax.broadcasted_iota(jnp.int32, s.shape, 1)
    gate = -jnp.exp(al_ref[...]) * jax.nn.softplus(s + dt_ref[...])
    f_ref[0] = jnp.where(lane < DN_G_F, jax.nn.sigmoid(s), gate)


def dn_features(qkv, small, conv_w, alog_row, dtb_row):
    b, l, nc = qkv.shape
    tl = _tile(l, 256, DN_HALO)
    nh = tl // DN_HALO
    nblk = l // DN_HALO
    w = nc // 3
    row = lambda bi, i: (bi, i, 0)
    return pl.pallas_call(
        functools.partial(_dn_feat_kernel, tl=tl),
        grid=(b, l // tl),
        in_specs=[pl.BlockSpec((1, tl, nc), row),
                  pl.BlockSpec((1, DN_HALO, nc), lambda bi, i: (bi, jnp.maximum(i * nh - 1, 0), 0)),
                  pl.BlockSpec((1, DN_HALO, nc), lambda bi, i: (bi, jnp.minimum((i + 1) * nh, nblk - 1), 0)),
                  pl.BlockSpec((1, tl, LANE), row),
                  _const_spec((DN_CONV_W, nc)), _const_spec((1, LANE)), _const_spec((1, LANE))],
        out_specs=[pl.BlockSpec((1, tl, w), row)] * 3 + [pl.BlockSpec((1, tl, LANE), row)],
        out_shape=[jax.ShapeDtypeStruct((b, l, w), F32)] * 3 + [jax.ShapeDtypeStruct((b, l, LANE), F32)],
        scratch_shapes=[pltpu.VMEM((tl + 2 * DN_HALO, nc), F32)],
        compiler_params=_cparams("parallel", "parallel"),
        name="dn_features",
    )(qkv, qkv, qkv, small, conv_w, alog_row, dtb_row)


def _bdot(a, b):
    return jnp.dot(a.astype(BF16), b.astype(BF16), preferred_element_type=F32)


def _bdot_nt(a, b):
    return lax.dot_general(a.astype(BF16), b.astype(BF16), (((1,), (1,)), ((), ())), preferred_element_type=F32)


def _bdot_tn(a, b):
    return lax.dot_general(a.astype(BF16), b.astype(BF16), (((0,), (0,)), ((), ())), preferred_element_type=F32)


def _dot3(a, b):
    a_hi, b_hi = a.astype(BF16), b.astype(BF16)
    a_lo, b_lo = (a - a_hi.astype(F32)).astype(BF16), (b - b_hi.astype(F32)).astype(BF16)
    dot = lambda p, q: jnp.dot(p, q, preferred_element_type=F32)
    return dot(a_hi, b_hi) + (dot(a_hi, b_lo) + dot(a_lo, b_hi))


def _cumsum_rows(x, reverse):
    n = x.shape[0]
    row = lax.broadcasted_iota(jnp.int32, x.shape, 0)
    s = 1
    while s < n:
        if reverse:
            x = x + jnp.where(row < n - s, pltpu.roll(x, n - s, 0), 0.0)
        else:
            x = x + jnp.where(row >= s, pltpu.roll(x, s, 0), 0.0)
        s *= 2
    return x


def _dn_chunks(chains):
    c = chains[0]["k"].shape[0]
    ri = lax.broadcasted_iota(jnp.int32, (c, c), 0)
    ci = lax.broadcasted_iota(jnp.int32, (c, c), 1)
    work = []
    for ch in chains:
        rev = ch["reverse"]
        incl = (ri <= ci) if rev else (ri >= ci)
        strict = (ri < ci) if rev else (ri > ci)
        decay = jnp.where(incl, jnp.exp(jnp.where(incl, ch["gcol"] - ch["grow"], 0.0)), 0.0)
        kb = ch["k"] * ch["beta"]
        qs = ch["q"] * (ch["k"].shape[1] ** -0.5)
        work.append(dict(ch, strict=strict, decay=decay, kb=kb, qs=qs))
    grams = [_bdot_nt(jnp.concatenate([w["kb"], w["qs"]], axis=0), w["k"]) for w in work]
    pws = [-jnp.where(w["strict"], g[:c] * w["decay"], 0.0) for w, g in zip(work, grams)]
    ymats = list(pws)
    for _ in range(int(math.log2(c)) - 1):
        pws = [_dot3(pw, pw) for pw in pws]
        ymats = [ym + pw + _dot3(ym, pw) for ym, pw in zip(ymats, pws)]
    outs = []
    uws, e_gs = [], []
    for w, ym in zip(work, ymats):
        e_g = jnp.exp(w["gcol"])
        rhs = jnp.concatenate([w["v"] * w["beta"], w["kb"] * e_g], axis=1)
        uws.append(rhs + _bdot(ym, rhs))
        e_gs.append(e_g)
    dv = work[0]["v"].shape[1]
    wss = [_bdot(jnp.concatenate([uw[:, dv:], w["qs"] * e_g], axis=0), w["state"])
           for w, uw, e_g in zip(work, uws, e_gs)]
    v_news = [uw[:, :dv] - ws[:c] for uw, ws in zip(uws, wss)]
    for w, g, ws, v_new in zip(work, grams, wss, v_news):
        gcol = w["gcol"]
        g_last = gcol[0:1, :] if w["reverse"] else gcol[c - 1:c, :]
        k_dec = w["k"] * jnp.exp(g_last - gcol)
        o = ws[c:] + _bdot(g[c:] * w["decay"], v_new)
        new_state = w["state"] * jnp.exp(g_last) + _bdot_tn(k_dec, v_new)
        outs.append((o, new_state))
    return outs


def _dn_scan_kernel(qf_ref, kf_ref, vf_ref, ff_ref, qb_ref, kb_ref, vb_ref, fb_ref, s0f_ref, s0b_ref,
                    of_ref, ob_ref, sf_ref, sb_ref, st_ref):
    n = pl.program_id(1)

    @pl.when(n == 0)
    def _():
        st_ref[0] = s0f_ref[0]
        st_ref[1] = s0b_ref[0]

    dirs = ((qf_ref, kf_ref, vf_ref, ff_ref, of_ref, False), (qb_ref, kb_ref, vb_ref, fb_ref, ob_ref, True))
    chains, dests = [], []
    for di, (q_ref, k_ref, v_ref, f_ref, o_ref, reverse) in enumerate(dirs):
        feats = f_ref[0]
        gc = _cumsum_rows(feats, reverse)
        gct = gc.T
        for h in range(DN_HEADS):
            cols = slice(h * DN_DK, (h + 1) * DN_DK)
            bl = (DN_BETA_B if reverse else DN_BETA_F) + h
            gl = (DN_G_B if reverse else DN_G_F) + h
            chains.append(dict(q=q_ref[0, :, cols], k=k_ref[0, :, cols], v=v_ref[0, :, cols],
                               beta=feats[:, bl:bl + 1], gcol=gc[:, gl:gl + 1], grow=gct[gl:gl + 1, :],
                               state=st_ref[di, h], reverse=reverse))
            dests.append((o_ref, cols, di, h))
    for (o_ref, cols, di, h), (o, s_new) in zip(dests, _dn_chunks(chains)):
        o_ref[0, :, cols] = o
        st_ref[di, h] = s_new

    @pl.when(n == pl.num_programs(1) - 1)
    def _():
        sf_ref[0] = st_ref[0]
        sb_ref[0] = st_ref[1]


def dn_scan(q, k, v, feats, s0f, s0b):
    b, l, w = q.shape
    nchunk = l // DN_CHUNK
    fwd = lambda bi, n: (bi, n, 0)
    bwd = lambda bi, n: (bi, nchunk - 1 - n, 0)
    st = lambda bi, n: (bi, 0, 0, 0)
    blk = lambda im, width: pl.BlockSpec((1, DN_CHUNK, width), im)
    st_spec = pl.BlockSpec((1, DN_HEADS, DN_DK, DN_DK), st)
    st_shape = jax.ShapeDtypeStruct((b, DN_HEADS, DN_DK, DN_DK), F32)
    return pl.pallas_call(
        _dn_scan_kernel,
        grid=(b, nchunk),
        in_specs=[blk(fwd, w), blk(fwd, w), blk(fwd, w), blk(fwd, LANE),
                  blk(bwd, w), blk(bwd, w), blk(bwd, w), blk(bwd, LANE), st_spec, st_spec],
        out_specs=[blk(fwd, w), blk(bwd, w), st_spec, st_spec],
        out_shape=[jax.ShapeDtypeStruct((b, l, w), F32)] * 2 + [st_shape, st_shape],
        scratch_shapes=[pltpu.VMEM((2, DN_HEADS, DN_DK, DN_DK), F32)],
        compiler_params=_cparams("parallel", "arbitrary"),
        name="dn_scan",
    )(q, k, v, feats, q, k, v, feats, s0f, s0b)


def _dn_out_kernel(of_ref, ob_ref, gate_ref, g_ref, o_ref):
    o = of_ref[0] + ob_ref[0]
    for c0 in range(0, o.shape[1], DN_DK):
        t = o[:, c0:c0 + DN_DK]
        t = t * lax.rsqrt(jnp.mean(t * t, axis=-1, keepdims=True) + EPS) * g_ref[...]
        o_ref[0, :, c0:c0 + DN_DK] = (t * _silu(gate_ref[0, :, c0:c0 + DN_DK])).astype(o_ref.dtype)


def dn_output(o_f, o_b, gate, norm_g):
    b, l, w = o_f.shape
    tl = _tile(l, 512)
    row = lambda bi, i: (bi, i, 0)
    return pl.pallas_call(
        _dn_out_kernel,
        grid=(b, l // tl),
        in_specs=[pl.BlockSpec((1, tl, w), row)] * 3 + [_const_spec((1, DN_DK))],
        out_specs=pl.BlockSpec((1, tl, w), row),
        out_shape=jax.ShapeDtypeStruct((b, l, w), BF16),
        compiler_params=_cparams("parallel", "parallel"),
        name="dn_output",
    )(o_f, o_b, gate, norm_g.reshape(1, DN_DK))


def _split_mods(mods, b, d):
    lat = [mods[:b, k * d:(k + 1) * d].reshape(b, 1, d) for k in range(N_MOD)]
    ctx = [jnp.broadcast_to(mods[b, k * d:(k + 1) * d].reshape(1, 1, d), (b, 1, d)) for k in range(N_MOD)]
    return lat, ctx


def _even_layer(x, ctx, lat_m, ctx_m, n1g, n2g, w_in, conv_w, conv_b, ln_g, ln_b, lam_vecs, subln_g, w_out,
                w1, w2, lambda_init):
    b, l, d = x.shape
    lc = ctx.shape[1]
    ch = conv_w.shape[1]
    qk = DIFF_HEADS * 2 * DIFF_DQK
    q0, k0, v0, n_in = 2 * ch, 2 * ch + qk, 2 * ch + 2 * qk, 2 * ch + 2 * qk + DIFF_HEADS * DIFF_DV
    scale = DIFF_DQK ** -0.5 * math.log2(math.e)
    splits = [(0, q0), (q0, n_in)]
    w_in = w_in.astype(BF16)
    rope_lat = _rope_tables(l, scale) + _rope_tables(l, 1.0)
    rope_ctx = _flat_tables(lc, scale) + _flat_tables(lc, 1.0)
    pc_lat, qkv_lat = norm_mod_proj(x, n1g, lat_m[0], lat_m[1], w_in, splits, (F32, BF16), rope_lat, (q0, k0, v0))
    pc_ctx, qkv_ctx = norm_mod_proj(ctx, n1g, ctx_m[0], ctx_m[1], w_in, splits, (F32, BF16), rope_ctx, (q0, k0, v0))
    conv_lat = conformer_conv(pc_lat, conv_w, conv_b, ln_g, ln_b)
    conv_ctx = conformer_conv(pc_ctx, conv_w, conv_b, ln_g, ln_b)
    k_all = jnp.concatenate([qkv_lat[:, :, qk:2 * qk], qkv_ctx[:, :, qk:2 * qk]], axis=1)
    vt_ctx = jnp.swapaxes(qkv_ctx[:, :, 2 * qk:], 1, 2)
    vt_all = jnp.concatenate([jnp.swapaxes(qkv_lat[:, :, 2 * qk:], 1, 2), vt_ctx], axis=2)
    qt_lat = jnp.swapaxes(qkv_lat[:, :, :qk], 1, 2)
    qt_ctx = jnp.swapaxes(qkv_ctx[:, :, :qk], 1, 2)
    o_lat = diff_attention(qt_lat, k_all, vt_all, lam_vecs, subln_g, lambda_init, 0)
    o_ctx = diff_attention(qt_ctx, qkv_ctx, vt_ctx, lam_vecs, subln_g, lambda_init, qk // LANE)
    w_out, w1, w2 = w_out.astype(BF16), w1.astype(BF16), w2.astype(BF16)
    x = out_proj_mlp(x, conv_lat, o_lat, w_out, lat_m[2], n2g, lat_m[3], lat_m[4], lat_m[5], w1, w2)
    ctx = out_proj_mlp(ctx, conv_ctx, o_ctx, w_out, ctx_m[2], n2g, ctx_m[3], ctx_m[4], ctx_m[5], w1, w2)
    return x, ctx


def _hyena_position_tables(length):
    t = jnp.linspace(0.0, 1.0, length, dtype=F32)[:, None]
    bands = (HY_EMB - 1) // 2
    omega = 2.0 * math.pi * jnp.arange(length, dtype=F32)[:, None] / length
    ang = omega * jnp.linspace(1e-4, bands - 1, bands, dtype=F32)
    z = jnp.concatenate([t, jnp.cos(ang), -jnp.sin(ang)], axis=-1)
    pos = jnp.minimum(jnp.abs(jnp.arange(2 * length) - length), length - 1)
    pad = -HY_EMB % SUBLANE
    zt = jnp.pad(z[pos].T, ((0, pad), (0, 0)))
    return zt, t[pos].T


def _odd_layer_last(x, ctx, lat_m, ctx_m, n1g, n2g, w_in, hy_short_w, hy_short_b, hy_w1, hy_b1, hy_f1, hy_w2, hy_b2,
                    hy_f2, hy_w3, hy_skip, dn_conv_w, alog_f, alog_b, dtb_f, dtb_b, dn_norm_g, w_out, w1, w2,
                    final_g):
    b, l, d = x.shape
    hy_in = hy_short_w.shape[1]
    hy_ch = hy_in // (HY_ORDER + 1)
    dn_w = DN_HEADS * DN_DK
    n_rest = w_in.shape[1] - hy_in
    n_pad = -n_rest % LANE
    w_rest = jnp.pad(w_in[:, hy_in:], ((0, 0), (0, n_pad))).astype(BF16)
    splits = [(0, dn_w), (dn_w, 4 * dn_w), (4 * dn_w, n_rest + n_pad)]
    f32x3 = (F32, F32, F32)
    gate_lat, qkv_lat, small_lat = norm_mod_proj(x, n1g, lat_m[0], lat_m[1], w_rest, splits, f32x3)
    _, qkv_ctx, small_ctx = norm_mod_proj(ctx, n1g, ctx_m[0], ctx_m[1], w_rest, splits, f32x3)

    pt = norm_mod_proj_t(x, n1g, lat_m[0], lat_m[1], w_in[:, :hy_in].T.astype(BF16))
    zt, t_row = _hyena_position_tables(l)
    w1t = jnp.pad(hy_w1.T, ((0, 0), (0, zt.shape[0] - hy_w1.shape[0])))
    hid = hyena_hidden(zt, w1t, hy_b1, hy_f1, hy_w2.T, hy_b2, hy_f2)
    deltas = jnp.abs(jnp.linspace(HY_MIN_DECAY, HY_MAX_DECAY, hy_ch, dtype=F32))
    filt = hyena_filters(hid, t_row, hy_w3.T, deltas, l)
    hy = hyena_mix(pt, filt, hy_short_w, hy_short_b, hy_skip)

    lanes = jnp.zeros((LANE,), F32)
    alog_row = lanes.at[DN_G_F:DN_G_F + DN_HEADS].set(alog_f).at[DN_G_B:DN_G_B + DN_HEADS].set(alog_b).reshape(1, LANE)
    dtb_row = lanes.at[DN_G_F:DN_G_F + DN_HEADS].set(dtb_f).at[DN_G_B:DN_G_B + DN_HEADS].set(dtb_b).reshape(1, LANE)
    q_c, k_c, v_c, f_c = dn_features(qkv_ctx, small_ctx, dn_conv_w, alog_row, dtb_row)
    q_l, k_l, v_l, f_l = dn_features(qkv_lat, small_lat, dn_conv_w, alog_row, dtb_row)
    s0 = jnp.zeros((b, DN_HEADS, DN_DK, DN_DK), F32)
    _, _, s_cf, s_cb = dn_scan(q_c, k_c, v_c, f_c, s0, s0)
    o_f, o_b, _, _ = dn_scan(q_l, k_l, v_l, f_l, s_cf, s_cb)
    dn = dn_output(o_f, o_b, gate_lat, dn_norm_g)

    return out_proj_mlp(x, hy, dn, w_out.astype(BF16), lat_m[2], n2g, lat_m[3], lat_m[4], lat_m[5],
                        w1.astype(BF16), w2.astype(BF16), final_g=final_g, a_transposed=True)


def kernel(x, c, ctx, c_ctx, ada_w, ada_b, norm1_g, norm2_g, mlp_w1, mlp_w2, ev_w_in, ev_conv_w, ev_conv_b, ev_ln_g,
           ev_ln_b, ev_lq1, ev_lk1, ev_lq2, ev_lk2, ev_subln_g, ev_w_out, od_w_in, od_hy_short_w, od_hy_short_b,
           od_hy_w1, od_hy_b1, od_hy_freq1, od_hy_w2, od_hy_b2, od_hy_freq2, od_hy_w3, od_hy_skip, od_dn_conv_w,
           od_dn_alog_f, od_dn_alog_b, od_dn_dtb_f, od_dn_dtb_b, od_dn_norm_g, od_w_out, final_g):
    b, _, d = x.shape
    assert ada_w.shape[0] == 2, "layer 0 = conformer/diff-attention, layer 1 (last) = Hyena/DeltaNet"
    rows = -(-(b + 1) // SUBLANE) * SUBLANE
    cvec = jnp.zeros((rows, d), F32).at[:b].set(c).at[b].set(c_ctx)

    lat_m, ctx_m = _split_mods(ada_mods(cvec, ada_w[0], ada_b[0]), b, d)
    lam_vecs = jnp.stack([ev_lq1[0], ev_lk1[0], ev_lq2[0], ev_lk2[0]])
    x, ctx = _even_layer(x, ctx, lat_m, ctx_m, norm1_g[0], norm2_g[0], ev_w_in[0], ev_conv_w[0], ev_conv_b[0],
                         ev_ln_g[0], ev_ln_b[0], lam_vecs, ev_subln_g[0], ev_w_out[0], mlp_w1[0], mlp_w2[0],
                         0.8 - 0.6 * math.exp(-0.3 * 0))

    lat_m, ctx_m = _split_mods(ada_mods(cvec, ada_w[1], ada_b[1]), b, d)
    return _odd_layer_last(x, ctx, lat_m, ctx_m, norm1_g[1], norm2_g[1], od_w_in[0], od_hy_short_w[0],
                           od_hy_short_b[0], od_hy_w1[0], od_hy_b1[0], od_hy_freq1[0], od_hy_w2[0], od_hy_b2[0],
                           od_hy_freq2[0], od_hy_w3[0], od_hy_skip[0], od_dn_conv_w[0], od_dn_alog_f[0],
                           od_dn_alog_b[0], od_dn_dtb_f[0], od_dn_dtb_b[0], od_dn_norm_g[0], od_w_out[0],
                           mlp_w1[1], mlp_w2[1], final_g)
```

```python
import functools
import math

import jax
import jax.numpy as jnp
import numpy as np
from jax import lax
from jax.experimental import pallas as pl
from jax.experimental.pallas import tpu as pltpu

F32 = jnp.float32
BF16 = jnp.bfloat16

EPS = 1e-6
N_MOD = 6
GRID_W = 64
ROPE_BASE = 10000.0
CONV_W = 31
DIFF_HEADS = 4
DIFF_DQK = 64
DIFF_DV = 128
HY_ORDER = 2
HY_EMB = 33
HY_FFN = 64
HY_MAX_DECAY = math.log(1e-2) / 0.3
HY_MIN_DECAY = math.log(1e-2) / 1.5
DN_HEADS = 4
DN_DK = 128
DN_CONV_W = 5
DN_CHUNK = 128

LANE = 128
SUBLANE = 8
VMEM_LIMIT = 56 * 1024 * 1024


def _cparams(*sem):
    return pltpu.CompilerParams(dimension_semantics=sem, vmem_limit_bytes=VMEM_LIMIT)


def _tile(n, pref, mult=SUBLANE):
    if n <= pref:
        return n
    t = (pref // mult) * mult
    while t > mult and n % t:
        t -= mult
    assert n % t == 0, (n, pref, mult)
    return t


def _const_spec(shape):
    nd = len(shape)
    return pl.BlockSpec(shape, lambda *_: (0,) * nd, pipeline_mode=pl.Buffered(1))


def _silu(x):
    return x * jax.nn.sigmoid(x)


def _norm_mod(x, g, sh, sc):
    y = x * lax.rsqrt(jnp.mean(x * x, axis=-1, keepdims=True) + EPS)
    return (y * g) * (1.0 + sc) + sh


def _mods_kernel(c_ref, w_ref, b_ref, o_ref):
    cond = _silu(c_ref[...])
    o_ref[...] = jnp.dot(cond, w_ref[...], preferred_element_type=F32,
                         precision=lax.Precision.HIGHEST) + b_ref[...]


def ada_mods(cvec, w, b):
    r, d = cvec.shape
    n = w.shape[1]
    tn = _tile(n, 512, LANE)
    return pl.pallas_call(
        _mods_kernel,
        grid=(n // tn,),
        in_specs=[pl.BlockSpec((r, d), lambda j: (0, 0)),
                  pl.BlockSpec((d, tn), lambda j: (0, j)),
                  pl.BlockSpec((1, tn), lambda j: (0, j))],
        out_specs=pl.BlockSpec((r, tn), lambda j: (0, j)),
        out_shape=jax.ShapeDtypeStruct((r, n), F32),
        compiler_params=_cparams("arbitrary"),
        name="ada_mods",
    )(cvec, w, b.reshape(1, n))


def _swap16(y):
    n = y.shape[-1]
    lane = lax.broadcasted_iota(jnp.int32, y.shape, y.ndim - 1)
    fwd = pltpu.roll(y, n - 16, y.ndim - 1)
    bwd = pltpu.roll(y, 16, y.ndim - 1)
    return jnp.where((lane % 32) < 16, fwd, bwd)


PROJ_CHUNK = 512


def _proj_kernel(x_ref, g_ref, sh_ref, sc_ref, w_ref, *rest, splits, rope_cols):
    n_out = len(splits)
    if rope_cols:
        cq_ref, sq_ref, ck_ref, sk_ref = rest[:4]
        rest = rest[4:]
    o_refs = rest[:n_out]
    h = _norm_mod(x_ref[0], g_ref[...], sh_ref[0], sc_ref[0]).astype(BF16)
    for o_ref, (s0, s1) in zip(o_refs, splits):
        tc = PROJ_CHUNK if (s1 - s0) % PROJ_CHUNK == 0 and s0 % PROJ_CHUNK == 0 else LANE
        for c0 in range(s0, s1, tc):
            y = jnp.dot(h, w_ref[:, c0:c0 + tc], preferred_element_type=F32)
            if rope_cols and rope_cols[0] <= c0 < rope_cols[2]:
                is_q = c0 < rope_cols[1]
                cos = (cq_ref if is_q else ck_ref)[...]
                sin = (sq_ref if is_q else sk_ref)[...]
                reps = tc // LANE
                cos = jnp.concatenate([cos] * reps, axis=1)
                sin = jnp.concatenate([sin] * reps, axis=1)
                y = y * cos + _swap16(y) * sin
            o_ref[0, :, c0 - s0:c0 - s0 + tc] = y.astype(o_ref.dtype)


def norm_mod_proj(x, g, sh, sc, w, splits, dtypes, rope=None, rope_cols=None):
    b, l, d = x.shape
    n = w.shape[1]
    tm = _tile(l, 512)
    assert all(s % LANE == 0 for sp in splits for s in sp)
    if rope_cols:
        assert all(c % PROJ_CHUNK == 0 for c in rope_cols)
    row = lambda bi, i: (bi, i, 0)
    in_specs = [pl.BlockSpec((1, tm, d), row),
                _const_spec((1, d)),
                pl.BlockSpec((1, 1, d), lambda bi, i: (bi, 0, 0)),
                pl.BlockSpec((1, 1, d), lambda bi, i: (bi, 0, 0)),
                _const_spec((d, n))]
    args = [x, g.reshape(1, d), sh, sc, w]
    if rope_cols:
        in_specs += [pl.BlockSpec((tm, LANE), lambda bi, i: (i, 0))] * 4
        args += list(rope)
    return pl.pallas_call(
        functools.partial(_proj_kernel, splits=tuple(splits), rope_cols=rope_cols),
        grid=(b, l // tm),
        in_specs=in_specs,
        out_specs=[pl.BlockSpec((1, tm, s1 - s0), row) for s0, s1 in splits],
        out_shape=[jax.ShapeDtypeStruct((b, l, s1 - s0), dt) for (s0, s1), dt in zip(splits, dtypes)],
        compiler_params=_cparams("parallel", "parallel"),
        name="norm_mod_proj",
    )(*args)


CONV_HALO = 16


def _conformer_kernel(ac_ref, gc_ref, ap_ref, gp_ref, an_ref, gn_ref, w_ref, b_ref, lg_ref, lb_ref, o_ref,
                      u_ref, y_ref, *, tl, ch):
    i = pl.program_id(1)
    last = pl.num_programs(1) - 1
    left = (CONV_W - 1) // 2
    glu = lambda a, g: a * jax.nn.sigmoid(g)
    u_ref[CONV_HALO:CONV_HALO + tl, :] = glu(ac_ref[0], gc_ref[0])
    prev = glu(ap_ref[0], gp_ref[0])
    nxt = glu(an_ref[0], gn_ref[0])
    u_ref[0:CONV_HALO, :] = jnp.where(i > 0, prev, 0.0)
    u_ref[CONV_HALO + tl:CONV_HALO + tl + CONV_HALO, :] = jnp.where(i < last, nxt, 0.0)
    rows = min(tl, 128)
    for r0 in range(0, tl, rows):
        for c0 in range(0, ch, LANE):
            acc = jnp.zeros((rows, LANE), F32) + b_ref[:, c0:c0 + LANE]
            for k in range(CONV_W):
                s = CONV_HALO - left + k + r0
                acc = acc + w_ref[k:k + 1, c0:c0 + LANE] * u_ref[s:s + rows, c0:c0 + LANE]
            y_ref[r0:r0 + rows, c0:c0 + LANE] = acc
    y = y_ref[...]
    mu = jnp.mean(y, axis=-1, keepdims=True)
    yc = y - mu
    var = jnp.mean(yc * yc, axis=-1, keepdims=True)
    z = yc * lax.rsqrt(var + EPS) * lg_ref[...] + lb_ref[...]
    o_ref[0] = _silu(z).astype(o_ref.dtype)


def conformer_conv(p, conv_w, conv_b, ln_g, ln_b):
    b, l, _ = p.shape
    ch = conv_w.shape[1]
    tl = _tile(l, 256, CONV_HALO)
    nh = tl // CONV_HALO
    nblk = l // CONV_HALO
    cur = lambda col: pl.BlockSpec((1, tl, ch), lambda bi, i: (bi, i, col))
    prv = lambda col: pl.BlockSpec((1, CONV_HALO, ch), lambda bi, i: (bi, jnp.maximum(i * nh - 1, 0), col))
    nxt = lambda col: pl.BlockSpec((1, CONV_HALO, ch), lambda bi, i: (bi, jnp.minimum((i + 1) * nh, nblk - 1), col))
    return pl.pallas_call(
        functools.partial(_conformer_kernel, tl=tl, ch=ch),
        grid=(b, l // tl),
        in_specs=[cur(0), cur(1), prv(0), prv(1), nxt(0), nxt(1),
                  _const_spec((CONV_W, ch)), _const_spec((1, ch)), _const_spec((1, ch)), _const_spec((1, ch))],
        out_specs=pl.BlockSpec((1, tl, ch), lambda bi, i: (bi, i, 0)),
        out_shape=jax.ShapeDtypeStruct((b, l, ch), BF16),
        scratch_shapes=[pltpu.VMEM((tl + 2 * CONV_HALO, ch), F32), pltpu.VMEM((tl, ch), F32)],
        compiler_params=_cparams("parallel", "parallel"),
        name="conformer_conv",
    )(p, p, p, p, p, p, conv_w, conv_b.reshape(1, ch), ln_g.reshape(1, ch), ln_b.reshape(1, ch))


ATTN_QBLK = 256


def _diff_attn_kernel(qt_ref, k_ref, vt_ref, lam_ref, g_ref, o_ref, qm_ref, m_ref, l_ref, acc_ref,
                      *, lambda_init):
    kv = pl.program_id(3)

    @pl.when(kv == 0)
    def _():
        qt = qt_ref[0]
        row = lax.broadcasted_iota(jnp.int32, qt.shape, 0)
        zero = jnp.zeros_like(qt)
        qm_ref[0] = jnp.where(row < DIFF_DQK, qt, zero)
        qm_ref[1] = jnp.where(row < DIFF_DQK, zero, qt)
        m_ref[...] = jnp.full(m_ref.shape, -jnp.inf, F32)
        l_ref[...] = jnp.zeros(l_ref.shape, F32)
        acc_ref[...] = jnp.zeros(acc_ref.shape, F32)

    k = k_ref[0]
    vt = vt_ref[0]
    qblk = min(ATTN_QBLK, qm_ref.shape[2])
    chains = [(mi, slice(q0, q0 + qblk)) for q0 in range(0, qm_ref.shape[2], qblk) for mi in range(2)]
    scores = [jnp.dot(k, qm_ref[mi, :, qs], preferred_element_type=F32) for mi, qs in chains]
    for (mi, qs), st in zip(chains, scores):
        m_old = m_ref[mi, :, qs]
        m_new = jnp.maximum(m_old, jnp.max(st, axis=0, keepdims=True))
        alpha = jnp.exp2(m_old - m_new)
        e = jnp.exp2(st - m_new)
        l_ref[mi, :, qs] = alpha * l_ref[mi, :, qs] + jnp.sum(e, axis=0, keepdims=True)
        acc_ref[mi, :, qs] = alpha * acc_ref[mi, :, qs] + jnp.dot(vt, e.astype(BF16), preferred_element_type=F32)
        m_ref[mi, :, qs] = m_new

    @pl.when(kv == pl.num_programs(3) - 1)
    def _():
        lam = (jnp.exp(jnp.sum(lam_ref[0:1, :] * lam_ref[1:2, :], axis=-1, keepdims=True))
               - jnp.exp(jnp.sum(lam_ref[2:3, :] * lam_ref[3:4, :], axis=-1, keepdims=True)) + lambda_init)
        ot = acc_ref[0] / l_ref[0] - lam * (acc_ref[1] / l_ref[1])
        ot = ot * lax.rsqrt(jnp.mean(ot * ot, axis=0, keepdims=True) + EPS)
        o = ot.T * g_ref[...]
        o_ref[0] = (o * (1.0 - lambda_init)).astype(o_ref.dtype)


def diff_attention(qt, k, vt, lam_vecs, subln_g, lambda_init, k_blk):
    b, _, lq = qt.shape
    lk = k.shape[1]
    tq = _tile(lq, 1024, LANE)
    tk = _tile(lk, 1536, LANE)
    return pl.pallas_call(
        functools.partial(_diff_attn_kernel, lambda_init=lambda_init),
        grid=(b, DIFF_HEADS, lq // tq, lk // tk),
        in_specs=[pl.BlockSpec((1, LANE, tq), lambda bi, h, i, j: (bi, h, i)),
                  pl.BlockSpec((1, tk, LANE), lambda bi, h, i, j: (bi, j, k_blk + h)),
                  pl.BlockSpec((1, LANE, tk), lambda bi, h, i, j: (bi, h, j)),
                  _const_spec((4, DIFF_DQK)), _const_spec((1, DIFF_DV))],
        out_specs=pl.BlockSpec((1, tq, LANE), lambda bi, h, i, j: (bi, i, h)),
        out_shape=jax.ShapeDtypeStruct((b, lq, DIFF_HEADS * DIFF_DV), BF16),
        scratch_shapes=[pltpu.VMEM((2, LANE, tq), BF16),
                        pltpu.VMEM((2, 1, tq), F32), pltpu.VMEM((2, 1, tq), F32),
                        pltpu.VMEM((2, DIFF_DV, tq), F32)],
        compiler_params=_cparams("parallel", "parallel", "parallel", "arbitrary"),
        name="diff_attention",
    )(qt, k, vt, lam_vecs, subln_g.reshape(1, DIFF_DV))


def _out_mlp_kernel(x_ref, a_ref, b_ref, wo_ref, gt1_ref, g2_ref, sh2_ref, sc2_ref, gt2_ref, w1_ref, w2_ref,
                    fg_ref, o_ref, *, a_transposed, hidden_chunk, final_norm):
    half = wo_ref.shape[0] // 2
    if a_transposed:
        ya = lax.dot_general(a_ref[0], wo_ref[0:half, :], (((0,), (0,)), ((), ())), preferred_element_type=F32)
    else:
        ya = jnp.dot(a_ref[0], wo_ref[0:half, :], preferred_element_type=F32)
    y = ya + jnp.dot(b_ref[0], wo_ref[half:, :], preferred_element_type=F32)
    x1 = x_ref[0] + gt1_ref[0] * y
    h = _norm_mod(x1, g2_ref[...], sh2_ref[0], sc2_ref[0]).astype(BF16)
    acc = jnp.zeros_like(x1)
    for c0 in range(0, w1_ref.shape[1], hidden_chunk):
        t = jnp.maximum(jnp.dot(h, w1_ref[:, c0:c0 + hidden_chunk], preferred_element_type=F32), 0.0)
        acc = acc + jnp.dot((t * t).astype(BF16), w2_ref[c0:c0 + hidden_chunk, :], preferred_element_type=F32)
    x2 = x1 + gt2_ref[0] * acc
    if final_norm:
        x2 = x2 * lax.rsqrt(jnp.mean(x2 * x2, axis=-1, keepdims=True) + EPS) * fg_ref[...]
    o_ref[0] = x2


def out_proj_mlp(x, mix_a, mix_b, w_out, gt1, g2, sh2, sc2, gt2, w1, w2, final_g=None, a_transposed=False):
    b, l, d = x.shape
    c = mix_b.shape[2]
    hid = w1.shape[1]
    tm = _tile(l, 512, LANE)
    row = lambda bi, i: (bi, i, 0)
    vec = lambda bi, i: (bi, 0, 0)
    a_spec = (pl.BlockSpec((1, c, tm), lambda bi, i: (bi, 0, i)) if a_transposed
              else pl.BlockSpec((1, tm, c), row))
    fg = jnp.ones((1, d), F32) if final_g is None else final_g.reshape(1, d)
    return pl.pallas_call(
        functools.partial(_out_mlp_kernel, a_transposed=a_transposed, hidden_chunk=min(hid, 1024),
                          final_norm=final_g is not None),
        grid=(b, l // tm),
        in_specs=[pl.BlockSpec((1, tm, d), row), a_spec, pl.BlockSpec((1, tm, c), row),
                  _const_spec((2 * c, d)),
                  pl.BlockSpec((1, 1, d), vec), _const_spec((1, d)),
                  pl.BlockSpec((1, 1, d), vec), pl.BlockSpec((1, 1, d), vec), pl.BlockSpec((1, 1, d), vec),
                  _const_spec((d, hid)), _const_spec((hid, d)), _const_spec((1, d))],
        out_specs=pl.BlockSpec((1, tm, d), row),
        out_shape=jax.ShapeDtypeStruct((b, l, d), F32),
        compiler_params=_cparams("parallel", "parallel"),
        name="out_proj_mlp",
    )(x, mix_a, mix_b, w_out, gt1, g2.reshape(1, d), sh2, sc2, gt2, w1, w2, fg)


def _rope_tables(length, scale):
    f32 = np.float32
    rows = length // GRID_W
    row = np.repeat(np.arange(rows, dtype=f32), GRID_W)
    col = np.tile(np.arange(GRID_W, dtype=f32), rows)
    n_freq = DIFF_DQK // 4
    inv = (f32(ROPE_BASE) ** (-np.arange(n_freq, dtype=f32) / f32(n_freq))).astype(f32)
    ang_r = row[:, None] * inv
    ang_c = col[:, None] * inv
    cos = np.concatenate([np.cos(ang_r)] * 2 + [np.cos(ang_c)] * 2, axis=-1)
    sin = np.concatenate([-np.sin(ang_r), np.sin(ang_r), -np.sin(ang_c), np.sin(ang_c)], axis=-1)
    cos = np.concatenate([cos, cos], axis=-1) * f32(scale)
    sin = np.concatenate([sin, sin], axis=-1) * f32(scale)
    return cos.astype(f32), sin.astype(f32)


def _flat_tables(length, scale):
    return np.full((length, LANE), scale, np.float32), np.zeros((length, LANE), np.float32)


def _proj_t_kernel(x_ref, g_ref, sh_ref, sc_ref, wt_ref, o_ref, *, rc):
    h = _norm_mod(x_ref[0], g_ref[...], sh_ref[0], sc_ref[0]).astype(BF16)
    for r0 in range(0, wt_ref.shape[0], rc):
        o_ref[0, r0:r0 + rc, :] = lax.dot_general(wt_ref[r0:r0 + rc, :], h, (((1,), (1,)), ((), ())),
                                                  preferred_element_type=F32)


def norm_mod_proj_t(x, g, sh, sc, wt):
    b, l, d = x.shape
    c = wt.shape[0]
    tl = _tile(l, 512, LANE)
    return pl.pallas_call(
        functools.partial(_proj_t_kernel, rc=_tile(c, 512)),
        grid=(b, l // tl),
        in_specs=[pl.BlockSpec((1, tl, d), lambda bi, i: (bi, i, 0)),
                  _const_spec((1, d)),
                  pl.BlockSpec((1, 1, d), lambda bi, i: (bi, 0, 0)),
                  pl.BlockSpec((1, 1, d), lambda bi, i: (bi, 0, 0)),
                  _const_spec((c, d))],
        out_specs=pl.BlockSpec((1, c, tl), lambda bi, i: (bi, 0, i)),
        out_shape=jax.ShapeDtypeStruct((b, c, l), F32),
        compiler_params=_cparams("parallel", "parallel"),
        name="norm_mod_proj_t",
    )(x, g.reshape(1, d), sh, sc, wt)


def _dot_hi(a, b):
    return jnp.dot(a, b, preferred_element_type=F32, precision=lax.Precision.HIGHEST)


def _hy_hidden_kernel(z_ref, w1_ref, b1_ref, f1_ref, w2_ref, b2_ref, f2_ref, o_ref):
    hid = jnp.sin(f1_ref[...] * (_dot_hi(w1_ref[...], z_ref[...]) + b1_ref[...]))
    o_ref[...] = jnp.sin(f2_ref[...] * (_dot_hi(w2_ref[...], hid) + b2_ref[...]))


def hyena_hidden(zt, w1t, b1, f1, w2t, b2, f2):
    e, n = zt.shape
    f = w1t.shape[0]
    tn = _tile(n, 2048, LANE)
    col = lambda v: v.reshape(f, 1)
    return pl.pallas_call(
        _hy_hidden_kernel,
        grid=(n // tn,),
        in_specs=[pl.BlockSpec((e, tn), lambda j: (0, j)),
                  _const_spec((f, e)), _const_spec((f, 1)), _const_spec((f, 1)),
                  _const_spec((f, f)), _const_spec((f, 1)), _const_spec((f, 1))],
        out_specs=pl.BlockSpec((f, tn), lambda j: (0, j)),
        out_shape=jax.ShapeDtypeStruct((f, n), F32),
        compiler_params=_cparams("parallel"),
        name="hyena_hidden",
    )(zt, w1t, col(b1), col(f1), w2t, col(b2), col(f2))


def _hy_filter_kernel(hid_ref, t_ref, wf_ref, wb_ref, dl_ref, o_ref, *, length, tn):
    n2 = hid_ref.shape[1]
    delta = dl_ref[...]

    def piece(c0):
        hid = hid_ref[:, c0:c0 + tn]
        dec = jnp.exp(-t_ref[:, c0:c0 + tn] * delta)
        return _dot_hi(wf_ref[...], hid) * dec, _dot_hi(wb_ref[...], hid) * dec

    norm = jnp.zeros((wf_ref.shape[0], 1), F32)
    for c0 in range(length, n2, tn):
        ff, fb = piece(c0)
        norm = norm + jnp.sum(jnp.abs(ff) + jnp.abs(fb), axis=-1, keepdims=True)
    inv = 1.0 / norm
    for c0 in range(0, n2, tn):
        ff, fb = piece(c0)
        if c0 >= length:
            o_ref[0, :, c0:c0 + tn] = ff * inv
        else:
            col = lax.broadcasted_iota(jnp.int32, fb.shape, 1) + c0
            o_ref[0, :, c0:c0 + tn] = jnp.where(col == 0, 0.0, fb * inv)


def hyena_filters(hid, t_row, w3t, deltas, length):
    f, n2 = hid.shape
    ch = deltas.shape[0]
    cb = LANE
    nblk = ch // cb
    tn = _tile(length, 2048, LANE)
    return pl.pallas_call(
        functools.partial(_hy_filter_kernel, length=length, tn=tn),
        grid=(HY_ORDER, nblk),
        in_specs=[_const_spec((f, n2)), _const_spec((1, n2)),
                  pl.BlockSpec((cb, f), lambda o, c: (o * 2 * nblk + c, 0)),
                  pl.BlockSpec((cb, f), lambda o, c: (o * 2 * nblk + nblk + c, 0)),
                  pl.BlockSpec((cb, 1), lambda o, c: (c, 0))],
        out_specs=pl.BlockSpec((1, cb, n2), lambda o, c: (o, c, 0)),
        out_shape=jax.ShapeDtypeStruct((HY_ORDER, ch, n2), F32),
        compiler_params=_cparams("parallel", "parallel"),
        name="hyena_filters",
    )(hid, t_row, w3t, w3t, deltas.reshape(ch, 1))


HY_T = 256
HY_CB = SUBLANE


def _hyena_kernel(v_ref, x1_ref, x2_ref, g_ref, sw_ref, sb_ref, sk_ref_s, o_ref, sk_ref, u_ref, acc_ref,
                  *, length, nch):
    nb = v_ref.shape[0]
    nj = length // HY_T
    cblk = pl.program_id(0)
    lane = lax.broadcasted_iota(jnp.int32, (nb, length), 1)

    def short(ref, cc, part):
        x = ref[:, cc, :]
        ch = part * nch + cblk * HY_CB + cc
        prev = jnp.where(lane == 0, 0.0, pltpu.roll(x, 1, 1))
        nxt = jnp.where(lane == length - 1, 0.0, pltpu.roll(x, length - 1, 1))
        return sw_ref[0, ch] * prev + sw_ref[1, ch] * x + sw_ref[2, ch] * nxt + sb_ref[ch]

    def long_conv(u, order, cc):
        g = jnp.broadcast_to(g_ref[order, pl.ds(cc, 1), :], (SUBLANE, 2 * length))
        half = HY_T // 2
        for a2 in range(half // (2 * SUBLANE)):
            lo = pltpu.roll(g, 2 * a2 * SUBLANE, 1, stride=1, stride_axis=0)
            hi = pltpu.roll(g, (2 * a2 + 1) * SUBLANE, 1, stride=1, stride_axis=0)
            pair = jnp.concatenate([lo, hi], axis=0).astype(BF16)
            r0 = 2 * a2 * SUBLANE
            sk_ref[r0:r0 + 2 * SUBLANE, :] = pair
            sk_ref[half + r0:half + r0 + 2 * SUBLANE, half:] = pair[:, :2 * length - half]
        for j in range(nj):
            u_ref[j * nb:(j + 1) * nb, :] = u[:, j * HY_T:(j + 1) * HY_T]
        acc_ref[...] = jnp.zeros(acc_ref.shape, F32)
        for d in range(-(nj - 1), nj):
            m = (nj - abs(d)) * nb
            lo_in, lo_out = max(0, -d) * nb, max(0, d) * nb
            tile = sk_ref[:, length + d * HY_T:length + (d + 1) * HY_T]
            acc_ref[lo_out:lo_out + m, :] += jnp.dot(u_ref[lo_in:lo_in + m, :].astype(BF16), tile,
                                                     preferred_element_type=F32)
        return jnp.concatenate([acc_ref[i * nb:(i + 1) * nb, :] for i in range(nj)], axis=1)

    def channel(cc, carry):
        v = short(v_ref, cc, 0)
        x1 = short(x1_ref, cc, 1)
        x2 = short(x2_ref, cc, 2)
        ch = cblk * HY_CB + cc
        z = x1 * (long_conv(v, 0, cc) + sk_ref_s[0, ch] * v)
        z = x2 * (long_conv(z, 1, cc) + sk_ref_s[1, ch] * z)
        o_ref[:, cc, :] = z
        return carry

    lax.fori_loop(0, HY_CB, channel, 0)


def hyena_mix(pt, filt, short_w, short_b, skip):
    b, c3, l = pt.shape
    nch = c3 // 3
    assert l % HY_T == 0 and nch % HY_CB == 0 and b == SUBLANE
    nblk = nch // HY_CB
    smem = pl.BlockSpec(memory_space=pltpu.SMEM)
    part = lambda k: pl.BlockSpec((b, HY_CB, l), lambda c: (0, k * nblk + c, 0))
    return pl.pallas_call(
        functools.partial(_hyena_kernel, length=l, nch=nch),
        grid=(nblk,),
        in_specs=[part(0), part(1), part(2),
                  pl.BlockSpec((HY_ORDER, HY_CB, 2 * l), lambda c: (0, c, 0)),
                  smem, smem, smem],
        out_specs=pl.BlockSpec((b, HY_CB, l), lambda c: (0, c, 0)),
        out_shape=jax.ShapeDtypeStruct((b, nch, l), F32),
        scratch_shapes=[pltpu.VMEM((HY_T, 2 * l), BF16), pltpu.VMEM((l // HY_T * b, HY_T), F32),
                        pltpu.VMEM((l // HY_T * b, HY_T), F32)],
        compiler_params=_cparams("parallel"),
        name="hyena_mix",
    )(pt, pt, pt, filt, short_w, short_b, skip)


DN_HALO = SUBLANE
DN_BETA_F, DN_BETA_B, DN_G_F, DN_G_B = 0, DN_HEADS, 2 * DN_HEADS, 3 * DN_HEADS


def _dn_feat_kernel(c_ref, p_ref, n_ref, s_ref, w_ref, al_ref, dt_ref, q_ref, k_ref, v_ref, f_ref, u_ref, *, tl):
    i = pl.program_id(1)
    last = pl.num_programs(1) - 1
    left = (DN_CONV_W - 1) // 2
    nc = c_ref.shape[2]
    u_ref[DN_HALO:DN_HALO + tl, :] = c_ref[0]
    u_ref[0:DN_HALO, :] = jnp.where(i > 0, p_ref[0], 0.0)
    u_ref[DN_HALO + tl:DN_HALO + tl + DN_HALO, :] = jnp.where(i < last, n_ref[0], 0.0)
    rows = min(tl, 128)
    outs = (q_ref, k_ref, v_ref)
    per = nc // len(outs)
    for r0 in range(0, tl, rows):
        for c0 in range(0, nc, LANE):
            acc = jnp.zeros((rows, LANE), F32)
            for k in range(DN_CONV_W):
                s = DN_HALO - left + k + r0
                acc = acc + w_ref[k:k + 1, c0:c0 + LANE] * u_ref[s:s + rows, c0:c0 + LANE]
            y = _silu(acc)
            which, off = divmod(c0, per)
            if which < 2:
                y = y * lax.rsqrt(jnp.sum(y * y, axis=-1, keepdims=True) + 1e-6)
            outs[which][0, r0:r0 + rows, off:off + LANE] = y
    s = s_ref[0]
    lane = lax.broadcasted_iota(jnp.int32, s.shape, 1)
    gate = -jnp.exp(al_ref[...]) * jax.nn.softplus(s + dt_ref[...])
    f_ref[0] = jnp.where(lane < DN_G_F, jax.nn.sigmoid(s), gate)


def dn_features(qkv, small, conv_w, alog_row, dtb_row):
    b, l, nc = qkv.shape
    tl = _tile(l, 256, DN_HALO)
    nh = tl // DN_HALO
    nblk = l // DN_HALO
    w = nc // 3
    row = lambda bi, i: (bi, i, 0)
    return pl.pallas_call(
        functools.partial(_dn_feat_kernel, tl=tl),
        grid=(b, l // tl),
        in_specs=[pl.BlockSpec((1, tl, nc), row),
                  pl.BlockSpec((1, DN_HALO, nc), lambda bi, i: (bi, jnp.maximum(i * nh - 1, 0), 0)),
                  pl.BlockSpec((1, DN_HALO, nc), lambda bi, i: (bi, jnp.minimum((i + 1) * nh, nblk - 1), 0)),
                  pl.BlockSpec((1, tl, LANE), row),
                  _const_spec((DN_CONV_W, nc)), _const_spec((1, LANE)), _const_spec((1, LANE))],
        out_specs=[pl.BlockSpec((1, tl, w), row)] * 3 + [pl.BlockSpec((1, tl, LANE), row)],
        out_shape=[jax.ShapeDtypeStruct((b, l, w), F32)] * 3 + [jax.ShapeDtypeStruct((b, l, LANE), F32)],
        scratch_shapes=[pltpu.VMEM((tl + 2 * DN_HALO, nc), F32)],
        compiler_params=_cparams("parallel", "parallel"),
        name="dn_features",
    )(qkv, qkv, qkv, small, conv_w, alog_row, dtb_row)


def _bdot(a, b):
    return jnp.dot(a.astype(BF16), b.astype(BF16), preferred_element_type=F32)


def _bdot_nt(a, b):
    return lax.dot_general(a.astype(BF16), b.astype(BF16), (((1,), (1,)), ((), ())), preferred_element_type=F32)


def _bdot_tn(a, b):
    return lax.dot_general(a.astype(BF16), b.astype(BF16), (((0,), (0,)), ((), ())), preferred_element_type=F32)


def _cumsum_rows(x, reverse):
    n = x.shape[0]
    row = lax.broadcasted_iota(jnp.int32, x.shape, 0)
    s = 1
    while s < n:
        if reverse:
            x = x + jnp.where(row < n - s, pltpu.roll(x, n - s, 0), 0.0)
        else:
            x = x + jnp.where(row >= s, pltpu.roll(x, s, 0), 0.0)
        s *= 2
    return x


def _dn_chunks(chains):
    c = chains[0]["k"].shape[0]
    ri = lax.broadcasted_iota(jnp.int32, (c, c), 0)
    ci = lax.broadcasted_iota(jnp.int32, (c, c), 1)
    work = []
    for ch in chains:
        rev = ch["reverse"]
        incl = (ri <= ci) if rev else (ri >= ci)
        strict = (ri < ci) if rev else (ri > ci)
        decay = jnp.where(incl, jnp.exp(jnp.where(incl, ch["gcol"] - ch["grow"], 0.0)), 0.0)
        kb = ch["k"] * ch["beta"]
        qs = ch["q"] * (ch["k"].shape[1] ** -0.5)
        work.append(dict(ch, strict=strict, decay=decay, kb=kb, qs=qs))
    grams = [_bdot_nt(jnp.concatenate([w["kb"], w["qs"]], axis=0), w["k"]) for w in work]
    lowers = [jnp.where(w["strict"], g[:c] * w["decay"], 0.0) for w, g in zip(work, grams)]
    same2 = (ri // 2) == (ci // 2)
    ymats = [-jnp.where(same2, lo, 0.0) for lo in lowers]
    m = 2
    while m < c:
        pair = ((ri // (2 * m)) == (ci // (2 * m))) & ((ri // m) != (ci // m))
        offs = [jnp.where(pair, lo, 0.0) for lo in lowers]
        zs = [off + _bdot(ym, off) for ym, off in zip(ymats, offs)]
        ymats = [ym - z - _bdot(z, ym) for ym, z in zip(ymats, zs)]
        m *= 2
    outs = []
    uws, e_gs = [], []
    for w, ym in zip(work, ymats):
        e_g = jnp.exp(w["gcol"])
        rhs = jnp.concatenate([w["v"] * w["beta"], w["kb"] * e_g], axis=1)
        uws.append(rhs + _bdot(ym, rhs))
        e_gs.append(e_g)
    dv = work[0]["v"].shape[1]
    wss = [_bdot(jnp.concatenate([uw[:, dv:], w["qs"] * e_g], axis=0), w["state"])
           for w, uw, e_g in zip(work, uws, e_gs)]
    v_news = [uw[:, :dv] - ws[:c] for uw, ws in zip(uws, wss)]
    for w, g, ws, v_new in zip(work, grams, wss, v_news):
        gcol = w["gcol"]
        g_last = gcol[0:1, :] if w["reverse"] else gcol[c - 1:c, :]
        k_dec = w["k"] * jnp.exp(g_last - gcol)
        o = ws[c:] + _bdot(g[c:] * w["decay"], v_new)
        new_state = w["state"] * jnp.exp(g_last) + _bdot_tn(k_dec, v_new)
        outs.append((o, new_state))
    return outs


def _dn_scan_kernel(qf_ref, kf_ref, vf_ref, ff_ref, qb_ref, kb_ref, vb_ref, fb_ref, s0f_ref, s0b_ref,
                    of_ref, ob_ref, sf_ref, sb_ref, st_ref):
    n = pl.program_id(1)

    @pl.when(n == 0)
    def _():
        st_ref[0] = s0f_ref[0]
        st_ref[1] = s0b_ref[0]

    dirs = ((qf_ref, kf_ref, vf_ref, ff_ref, of_ref, False), (qb_ref, kb_ref, vb_ref, fb_ref, ob_ref, True))
    chains, dests = [], []
    for di, (q_ref, k_ref, v_ref, f_ref, o_ref, reverse) in enumerate(dirs):
        feats = f_ref[0]
        gc = _cumsum_rows(feats, reverse)
        gct = gc.T
        for h in range(DN_HEADS):
            cols = slice(h * DN_DK, (h + 1) * DN_DK)
            bl = (DN_BETA_B if reverse else DN_BETA_F) + h
            gl = (DN_G_B if reverse else DN_G_F) + h
            chains.append(dict(q=q_ref[0, :, cols], k=k_ref[0, :, cols], v=v_ref[0, :, cols],
                               beta=feats[:, bl:bl + 1], gcol=gc[:, gl:gl + 1], grow=gct[gl:gl + 1, :],
                               state=st_ref[di, h], reverse=reverse))
            dests.append((o_ref, cols, di, h))
    for (o_ref, cols, di, h), (o, s_new) in zip(dests, _dn_chunks(chains)):
        o_ref[0, :, cols] = o
        st_ref[di, h] = s_new

    @pl.when(n == pl.num_programs(1) - 1)
    def _():
        sf_ref[0] = st_ref[0]
        sb_ref[0] = st_ref[1]


def dn_scan(q, k, v, feats, s0f, s0b):
    b, l, w = q.shape
    nchunk = l // DN_CHUNK
    fwd = lambda bi, n: (bi, n, 0)
    bwd = lambda bi, n: (bi, nchunk - 1 - n, 0)
    st = lambda bi, n: (bi, 0, 0, 0)
    blk = lambda im, width: pl.BlockSpec((1, DN_CHUNK, width), im)
    st_spec = pl.BlockSpec((1, DN_HEADS, DN_DK, DN_DK), st)
    st_shape = jax.ShapeDtypeStruct((b, DN_HEADS, DN_DK, DN_DK), F32)
    return pl.pallas_call(
        _dn_scan_kernel,
        grid=(b, nchunk),
        in_specs=[blk(fwd, w), blk(fwd, w), blk(fwd, w), blk(fwd, LANE),
                  blk(bwd, w), blk(bwd, w), blk(bwd, w), blk(bwd, LANE), st_spec, st_spec],
        out_specs=[blk(fwd, w), blk(bwd, w), st_spec, st_spec],
        out_shape=[jax.ShapeDtypeStruct((b, l, w), F32)] * 2 + [st_shape, st_shape],
        scratch_shapes=[pltpu.VMEM((2, DN_HEADS, DN_DK, DN_DK), F32)],
        compiler_params=_cparams("parallel", "arbitrary"),
        name="dn_scan",
    )(q, k, v, feats, q, k, v, feats, s0f, s0b)


def _dn_out_kernel(of_ref, ob_ref, gate_ref, g_ref, o_ref):
    o = of_ref[0] + ob_ref[0]
    for c0 in range(0, o.shape[1], DN_DK):
        t = o[:, c0:c0 + DN_DK]
        t = t * lax.rsqrt(jnp.mean(t * t, axis=-1, keepdims=True) + EPS) * g_ref[...]
        o_ref[0, :, c0:c0 + DN_DK] = (t * _silu(gate_ref[0, :, c0:c0 + DN_DK])).astype(o_ref.dtype)


def dn_output(o_f, o_b, gate, norm_g):
    b, l, w = o_f.shape
    tl = _tile(l, 512)
    row = lambda bi, i: (bi, i, 0)
    return pl.pallas_call(
        _dn_out_kernel,
        grid=(b, l // tl),
        in_specs=[pl.BlockSpec((1, tl, w), row)] * 3 + [_const_spec((1, DN_DK))],
        out_specs=pl.BlockSpec((1, tl, w), row),
        out_shape=jax.ShapeDtypeStruct((b, l, w), BF16),
        compiler_params=_cparams("parallel", "parallel"),
        name="dn_output",
    )(o_f, o_b, gate, norm_g.reshape(1, DN_DK))


def _split_mods(mods, b, d):
    lat = [mods[:b, k * d:(k + 1) * d].reshape(b, 1, d) for k in range(N_MOD)]
    ctx = [jnp.broadcast_to(mods[b, k * d:(k + 1) * d].reshape(1, 1, d), (b, 1, d)) for k in range(N_MOD)]
    return lat, ctx


def _even_layer(x, ctx, lat_m, ctx_m, n1g, n2g, w_in, conv_w, conv_b, ln_g, ln_b, lam_vecs, subln_g, w_out,
                w1, w2, lambda_init):
    b, l, d = x.shape
    lc = ctx.shape[1]
    ch = conv_w.shape[1]
    qk = DIFF_HEADS * 2 * DIFF_DQK
    q0, k0, v0, n_in = 2 * ch, 2 * ch + qk, 2 * ch + 2 * qk, 2 * ch + 2 * qk + DIFF_HEADS * DIFF_DV
    scale = DIFF_DQK ** -0.5 * math.log2(math.e)
    splits = [(0, q0), (q0, n_in)]
    w_in = w_in.astype(BF16)
    rope_lat = _rope_tables(l, scale) + _rope_tables(l, 1.0)
    rope_ctx = _flat_tables(lc, scale) + _flat_tables(lc, 1.0)
    pc_lat, qkv_lat = norm_mod_proj(x, n1g, lat_m[0], lat_m[1], w_in, splits, (F32, BF16), rope_lat, (q0, k0, v0))
    pc_ctx, qkv_ctx = norm_mod_proj(ctx, n1g, ctx_m[0], ctx_m[1], w_in, splits, (F32, BF16), rope_ctx, (q0, k0, v0))
    conv_lat = conformer_conv(pc_lat, conv_w, conv_b, ln_g, ln_b)
    conv_ctx = conformer_conv(pc_ctx, conv_w, conv_b, ln_g, ln_b)
    k_all = jnp.concatenate([qkv_lat[:, :, qk:2 * qk], qkv_ctx[:, :, qk:2 * qk]], axis=1)
    vt_ctx = jnp.swapaxes(qkv_ctx[:, :, 2 * qk:], 1, 2)
    vt_all = jnp.concatenate([jnp.swapaxes(qkv_lat[:, :, 2 * qk:], 1, 2), vt_ctx], axis=2)
    qt_lat = jnp.swapaxes(qkv_lat[:, :, :qk], 1, 2)
    qt_ctx = jnp.swapaxes(qkv_ctx[:, :, :qk], 1, 2)
    o_lat = diff_attention(qt_lat, k_all, vt_all, lam_vecs, subln_g, lambda_init, 0)
    o_ctx = diff_attention(qt_ctx, qkv_ctx, vt_ctx, lam_vecs, subln_g, lambda_init, qk // LANE)
    w_out, w1, w2 = w_out.astype(BF16), w1.astype(BF16), w2.astype(BF16)
    x = out_proj_mlp(x, conv_lat, o_lat, w_out, lat_m[2], n2g, lat_m[3], lat_m[4], lat_m[5], w1, w2)
    ctx = out_proj_mlp(ctx, conv_ctx, o_ctx, w_out, ctx_m[2], n2g, ctx_m[3], ctx_m[4], ctx_m[5], w1, w2)
    return x, ctx


def _hyena_position_tables(length):
    f32 = np.float32
    t = np.linspace(0.0, 1.0, length, dtype=f32)[:, None]
    bands = (HY_EMB - 1) // 2
    omega = (f32(2.0 * math.pi) * np.arange(length, dtype=f32)[:, None] / f32(length)).astype(f32)
    ang = omega * np.linspace(1e-4, bands - 1, bands, dtype=f32)
    z = np.concatenate([t, np.cos(ang), -np.sin(ang)], axis=-1).astype(f32)
    pos = np.minimum(np.abs(np.arange(2 * length) - length), length - 1)
    pad = -HY_EMB % SUBLANE
    zt = np.pad(z[pos].T, ((0, pad), (0, 0)))
    return np.ascontiguousarray(zt), np.ascontiguousarray(t[pos].T)


def _odd_layer_last(x, ctx, lat_m, ctx_m, n1g, n2g, w_in, hy_short_w, hy_short_b, hy_w1, hy_b1, hy_f1, hy_w2, hy_b2,
                    hy_f2, hy_w3, hy_skip, dn_conv_w, alog_f, alog_b, dtb_f, dtb_b, dn_norm_g, w_out, w1, w2,
                    final_g):
    b, l, d = x.shape
    hy_in = hy_short_w.shape[1]
    hy_ch = hy_in // (HY_ORDER + 1)
    dn_w = DN_HEADS * DN_DK
    n_rest = w_in.shape[1] - hy_in
    n_pad = -n_rest % LANE
    w_rest = jnp.pad(w_in[:, hy_in:], ((0, 0), (0, n_pad))).astype(BF16)
    splits = [(0, dn_w), (dn_w, 4 * dn_w), (4 * dn_w, n_rest + n_pad)]
    f32x3 = (F32, F32, F32)
    gate_lat, qkv_lat, small_lat = norm_mod_proj(x, n1g, lat_m[0], lat_m[1], w_rest, splits, f32x3)
    _, qkv_ctx, small_ctx = norm_mod_proj(ctx, n1g, ctx_m[0], ctx_m[1], w_rest, splits, f32x3)

    pt = norm_mod_proj_t(x, n1g, lat_m[0], lat_m[1], w_in[:, :hy_in].T.astype(BF16))
    zt, t_row = _hyena_position_tables(l)
    w1t = jnp.pad(hy_w1.T, ((0, 0), (0, zt.shape[0] - hy_w1.shape[0])))
    hid = hyena_hidden(zt, w1t, hy_b1, hy_f1, hy_w2.T, hy_b2, hy_f2)
    deltas = jnp.abs(jnp.linspace(HY_MIN_DECAY, HY_MAX_DECAY, hy_ch, dtype=F32))
    filt = hyena_filters(hid, t_row, hy_w3.T, deltas, l)
    hy = hyena_mix(pt, filt, hy_short_w, hy_short_b, hy_skip)

    lanes = jnp.zeros((LANE,), F32)
    alog_row = lanes.at[DN_G_F:DN_G_F + DN_HEADS].set(alog_f).at[DN_G_B:DN_G_B + DN_HEADS].set(alog_b).reshape(1, LANE)
    dtb_row = lanes.at[DN_G_F:DN_G_F + DN_HEADS].set(dtb_f).at[DN_G_B:DN_G_B + DN_HEADS].set(dtb_b).reshape(1, LANE)
    q_c, k_c, v_c, f_c = dn_features(qkv_ctx, small_ctx, dn_conv_w, alog_row, dtb_row)
    q_l, k_l, v_l, f_l = dn_features(qkv_lat, small_lat, dn_conv_w, alog_row, dtb_row)
    s0 = jnp.zeros((b, DN_HEADS, DN_DK, DN_DK), F32)
    _, _, s_cf, s_cb = dn_scan(q_c, k_c, v_c, f_c, s0, s0)
    o_f, o_b, _, _ = dn_scan(q_l, k_l, v_l, f_l, s_cf, s_cb)
    dn = dn_output(o_f, o_b, gate_lat, dn_norm_g)

    return out_proj_mlp(x, hy, dn, w_out.astype(BF16), lat_m[2], n2g, lat_m[3], lat_m[4], lat_m[5],
                        w1.astype(BF16), w2.astype(BF16), final_g=final_g, a_transposed=True)


def kernel(x, c, ctx, c_ctx, ada_w, ada_b, norm1_g, norm2_g, mlp_w1, mlp_w2, ev_w_in, ev_conv_w, ev_conv_b, ev_ln_g,
           ev_ln_b, ev_lq1, ev_lk1, ev_lq2, ev_lk2, ev_subln_g, ev_w_out, od_w_in, od_hy_short_w, od_hy_short_b,
           od_hy_w1, od_hy_b1, od_hy_freq1, od_hy_w2, od_hy_b2, od_hy_freq2, od_hy_w3, od_hy_skip, od_dn_conv_w,
           od_dn_alog_f, od_dn_alog_b, od_dn_dtb_f, od_dn_dtb_b, od_dn_norm_g, od_w_out, final_g):
    b, _, d = x.shape
    assert ada_w.shape[0] == 2, "layer 0 = conformer/diff-attention, layer 1 (last) = Hyena/DeltaNet"
    rows = -(-(b + 1) // SUBLANE) * SUBLANE
    cvec = jnp.zeros((rows, d), F32).at[:b].set(c).at[b].set(c_ctx)

    lat_m, ctx_m = _split_mods(ada_mods(cvec, ada_w[0], ada_b[0]), b, d)
    lam_vecs = jnp.stack([ev_lq1[0], ev_lk1[0], ev_lq2[0], ev_lk2[0]])
    x, ctx = _even_layer(x, ctx, lat_m, ctx_m, norm1_g[0], norm2_g[0], ev_w_in[0], ev_conv_w[0], ev_conv_b[0],
                         ev_ln_g[0], ev_ln_b[0], lam_vecs, ev_subln_g[0], ev_w_out[0], mlp_w1[0], mlp_w2[0],
                         0.8 - 0.6 * math.exp(-0.3 * 0))

    lat_m, ctx_m = _split_mods(ada_mods(cvec, ada_w[1], ada_b[1]), b, d)
    return _odd_layer_last(x, ctx, lat_m, ctx_m, norm1_g[1], norm2_g[1], od_w_in[0], od_hy_short_w[0],
                           od_hy_short_b[0], od_hy_w1[0], od_hy_b1[0], od_hy_freq1[0], od_hy_w2[0], od_hy_b2[0],
                           od_hy_freq2[0], od_hy_w3[0], od_hy_skip[0], od_dn_conv_w[0], od_dn_alog_f[0],
                           od_dn_alog_b[0], od_dn_dtb_f[0], od_dn_dtb_b[0], od_dn_norm_g[0], od_w_out[0],
                           mlp_w1[1], mlp_w2[1], final_g)
```

```python
import functools
import math

import jax
import jax.numpy as jnp
import numpy as np
from jax import lax
from jax.experimental import pallas as pl
from jax.experimental.pallas import tpu as pltpu

F32 = jnp.float32
BF16 = jnp.bfloat16

EPS = 1e-6
N_MOD = 6
GRID_W = 64
ROPE_BASE = 10000.0
CONV_W = 31
DIFF_HEADS = 4
DIFF_DQK = 64
DIFF_DV = 128
HY_ORDER = 2
HY_EMB = 33
HY_FFN = 64
HY_MAX_DECAY = math.log(1e-2) / 0.3
HY_MIN_DECAY = math.log(1e-2) / 1.5
DN_HEADS = 4
DN_DK = 128
DN_CONV_W = 5
DN_CHUNK = 128

LANE = 128
SUBLANE = 8
VMEM_LIMIT = 56 * 1024 * 1024


def _cparams(*sem):
    return pltpu.CompilerParams(dimension_semantics=sem, vmem_limit_bytes=VMEM_LIMIT)


def _tile(n, pref, mult=SUBLANE):
    if n <= pref:
        return n
    t = (pref // mult) * mult
    while t > mult and n % t:
        t -= mult
    assert n % t == 0, (n, pref, mult)
    return t


def _const_spec(shape):
    nd = len(shape)
    return pl.BlockSpec(shape, lambda *_: (0,) * nd, pipeline_mode=pl.Buffered(1))


def _silu(x):
    return x * jax.nn.sigmoid(x)


def _norm_mod(x, g, sh, sc):
    y = x * lax.rsqrt(jnp.mean(x * x, axis=-1, keepdims=True) + EPS)
    return (y * g) * (1.0 + sc) + sh


def _mods_kernel(c_ref, w_ref, b_ref, o_ref):
    cond = _silu(c_ref[...])
    o_ref[...] = jnp.dot(cond, w_ref[...], preferred_element_type=F32,
                         precision=lax.Precision.HIGHEST) + b_ref[...]


def ada_mods(cvec, w, b):
    r, d = cvec.shape
    n = w.shape[1]
    tn = _tile(n, 512, LANE)
    return pl.pallas_call(
        _mods_kernel,
        grid=(n // tn,),
        in_specs=[pl.BlockSpec((r, d), lambda j: (0, 0)),
                  pl.BlockSpec((d, tn), lambda j: (0, j)),
                  pl.BlockSpec((1, tn), lambda j: (0, j))],
        out_specs=pl.BlockSpec((r, tn), lambda j: (0, j)),
        out_shape=jax.ShapeDtypeStruct((r, n), F32),
        compiler_params=_cparams("arbitrary"),
        name="ada_mods",
    )(cvec, w, b.reshape(1, n))


def _swap16(y):
    n = y.shape[-1]
    lane = lax.broadcasted_iota(jnp.int32, y.shape, y.ndim - 1)
    fwd = pltpu.roll(y, n - 16, y.ndim - 1)
    bwd = pltpu.roll(y, 16, y.ndim - 1)
    return jnp.where((lane % 32) < 16, fwd, bwd)


PROJ_CHUNK = 512


def _proj_kernel(x_ref, g_ref, sh_ref, sc_ref, w_ref, *rest, splits, rope_cols):
    n_out = len(splits)
    if rope_cols:
        cq_ref, sq_ref, ck_ref, sk_ref = rest[:4]
        rest = rest[4:]
    o_refs = rest[:n_out]
    h = _norm_mod(x_ref[0], g_ref[...], sh_ref[0], sc_ref[0]).astype(BF16)
    for o_ref, (s0, s1) in zip(o_refs, splits):
        tc = PROJ_CHUNK if (s1 - s0) % PROJ_CHUNK == 0 and s0 % PROJ_CHUNK == 0 else LANE
        for c0 in range(s0, s1, tc):
            y = jnp.dot(h, w_ref[:, c0:c0 + tc], preferred_element_type=F32)
            if rope_cols and rope_cols[0] <= c0 < rope_cols[2]:
                is_q = c0 < rope_cols[1]
                cos = (cq_ref if is_q else ck_ref)[...]
                sin = (sq_ref if is_q else sk_ref)[...]
                reps = tc // LANE
                cos = jnp.concatenate([cos] * reps, axis=1)
                sin = jnp.concatenate([sin] * reps, axis=1)
                y = y * cos + _swap16(y) * sin
            o_ref[0, :, c0 - s0:c0 - s0 + tc] = y.astype(o_ref.dtype)


def norm_mod_proj(x, g, sh, sc, w, splits, dtypes, rope=None, rope_cols=None):
    b, l, d = x.shape
    n = w.shape[1]
    tm = _tile(l, 512)
    assert all(s % LANE == 0 for sp in splits for s in sp)
    if rope_cols:
        assert all(c % PROJ_CHUNK == 0 for c in rope_cols)
    row = lambda bi, i: (bi, i, 0)
    in_specs = [pl.BlockSpec((1, tm, d), row),
                _const_spec((1, d)),
                pl.BlockSpec((1, 1, d), lambda bi, i: (bi, 0, 0)),
                pl.BlockSpec((1, 1, d), lambda bi, i: (bi, 0, 0)),
                _const_spec((d, n))]
    args = [x, g.reshape(1, d), sh, sc, w]
    if rope_cols:
        in_specs += [pl.BlockSpec((tm, LANE), lambda bi, i: (i, 0))] * 4
        args += list(rope)
    return pl.pallas_call(
        functools.partial(_proj_kernel, splits=tuple(splits), rope_cols=rope_cols),
        grid=(b, l // tm),
        in_specs=in_specs,
        out_specs=[pl.BlockSpec((1, tm, s1 - s0), row) for s0, s1 in splits],
        out_shape=[jax.ShapeDtypeStruct((b, l, s1 - s0), dt) for (s0, s1), dt in zip(splits, dtypes)],
        compiler_params=_cparams("parallel", "parallel"),
        name="norm_mod_proj",
    )(*args)


CONV_HALO = 16


def _conformer_kernel(ac_ref, gc_ref, ap_ref, gp_ref, an_ref, gn_ref, w_ref, b_ref, lg_ref, lb_ref, o_ref,
                      u_ref, sh_ref, y_ref, *, tl, ch):
    i = pl.program_id(1)
    last = pl.num_programs(1) - 1
    left = (CONV_W - 1) // 2
    glu = lambda a, g: a * jax.nn.sigmoid(g)
    u_ref[CONV_HALO:CONV_HALO + tl, :] = glu(ac_ref[0], gc_ref[0])
    prev = glu(ap_ref[0], gp_ref[0])
    nxt = glu(an_ref[0], gn_ref[0])
    u_ref[0:CONV_HALO, :] = jnp.where(i > 0, prev, 0.0)
    u_ref[CONV_HALO + tl:CONV_HALO + tl + CONV_HALO, :] = jnp.where(i < last, nxt, 0.0)
    n = sh_ref.shape[1]
    for p in range(SUBLANE):
        sh_ref[p] = u_ref[p:p + n, :]
    rows = min(tl, 128)
    for r0 in range(0, tl, rows):
        for c0 in range(0, ch, LANE):
            acc = jnp.zeros((rows, LANE), F32) + b_ref[:, c0:c0 + LANE]
            for k in range(CONV_W):
                s = CONV_HALO - left + k + r0
                p, base = s % SUBLANE, s - s % SUBLANE
                acc = acc + w_ref[k:k + 1, c0:c0 + LANE] * sh_ref[p, base:base + rows, c0:c0 + LANE]
            y_ref[r0:r0 + rows, c0:c0 + LANE] = acc
    y = y_ref[...]
    mu = jnp.mean(y, axis=-1, keepdims=True)
    yc = y - mu
    var = jnp.mean(yc * yc, axis=-1, keepdims=True)
    z = yc * lax.rsqrt(var + EPS) * lg_ref[...] + lb_ref[...]
    o_ref[0] = _silu(z).astype(o_ref.dtype)


def conformer_conv(p, conv_w, conv_b, ln_g, ln_b):
    b, l, _ = p.shape
    ch = conv_w.shape[1]
    tl = _tile(l, 256, CONV_HALO)
    nh = tl // CONV_HALO
    nblk = l // CONV_HALO
    cur = lambda col: pl.BlockSpec((1, tl, ch), lambda bi, i: (bi, i, col))
    prv = lambda col: pl.BlockSpec((1, CONV_HALO, ch), lambda bi, i: (bi, jnp.maximum(i * nh - 1, 0), col))
    nxt = lambda col: pl.BlockSpec((1, CONV_HALO, ch), lambda bi, i: (bi, jnp.minimum((i + 1) * nh, nblk - 1), col))
    return pl.pallas_call(
        functools.partial(_conformer_kernel, tl=tl, ch=ch),
        grid=(b, l // tl),
        in_specs=[cur(0), cur(1), prv(0), prv(1), nxt(0), nxt(1),
                  _const_spec((CONV_W, ch)), _const_spec((1, ch)), _const_spec((1, ch)), _const_spec((1, ch))],
        out_specs=pl.BlockSpec((1, tl, ch), lambda bi, i: (bi, i, 0)),
        out_shape=jax.ShapeDtypeStruct((b, l, ch), BF16),
        scratch_shapes=[pltpu.VMEM((tl + 2 * CONV_HALO, ch), F32),
                        pltpu.VMEM((SUBLANE, tl + 2 * CONV_HALO - SUBLANE, ch), F32),
                        pltpu.VMEM((tl, ch), F32)],
        compiler_params=_cparams("parallel", "parallel"),
        name="conformer_conv",
    )(p, p, p, p, p, p, conv_w, conv_b.reshape(1, ch), ln_g.reshape(1, ch), ln_b.reshape(1, ch))


ATTN_QBLK = 256


ATTN_ROWS = 128
ATTN_ONES = 16
ATTN_REDO = 120.0
ATTN_REF_ROW = (DIFF_DQK, 0)


def _round_bf16(x):
    return x.astype(BF16).astype(F32)


def _diff_attn_kernel(qt_ref, k_ref, vt_ref, lam_ref, g_ref, o_ref, qm_ref, m_ref, acc_ref, flag_ref,
                      *, lambda_init):
    kv = pl.program_id(3)
    tq = qm_ref.shape[2]
    qblk = min(ATTN_QBLK, tq)
    chains = [(mi, slice(q0, q0 + qblk)) for q0 in range(0, tq, qblk) for mi in range(2)]

    @pl.when(kv == 0)
    def _():
        qt = qt_ref[0].astype(F32)
        row = lax.broadcasted_iota(jnp.int32, qt.shape, 0)
        qm_ref[0] = jnp.where(row < DIFF_DQK, qt, 0.0)
        qm_ref[1] = jnp.where(row < DIFF_DQK, 0.0, qt)
        m_ref[...] = jnp.zeros(m_ref.shape, F32)
        acc_ref[...] = jnp.zeros(acc_ref.shape, F32)

    k = k_ref[0]
    lane = lax.broadcasted_iota(jnp.int32, k.shape, 1)
    kx = [jnp.where(lane == r, jnp.ones_like(k), k) for r in ATTN_REF_ROW]
    vtx = vt_ref[0, 0]

    def scores(mi, qs):
        return jnp.dot(kx[mi], qm_ref[mi, :, qs].astype(BF16), preferred_element_type=F32)

    def move_reference(mi, qs, m_new):
        m_ref[mi, :, qs] = m_new
        r = ATTN_REF_ROW[mi]
        qm_ref[mi, r:r + 1, qs] = -m_new

    flag_ref[0] = (kv == 0).astype(jnp.int32)

    @pl.when(kv > 0)
    def _():
        sc = [scores(mi, qs) for mi, qs in chains]
        mxs, pvs = [], []
        for s in sc:
            es, mx = [], None
            for r0 in range(0, s.shape[0], ATTN_ROWS):
                grp = s[r0:r0 + ATTN_ROWS]
                es.append(jnp.exp2(grp).astype(BF16))
                top = jnp.max(grp, axis=0, keepdims=True)
                mx = top if mx is None else jnp.maximum(mx, top)
            mxs.append(mx)
            pvs.append(jnp.dot(vtx, jnp.concatenate(es, axis=0), preferred_element_type=F32))
        redo = jnp.max(functools.reduce(jnp.maximum, mxs)) > ATTN_REDO
        flag_ref[0] = redo.astype(jnp.int32)

        @pl.when(jnp.logical_not(redo))
        def _():
            for (mi, qs), mx, pv in zip(chains, mxs, pvs):
                m_old = m_ref[mi, :, qs]
                m_new = _round_bf16(m_old + jnp.maximum(mx, 0.0))
                acc_ref[mi, :, qs] = (acc_ref[mi, :, qs] + pv) * jnp.exp2(m_old - m_new)
                move_reference(mi, qs, m_new)

    @pl.when(flag_ref[0] != 0)
    def _():
        first = kv == 0
        sc = [scores(mi, qs) for mi, qs in chains]
        for (mi, qs), s in zip(chains, sc):
            mx = jnp.max(s, axis=0, keepdims=True)
            m_old = m_ref[mi, :, qs]
            m_new = _round_bf16(m_old + jnp.where(first, mx, jnp.maximum(mx, 0.0)))
            delta = m_new - m_old
            e = jnp.exp2(s - delta)
            alpha = jnp.exp2(-jnp.maximum(delta, 0.0))
            acc_ref[mi, :, qs] = alpha * acc_ref[mi, :, qs] + jnp.dot(vtx, e.astype(BF16),
                                                                      preferred_element_type=F32)
            move_reference(mi, qs, m_new)

    @pl.when(kv == pl.num_programs(3) - 1)
    def _():
        lam = (jnp.exp(jnp.sum(lam_ref[0:1, :] * lam_ref[1:2, :], axis=-1, keepdims=True))
               - jnp.exp(jnp.sum(lam_ref[2:3, :] * lam_ref[3:4, :], axis=-1, keepdims=True)) + lambda_init)
        dv = DIFF_DV
        ot = (acc_ref[0, :dv, :] / acc_ref[0, dv:dv + 1, :]
              - lam * (acc_ref[1, :dv, :] / acc_ref[1, dv:dv + 1, :]))
        ot = ot * lax.rsqrt(jnp.mean(ot * ot, axis=0, keepdims=True) + EPS)
        o = ot.T * g_ref[...]
        o_ref[0] = (o * (1.0 - lambda_init)).astype(o_ref.dtype)


def diff_attention(qt, k, vt, lam_vecs, subln_g, lambda_init, k_blk):
    b, _, lq = qt.shape
    lk = k.shape[1]
    tq = _tile(lq, 1024, LANE)
    tk = _tile(lk, 1024, LANE)
    rows = vt.shape[2]
    return pl.pallas_call(
        functools.partial(_diff_attn_kernel, lambda_init=lambda_init),
        grid=(b, DIFF_HEADS, lq // tq, lk // tk),
        in_specs=[pl.BlockSpec((1, LANE, tq), lambda bi, h, i, j: (bi, h, i)),
                  pl.BlockSpec((1, tk, LANE), lambda bi, h, i, j: (bi, j, k_blk + h)),
                  pl.BlockSpec((1, 1, rows, tk), lambda bi, h, i, j: (bi, h, 0, j)),
                  _const_spec((4, DIFF_DQK)), _const_spec((1, DIFF_DV))],
        out_specs=pl.BlockSpec((1, tq, LANE), lambda bi, h, i, j: (bi, i, h)),
        out_shape=jax.ShapeDtypeStruct((b, lq, DIFF_HEADS * DIFF_DV), BF16),
        scratch_shapes=[pltpu.VMEM((2, LANE, tq), F32), pltpu.VMEM((2, 1, tq), F32),
                        pltpu.VMEM((2, rows, tq), F32), pltpu.SMEM((1,), jnp.int32)],
        compiler_params=_cparams("parallel", "parallel", "parallel", "arbitrary"),
        name="diff_attention",
    )(qt, k, vt, lam_vecs, subln_g.reshape(1, DIFF_DV))


def _out_mlp_kernel(x_ref, a_ref, b_ref, wo_ref, gt1_ref, g2_ref, sh2_ref, sc2_ref, gt2_ref, w1_ref, w2_ref,
                    fg_ref, o_ref, *, a_transposed, hidden_chunk, final_norm):
    half = wo_ref.shape[0] // 2
    if a_transposed:
        ya = lax.dot_general(a_ref[0], wo_ref[0:half, :], (((0,), (0,)), ((), ())), preferred_element_type=F32)
    else:
        ya = jnp.dot(a_ref[0], wo_ref[0:half, :], preferred_element_type=F32)
    y = ya + jnp.dot(b_ref[0], wo_ref[half:, :], preferred_element_type=F32)
    x1 = x_ref[0] + gt1_ref[0] * y
    h = _norm_mod(x1, g2_ref[...], sh2_ref[0], sc2_ref[0]).astype(BF16)
    acc = jnp.zeros_like(x1)
    for c0 in range(0, w1_ref.shape[1], hidden_chunk):
        t = jnp.maximum(jnp.dot(h, w1_ref[:, c0:c0 + hidden_chunk], preferred_element_type=F32), 0.0)
        acc = acc + jnp.dot((t * t).astype(BF16), w2_ref[c0:c0 + hidden_chunk, :], preferred_element_type=F32)
    x2 = x1 + gt2_ref[0] * acc
    if final_norm:
        x2 = x2 * lax.rsqrt(jnp.mean(x2 * x2, axis=-1, keepdims=True) + EPS) * fg_ref[...]
    o_ref[0] = x2


def out_proj_mlp(x, mix_a, mix_b, w_out, gt1, g2, sh2, sc2, gt2, w1, w2, final_g=None, a_transposed=False):
    b, l, d = x.shape
    c = mix_b.shape[2]
    hid = w1.shape[1]
    tm = _tile(l, 512, LANE)
    row = lambda bi, i: (bi, i, 0)
    vec = lambda bi, i: (bi, 0, 0)
    a_spec = (pl.BlockSpec((1, c, tm), lambda bi, i: (bi, 0, i)) if a_transposed
              else pl.BlockSpec((1, tm, c), row))
    fg = jnp.ones((1, d), F32) if final_g is None else final_g.reshape(1, d)
    return pl.pallas_call(
        functools.partial(_out_mlp_kernel, a_transposed=a_transposed, hidden_chunk=min(hid, 1024),
                          final_norm=final_g is not None),
        grid=(b, l // tm),
        in_specs=[pl.BlockSpec((1, tm, d), row), a_spec, pl.BlockSpec((1, tm, c), row),
                  _const_spec((2 * c, d)),
                  pl.BlockSpec((1, 1, d), vec), _const_spec((1, d)),
                  pl.BlockSpec((1, 1, d), vec), pl.BlockSpec((1, 1, d), vec), pl.BlockSpec((1, 1, d), vec),
                  _const_spec((d, hid)), _const_spec((hid, d)), _const_spec((1, d))],
        out_specs=pl.BlockSpec((1, tm, d), row),
        out_shape=jax.ShapeDtypeStruct((b, l, d), F32),
        compiler_params=_cparams("parallel", "parallel"),
        name="out_proj_mlp",
    )(x, mix_a, mix_b, w_out, gt1, g2.reshape(1, d), sh2, sc2, gt2, w1, w2, fg)


def _rope_tables(length, scale):
    f32 = np.float32
    rows = length // GRID_W
    row = np.repeat(np.arange(rows, dtype=f32), GRID_W)
    col = np.tile(np.arange(GRID_W, dtype=f32), rows)
    n_freq = DIFF_DQK // 4
    inv = (f32(ROPE_BASE) ** (-np.arange(n_freq, dtype=f32) / f32(n_freq))).astype(f32)
    ang_r = row[:, None] * inv
    ang_c = col[:, None] * inv
    cos = np.concatenate([np.cos(ang_r)] * 2 + [np.cos(ang_c)] * 2, axis=-1)
    sin = np.concatenate([-np.sin(ang_r), np.sin(ang_r), -np.sin(ang_c), np.sin(ang_c)], axis=-1)
    cos = np.concatenate([cos, cos], axis=-1) * f32(scale)
    sin = np.concatenate([sin, sin], axis=-1) * f32(scale)
    return cos.astype(f32), sin.astype(f32)


def _flat_tables(length, scale):
    return np.full((length, LANE), scale, np.float32), np.zeros((length, LANE), np.float32)


def _proj_t_kernel(x_ref, g_ref, sh_ref, sc_ref, wt_ref, o_ref, *, rc):
    h = _norm_mod(x_ref[0], g_ref[...], sh_ref[0], sc_ref[0]).astype(BF16)
    for r0 in range(0, wt_ref.shape[0], rc):
        o_ref[0, r0:r0 + rc, :] = lax.dot_general(wt_ref[r0:r0 + rc, :], h, (((1,), (1,)), ((), ())),
                                                  preferred_element_type=F32)


def norm_mod_proj_t(x, g, sh, sc, wt):
    b, l, d = x.shape
    c = wt.shape[0]
    tl = _tile(l, 512, LANE)
    return pl.pallas_call(
        functools.partial(_proj_t_kernel, rc=_tile(c, 512)),
        grid=(b, l // tl),
        in_specs=[pl.BlockSpec((1, tl, d), lambda bi, i: (bi, i, 0)),
                  _const_spec((1, d)),
                  pl.BlockSpec((1, 1, d), lambda bi, i: (bi, 0, 0)),
                  pl.BlockSpec((1, 1, d), lambda bi, i: (bi, 0, 0)),
                  _const_spec((c, d))],
        out_specs=pl.BlockSpec((1, c, tl), lambda bi, i: (bi, 0, i)),
        out_shape=jax.ShapeDtypeStruct((b, c, l), F32),
        compiler_params=_cparams("parallel", "parallel"),
        name="norm_mod_proj_t",
    )(x, g.reshape(1, d), sh, sc, wt)


def _dot_hi(a, b):
    return jnp.dot(a, b, preferred_element_type=F32, precision=lax.Precision.HIGHEST)


def _hy_hidden_kernel(z_ref, w1_ref, b1_ref, f1_ref, w2_ref, b2_ref, f2_ref, o_ref):
    hid = jnp.sin(f1_ref[...] * (_dot_hi(w1_ref[...], z_ref[...]) + b1_ref[...]))
    o_ref[...] = jnp.sin(f2_ref[...] * (_dot_hi(w2_ref[...], hid) + b2_ref[...]))


def hyena_hidden(zt, w1t, b1, f1, w2t, b2, f2):
    e, n = zt.shape
    f = w1t.shape[0]
    tn = _tile(n, 2048, LANE)
    col = lambda v: v.reshape(f, 1)
    return pl.pallas_call(
        _hy_hidden_kernel,
        grid=(n // tn,),
        in_specs=[pl.BlockSpec((e, tn), lambda j: (0, j)),
                  _const_spec((f, e)), _const_spec((f, 1)), _const_spec((f, 1)),
                  _const_spec((f, f)), _const_spec((f, 1)), _const_spec((f, 1))],
        out_specs=pl.BlockSpec((f, tn), lambda j: (0, j)),
        out_shape=jax.ShapeDtypeStruct((f, n), F32),
        compiler_params=_cparams("parallel"),
        name="hyena_hidden",
    )(zt, w1t, col(b1), col(f1), w2t, col(b2), col(f2))


def _hy_filter_kernel(hid_ref, t_ref, wf_ref, wb_ref, dl_ref, o_ref, *, length, tn):
    n2 = hid_ref.shape[1]
    delta = dl_ref[...]

    def piece(c0):
        hid = hid_ref[:, c0:c0 + tn]
        dec = jnp.exp(-t_ref[:, c0:c0 + tn] * delta)
        return _dot_hi(wf_ref[...], hid) * dec, _dot_hi(wb_ref[...], hid) * dec

    norm = jnp.zeros((wf_ref.shape[0], 1), F32)
    for c0 in range(length, n2, tn):
        ff, fb = piece(c0)
        norm = norm + jnp.sum(jnp.abs(ff) + jnp.abs(fb), axis=-1, keepdims=True)
    inv = 1.0 / norm
    for c0 in range(0, n2, tn):
        ff, fb = piece(c0)
        if c0 >= length:
            o_ref[0, :, c0:c0 + tn] = ff * inv
        else:
            col = lax.broadcasted_iota(jnp.int32, fb.shape, 1) + c0
            o_ref[0, :, c0:c0 + tn] = jnp.where(col == 0, 0.0, fb * inv)


def hyena_filters(hid, t_row, w3t, deltas, length):
    f, n2 = hid.shape
    ch = deltas.shape[0]
    cb = LANE
    nblk = ch // cb
    tn = _tile(length, 2048, LANE)
    return pl.pallas_call(
        functools.partial(_hy_filter_kernel, length=length, tn=tn),
        grid=(HY_ORDER, nblk),
        in_specs=[_const_spec((f, n2)), _const_spec((1, n2)),
                  pl.BlockSpec((cb, f), lambda o, c: (o * 2 * nblk + c, 0)),
                  pl.BlockSpec((cb, f), lambda o, c: (o * 2 * nblk + nblk + c, 0)),
                  pl.BlockSpec((cb, 1), lambda o, c: (c, 0))],
        out_specs=pl.BlockSpec((1, cb, n2), lambda o, c: (o, c, 0)),
        out_shape=jax.ShapeDtypeStruct((HY_ORDER, ch, n2), F32),
        compiler_params=_cparams("parallel", "parallel"),
        name="hyena_filters",
    )(hid, t_row, w3t, w3t, deltas.reshape(ch, 1))


HY_T = 256
HY_CB = SUBLANE


def _hyena_kernel(v_ref, x1_ref, x2_ref, g_ref, sw_ref, sb_ref, sk_ref_s, o_ref, sk_ref, u_ref, acc_ref,
                  *, length, nch):
    nb = v_ref.shape[0]
    nj = length // HY_T
    cblk = pl.program_id(0)
    lane = lax.broadcasted_iota(jnp.int32, (nb, length), 1)

    def short(ref, cc, part):
        x = ref[:, cc, :]
        ch = part * nch + cblk * HY_CB + cc
        prev = jnp.where(lane == 0, 0.0, pltpu.roll(x, 1, 1))
        nxt = jnp.where(lane == length - 1, 0.0, pltpu.roll(x, length - 1, 1))
        return sw_ref[0, ch] * prev + sw_ref[1, ch] * x + sw_ref[2, ch] * nxt + sb_ref[ch]

    def long_conv(u, order, cc):
        g = jnp.broadcast_to(g_ref[order, pl.ds(cc, 1), :], (SUBLANE, 2 * length))
        half = HY_T // 2
        for a2 in range(half // (2 * SUBLANE)):
            lo = pltpu.roll(g, 2 * a2 * SUBLANE, 1, stride=1, stride_axis=0)
            hi = pltpu.roll(g, (2 * a2 + 1) * SUBLANE, 1, stride=1, stride_axis=0)
            pair = jnp.concatenate([lo, hi], axis=0).astype(BF16)
            r0 = 2 * a2 * SUBLANE
            sk_ref[r0:r0 + 2 * SUBLANE, :] = pair
            sk_ref[half + r0:half + r0 + 2 * SUBLANE, half:] = pair[:, :2 * length - half]
        for j in range(nj):
            u_ref[j * nb:(j + 1) * nb, :] = u[:, j * HY_T:(j + 1) * HY_T]
        acc_ref[...] = jnp.zeros(acc_ref.shape, F32)
        for d in range(-(nj - 1), nj):
            m = (nj - abs(d)) * nb
            lo_in, lo_out = max(0, -d) * nb, max(0, d) * nb
            tile = sk_ref[:, length + d * HY_T:length + (d + 1) * HY_T]
            acc_ref[lo_out:lo_out + m, :] += jnp.dot(u_ref[lo_in:lo_in + m, :].astype(BF16), tile,
                                                     preferred_element_type=F32)
        return jnp.concatenate([acc_ref[i * nb:(i + 1) * nb, :] for i in range(nj)], axis=1)

    def channel(cc, carry):
        v = short(v_ref, cc, 0)
        x1 = short(x1_ref, cc, 1)
        x2 = short(x2_ref, cc, 2)
        ch = cblk * HY_CB + cc
        z = x1 * (long_conv(v, 0, cc) + sk_ref_s[0, ch] * v)
        z = x2 * (long_conv(z, 1, cc) + sk_ref_s[1, ch] * z)
        o_ref[:, cc, :] = z
        return carry

    lax.fori_loop(0, HY_CB, channel, 0)


def hyena_mix(pt, filt, short_w, short_b, skip):
    b, c3, l = pt.shape
    nch = c3 // 3
    assert l % HY_T == 0 and nch % HY_CB == 0 and b == SUBLANE
    nblk = nch // HY_CB
    smem = pl.BlockSpec(memory_space=pltpu.SMEM)
    part = lambda k: pl.BlockSpec((b, HY_CB, l), lambda c: (0, k * nblk + c, 0))
    return pl.pallas_call(
        functools.partial(_hyena_kernel, length=l, nch=nch),
        grid=(nblk,),
        in_specs=[part(0), part(1), part(2),
                  pl.BlockSpec((HY_ORDER, HY_CB, 2 * l), lambda c: (0, c, 0)),
                  smem, smem, smem],
        out_specs=pl.BlockSpec((b, HY_CB, l), lambda c: (0, c, 0)),
        out_shape=jax.ShapeDtypeStruct((b, nch, l), F32),
        scratch_shapes=[pltpu.VMEM((HY_T, 2 * l), BF16), pltpu.VMEM((l // HY_T * b, HY_T), F32),
                        pltpu.VMEM((l // HY_T * b, HY_T), F32)],
        compiler_params=_cparams("parallel"),
        name="hyena_mix",
    )(pt, pt, pt, filt, short_w, short_b, skip)


DN_HALO = SUBLANE
DN_BETA_F, DN_BETA_B, DN_G_F, DN_G_B = 0, DN_HEADS, 2 * DN_HEADS, 3 * DN_HEADS


def _dn_feat_kernel(c_ref, p_ref, n_ref, s_ref, w_ref, al_ref, dt_ref, q_ref, k_ref, v_ref, f_ref, u_ref, *, tl):
    i = pl.program_id(1)
    last = pl.num_programs(1) - 1
    left = (DN_CONV_W - 1) // 2
    nc = c_ref.shape[2]
    u_ref[DN_HALO:DN_HALO + tl, :] = c_ref[0]
    u_ref[0:DN_HALO, :] = jnp.where(i > 0, p_ref[0], 0.0)
    u_ref[DN_HALO + tl:DN_HALO + tl + DN_HALO, :] = jnp.where(i < last, n_ref[0], 0.0)
    rows = min(tl, 128)
    outs = (q_ref, k_ref, v_ref)
    per = nc // len(outs)
    for r0 in range(0, tl, rows):
        for c0 in range(0, nc, LANE):
            acc = jnp.zeros((rows, LANE), F32)
            for k in range(DN_CONV_W):
                s = DN_HALO - left + k + r0
                acc = acc + w_ref[k:k + 1, c0:c0 + LANE] * u_ref[s:s + rows, c0:c0 + LANE]
            y = _silu(acc)
            which, off = divmod(c0, per)
            if which < 2:
                y = y * lax.rsqrt(jnp.sum(y * y, axis=-1, keepdims=True) + 1e-6)
            outs[which][0, r0:r0 + rows, off:off + LANE] = y
    s = s_ref[0]
    lane = lax.broadcasted_iota(jnp.int32, s.shape, 1)
    gate = -jnp.exp(al_ref[...]) * jax.nn.softplus(s + dt_ref[...])
    f_ref[0] = jnp.where(lane < DN_G_F, jax.nn.sigmoid(s), gate)


def dn_features(qkv, small, conv_w, alog_row, dtb_row):
    b, l, nc = qkv.shape
    tl = _tile(l, 256, DN_HALO)
    nh = tl // DN_HALO
    nblk = l // DN_HALO
    w = nc // 3
    row = lambda bi, i: (bi, i, 0)
    return pl.pallas_call(
        functools.partial(_dn_feat_kernel, tl=tl),
        grid=(b, l // tl),
        in_specs=[pl.BlockSpec((1, tl, nc), row),
                  pl.BlockSpec((1, DN_HALO, nc), lambda bi, i: (bi, jnp.maximum(i * nh - 1, 0), 0)),
                  pl.BlockSpec((1, DN_HALO, nc), lambda bi, i: (bi, jnp.minimum((i + 1) * nh, nblk - 1), 0)),
                  pl.BlockSpec((1, tl, LANE), row),
                  _const_spec((DN_CONV_W, nc)), _const_spec((1, LANE)), _const_spec((1, LANE))],
        out_specs=[pl.BlockSpec((1, tl, w), row)] * 3 + [pl.BlockSpec((1, tl, LANE), row)],
        out_shape=[jax.ShapeDtypeStruct((b, l, w), F32)] * 3 + [jax.ShapeDtypeStruct((b, l, LANE), F32)],
        scratch_shapes=[pltpu.VMEM((tl + 2 * DN_HALO, nc), F32)],
        compiler_params=_cparams("parallel", "parallel"),
        name="dn_features",
    )(qkv, qkv, qkv, small, conv_w, alog_row, dtb_row)


def _bdot(a, b):
    return jnp.dot(a.astype(BF16), b.astype(BF16), preferred_element_type=F32)


def _bdot_nt(a, b):
    return lax.dot_general(a.astype(BF16), b.astype(BF16), (((1,), (1,)), ((), ())), preferred_element_type=F32)


def _bdot_tn(a, b):
    return lax.dot_general(a.astype(BF16), b.astype(BF16), (((0,), (0,)), ((), ())), preferred_element_type=F32)


def _cumsum_rows(x, reverse):
    n = x.shape[0]
    row = lax.broadcasted_iota(jnp.int32, x.shape, 0)
    s = 1
    while s < n:
        if reverse:
            x = x + jnp.where(row < n - s, pltpu.roll(x, n - s, 0), 0.0)
        else:
            x = x + jnp.where(row >= s, pltpu.roll(x, s, 0), 0.0)
        s *= 2
    return x


def _dn_chunks(chains):
    c = chains[0]["k"].shape[0]
    ri = lax.broadcasted_iota(jnp.int32, (c, c), 0)
    ci = lax.broadcasted_iota(jnp.int32, (c, c), 1)
    work = []
    for ch in chains:
        rev = ch["reverse"]
        incl = (ri <= ci) if rev else (ri >= ci)
        strict = (ri < ci) if rev else (ri > ci)
        decay = jnp.where(incl, jnp.exp(jnp.where(incl, ch["gcol"] - ch["grow"], 0.0)), 0.0)
        kb = ch["k"] * ch["beta"]
        qs = ch["q"] * (ch["k"].shape[1] ** -0.5)
        work.append(dict(ch, strict=strict, decay=decay, kb=kb, qs=qs))
    grams = [_bdot_nt(jnp.concatenate([w["kb"], w["qs"]], axis=0), w["k"]) for w in work]
    lowers = [jnp.where(w["strict"], g[:c] * w["decay"], 0.0) for w, g in zip(work, grams)]
    same2 = (ri // 2) == (ci // 2)
    ymats = [-jnp.where(same2, lo, 0.0) for lo in lowers]
    m = 2
    while m < c:
        pair = ((ri // (2 * m)) == (ci // (2 * m))) & ((ri // m) != (ci // m))
        offs = [jnp.where(pair, lo, 0.0) for lo in lowers]
        zs = [off + _bdot(ym, off) for ym, off in zip(ymats, offs)]
        ymats = [ym - z - _bdot(z, ym) for ym, z in zip(ymats, zs)]
        m *= 2
    outs = []
    uws, e_gs = [], []
    for w, ym in zip(work, ymats):
        e_g = jnp.exp(w["gcol"])
        rhs = jnp.concatenate([w["v"] * w["beta"], w["kb"] * e_g], axis=1)
        uws.append(rhs + _bdot(ym, rhs))
        e_gs.append(e_g)
    dv = work[0]["v"].shape[1]
    wss = [_bdot(jnp.concatenate([uw[:, dv:], w["qs"] * e_g], axis=0), w["state"])
           for w, uw, e_g in zip(work, uws, e_gs)]
    v_news = [uw[:, :dv] - ws[:c] for uw, ws in zip(uws, wss)]
    for w, g, ws, v_new in zip(work, grams, wss, v_news):
        gcol = w["gcol"]
        g_last = gcol[0:1, :] if w["reverse"] else gcol[c - 1:c, :]
        k_dec = w["k"] * jnp.exp(g_last - gcol)
        o = ws[c:] + _bdot(g[c:] * w["decay"], v_new)
        new_state = w["state"] * jnp.exp(g_last) + _bdot_tn(k_dec, v_new)
        outs.append((o, new_state))
    return outs


def _dn_scan_kernel(qf_ref, kf_ref, vf_ref, ff_ref, qb_ref, kb_ref, vb_ref, fb_ref, s0f_ref, s0b_ref,
                    of_ref, ob_ref, sf_ref, sb_ref, st_ref):
    n = pl.program_id(1)

    @pl.when(n == 0)
    def _():
        st_ref[0] = s0f_ref[0]
        st_ref[1] = s0b_ref[0]

    dirs = ((qf_ref, kf_ref, vf_ref, ff_ref, of_ref, False), (qb_ref, kb_ref, vb_ref, fb_ref, ob_ref, True))
    chains, dests = [], []
    for di, (q_ref, k_ref, v_ref, f_ref, o_ref, reverse) in enumerate(dirs):
        feats = f_ref[0]
        gc = _cumsum_rows(feats, reverse)
        gct = gc.T
        for h in range(DN_HEADS):
            cols = slice(h * DN_DK, (h + 1) * DN_DK)
            bl = (DN_BETA_B if reverse else DN_BETA_F) + h
            gl = (DN_G_B if reverse else DN_G_F) + h
            chains.append(dict(q=q_ref[0, :, cols], k=k_ref[0, :, cols], v=v_ref[0, :, cols],
                               beta=feats[:, bl:bl + 1], gcol=gc[:, gl:gl + 1], grow=gct[gl:gl + 1, :],
                               state=st_ref[di, h], reverse=reverse))
            dests.append((o_ref, cols, di, h))
    for (o_ref, cols, di, h), (o, s_new) in zip(dests, _dn_chunks(chains)):
        o_ref[0, :, cols] = o
        st_ref[di, h] = s_new

    @pl.when(n == pl.num_programs(1) - 1)
    def _():
        sf_ref[0] = st_ref[0]
        sb_ref[0] = st_ref[1]


def dn_scan(q, k, v, feats, s0f, s0b):
    b, l, w = q.shape
    nchunk = l // DN_CHUNK
    fwd = lambda bi, n: (bi, n, 0)
    bwd = lambda bi, n: (bi, nchunk - 1 - n, 0)
    st = lambda bi, n: (bi, 0, 0, 0)
    blk = lambda im, width: pl.BlockSpec((1, DN_CHUNK, width), im)
    st_spec = pl.BlockSpec((1, DN_HEADS, DN_DK, DN_DK), st)
    st_shape = jax.ShapeDtypeStruct((b, DN_HEADS, DN_DK, DN_DK), F32)
    return pl.pallas_call(
        _dn_scan_kernel,
        grid=(b, nchunk),
        in_specs=[blk(fwd, w), blk(fwd, w), blk(fwd, w), blk(fwd, LANE),
                  blk(bwd, w), blk(bwd, w), blk(bwd, w), blk(bwd, LANE), st_spec, st_spec],
        out_specs=[blk(fwd, w), blk(bwd, w), st_spec, st_spec],
        out_shape=[jax.ShapeDtypeStruct((b, l, w), F32)] * 2 + [st_shape, st_shape],
        scratch_shapes=[pltpu.VMEM((2, DN_HEADS, DN_DK, DN_DK), F32)],
        compiler_params=_cparams("parallel", "arbitrary"),
        name="dn_scan",
    )(q, k, v, feats, q, k, v, feats, s0f, s0b)


def _dn_out_kernel(of_ref, ob_ref, gate_ref, g_ref, o_ref):
    o = of_ref[0] + ob_ref[0]
    for c0 in range(0, o.shape[1], DN_DK):
        t = o[:, c0:c0 + DN_DK]
        t = t * lax.rsqrt(jnp.mean(t * t, axis=-1, keepdims=True) + EPS) * g_ref[...]
        o_ref[0, :, c0:c0 + DN_DK] = (t * _silu(gate_ref[0, :, c0:c0 + DN_DK])).astype(o_ref.dtype)


def dn_output(o_f, o_b, gate, norm_g):
    b, l, w = o_f.shape
    tl = _tile(l, 512)
    row = lambda bi, i: (bi, i, 0)
    return pl.pallas_call(
        _dn_out_kernel,
        grid=(b, l // tl),
        in_specs=[pl.BlockSpec((1, tl, w), row)] * 3 + [_const_spec((1, DN_DK))],
        out_specs=pl.BlockSpec((1, tl, w), row),
        out_shape=jax.ShapeDtypeStruct((b, l, w), BF16),
        compiler_params=_cparams("parallel", "parallel"),
        name="dn_output",
    )(o_f, o_b, gate, norm_g.reshape(1, DN_DK))


def _split_mods(mods, b, d):
    lat = [mods[:b, k * d:(k + 1) * d].reshape(b, 1, d) for k in range(N_MOD)]
    ctx = [jnp.broadcast_to(mods[b, k * d:(k + 1) * d].reshape(1, 1, d), (b, 1, d)) for k in range(N_MOD)]
    return lat, ctx


def _even_layer(x, ctx, lat_m, ctx_m, n1g, n2g, w_in, conv_w, conv_b, ln_g, ln_b, lam_vecs, subln_g, w_out,
                w1, w2, lambda_init):
    b, l, d = x.shape
    lc = ctx.shape[1]
    ch = conv_w.shape[1]
    qk = DIFF_HEADS * 2 * DIFF_DQK
    q0, k0, v0, n_in = 2 * ch, 2 * ch + qk, 2 * ch + 2 * qk, 2 * ch + 2 * qk + DIFF_HEADS * DIFF_DV
    scale = DIFF_DQK ** -0.5 * math.log2(math.e)
    splits = [(0, q0), (q0, n_in)]
    w_in = w_in.astype(BF16)
    rope_lat = _rope_tables(l, scale) + _rope_tables(l, 1.0)
    rope_ctx = _flat_tables(lc, scale) + _flat_tables(lc, 1.0)
    pc_lat, qkv_lat = norm_mod_proj(x, n1g, lat_m[0], lat_m[1], w_in, splits, (F32, BF16), rope_lat, (q0, k0, v0))
    pc_ctx, qkv_ctx = norm_mod_proj(ctx, n1g, ctx_m[0], ctx_m[1], w_in, splits, (F32, BF16), rope_ctx, (q0, k0, v0))
    conv_lat = conformer_conv(pc_lat, conv_w, conv_b, ln_g, ln_b)
    conv_ctx = conformer_conv(pc_ctx, conv_w, conv_b, ln_g, ln_b)
    k_all = jnp.concatenate([qkv_lat[:, :, qk:2 * qk], qkv_ctx[:, :, qk:2 * qk]], axis=1)
    def heads_t(v):
        vt = jnp.swapaxes(v, 1, 2).reshape(b, DIFF_HEADS, DIFF_DV, v.shape[1])
        return jnp.concatenate([vt, jnp.ones((b, DIFF_HEADS, ATTN_ONES, v.shape[1]), vt.dtype)], axis=2)

    vt_ctx = heads_t(qkv_ctx[:, :, 2 * qk:])
    vt_all = jnp.concatenate([heads_t(qkv_lat[:, :, 2 * qk:]), vt_ctx], axis=3)
    qt_lat = jnp.swapaxes(qkv_lat[:, :, :qk], 1, 2)
    qt_ctx = jnp.swapaxes(qkv_ctx[:, :, :qk], 1, 2)
    o_lat = diff_attention(qt_lat, k_all, vt_all, lam_vecs, subln_g, lambda_init, 0)
    o_ctx = diff_attention(qt_ctx, qkv_ctx, vt_ctx, lam_vecs, subln_g, lambda_init, qk // LANE)
    w_out, w1, w2 = w_out.astype(BF16), w1.astype(BF16), w2.astype(BF16)
    x = out_proj_mlp(x, conv_lat, o_lat, w_out, lat_m[2], n2g, lat_m[3], lat_m[4], lat_m[5], w1, w2)
    ctx = out_proj_mlp(ctx, conv_ctx, o_ctx, w_out, ctx_m[2], n2g, ctx_m[3], ctx_m[4], ctx_m[5], w1, w2)
    return x, ctx


def _hyena_position_tables(length):
    f32 = np.float32
    t = np.linspace(0.0, 1.0, length, dtype=f32)[:, None]
    bands = (HY_EMB - 1) // 2
    omega = (f32(2.0 * math.pi) * np.arange(length, dtype=f32)[:, None] / f32(length)).astype(f32)
    ang = omega * np.linspace(1e-4, bands - 1, bands, dtype=f32)
    z = np.concatenate([t, np.cos(ang), -np.sin(ang)], axis=-1).astype(f32)
    pos = np.minimum(np.abs(np.arange(2 * length) - length), length - 1)
    pad = -HY_EMB % SUBLANE
    zt = np.pad(z[pos].T, ((0, pad), (0, 0)))
    return np.ascontiguousarray(zt), np.ascontiguousarray(t[pos].T)


def _odd_layer_last(x, ctx, lat_m, ctx_m, n1g, n2g, w_in, hy_short_w, hy_short_b, hy_w1, hy_b1, hy_f1, hy_w2, hy_b2,
                    hy_f2, hy_w3, hy_skip, dn_conv_w, alog_f, alog_b, dtb_f, dtb_b, dn_norm_g, w_out, w1, w2,
                    final_g):
    b, l, d = x.shape
    hy_in = hy_short_w.shape[1]
    hy_ch = hy_in // (HY_ORDER + 1)
    dn_w = DN_HEADS * DN_DK
    n_rest = w_in.shape[1] - hy_in
    n_pad = -n_rest % LANE
    w_rest = jnp.pad(w_in[:, hy_in:], ((0, 0), (0, n_pad))).astype(BF16)
    splits = [(0, dn_w), (dn_w, 4 * dn_w), (4 * dn_w, n_rest + n_pad)]
    f32x3 = (F32, F32, F32)
    gate_lat, qkv_lat, small_lat = norm_mod_proj(x, n1g, lat_m[0], lat_m[1], w_rest, splits, f32x3)
    _, qkv_ctx, small_ctx = norm_mod_proj(ctx, n1g, ctx_m[0], ctx_m[1], w_rest, splits, f32x3)

    pt = norm_mod_proj_t(x, n1g, lat_m[0], lat_m[1], w_in[:, :hy_in].T.astype(BF16))
    zt, t_row = _hyena_position_tables(l)
    w1t = jnp.pad(hy_w1.T, ((0, 0), (0, zt.shape[0] - hy_w1.shape[0])))
    hid = hyena_hidden(zt, w1t, hy_b1, hy_f1, hy_w2.T, hy_b2, hy_f2)
    deltas = jnp.abs(jnp.linspace(HY_MIN_DECAY, HY_MAX_DECAY, hy_ch, dtype=F32))
    filt = hyena_filters(hid, t_row, hy_w3.T, deltas, l)
    hy = hyena_mix(pt, filt, hy_short_w, hy_short_b, hy_skip)

    lanes = jnp.zeros((LANE,), F32)
    alog_row = lanes.at[DN_G_F:DN_G_F + DN_HEADS].set(alog_f).at[DN_G_B:DN_G_B + DN_HEADS].set(alog_b).reshape(1, LANE)
    dtb_row = lanes.at[DN_G_F:DN_G_F + DN_HEADS].set(dtb_f).at[DN_G_B:DN_G_B + DN_HEADS].set(dtb_b).reshape(1, LANE)
    q_c, k_c, v_c, f_c = dn_features(qkv_ctx, small_ctx, dn_conv_w, alog_row, dtb_row)
    q_l, k_l, v_l, f_l = dn_features(qkv_lat, small_lat, dn_conv_w, alog_row, dtb_row)
    s0 = jnp.zeros((b, DN_HEADS, DN_DK, DN_DK), F32)
    _, _, s_cf, s_cb = dn_scan(q_c, k_c, v_c, f_c, s0, s0)
    o_f, o_b, _, _ = dn_scan(q_l, k_l, v_l, f_l, s_cf, s_cb)
    dn = dn_output(o_f, o_b, gate_lat, dn_norm_g)

    return out_proj_mlp(x, hy, dn, w_out.astype(BF16), lat_m[2], n2g, lat_m[3], lat_m[4], lat_m[5],
                        w1.astype(BF16), w2.astype(BF16), final_g=final_g, a_transposed=True)


def kernel(x, c, ctx, c_ctx, ada_w, ada_b, norm1_g, norm2_g, mlp_w1, mlp_w2, ev_w_in, ev_conv_w, ev_conv_b, ev_ln_g,
           ev_ln_b, ev_lq1, ev_lk1, ev_lq2, ev_lk2, ev_subln_g, ev_w_out, od_w_in, od_hy_short_w, od_hy_short_b,
           od_hy_w1, od_hy_b1, od_hy_freq1, od_hy_w2, od_hy_b2, od_hy_freq2, od_hy_w3, od_hy_skip, od_dn_conv_w,
           od_dn_alog_f, od_dn_alog_b, od_dn_dtb_f, od_dn_dtb_b, od_dn_norm_g, od_w_out, final_g):
    b, _, d = x.shape
    assert ada_w.shape[0] == 2, "layer 0 = conformer/diff-attention, layer 1 (last) = Hyena/DeltaNet"
    rows = -(-(b + 1) // SUBLANE) * SUBLANE
    cvec = jnp.zeros((rows, d), F32).at[:b].set(c).at[b].set(c_ctx)

    lat_m, ctx_m = _split_mods(ada_mods(cvec, ada_w[0], ada_b[0]), b, d)
    lam_vecs = jnp.stack([ev_lq1[0], ev_lk1[0], ev_lq2[0], ev_lk2[0]])
    x, ctx = _even_layer(x, ctx, lat_m, ctx_m, norm1_g[0], norm2_g[0], ev_w_in[0], ev_conv_w[0], ev_conv_b[0],
                         ev_ln_g[0], ev_ln_b[0], lam_vecs, ev_subln_g[0], ev_w_out[0], mlp_w1[0], mlp_w2[0],
                         0.8 - 0.6 * math.exp(-0.3 * 0))

    lat_m, ctx_m = _split_mods(ada_mods(cvec, ada_w[1], ada_b[1]), b, d)
    return _odd_layer_last(x, ctx, lat_m, ctx_m, norm1_g[1], norm2_g[1], od_w_in[0], od_hy_short_w[0],
                           od_hy_short_b[0], od_hy_w1[0], od_hy_b1[0], od_hy_freq1[0], od_hy_w2[0], od_hy_b2[0],
                           od_hy_freq2[0], od_hy_w3[0], od_hy_skip[0], od_dn_conv_w[0], od_dn_alog_f[0],
                           od_dn_alog_b[0], od_dn_dtb_f[0], od_dn_dtb_b[0], od_dn_norm_g[0], od_w_out[0],
                           mlp_w1[1], mlp_w2[1], final_g)
```

```python
import functools
import math

import jax
import jax.numpy as jnp
import numpy as np
from jax import lax
from jax.experimental import pallas as pl
from jax.experimental.pallas import tpu as pltpu

F32 = jnp.float32
BF16 = jnp.bfloat16

EPS = 1e-6
N_MOD = 6
GRID_W = 64
ROPE_BASE = 10000.0
CONV_W = 31
DIFF_HEADS = 4
DIFF_DQK = 64
DIFF_DV = 128
HY_ORDER = 2
HY_EMB = 33
HY_FFN = 64
HY_MAX_DECAY = math.log(1e-2) / 0.3
HY_MIN_DECAY = math.log(1e-2) / 1.5
DN_HEADS = 4
DN_DK = 128
DN_CONV_W = 5
DN_CHUNK = 128

LANE = 128
SUBLANE = 8
VMEM_LIMIT = 56 * 1024 * 1024


def _cparams(*sem):
    return pltpu.CompilerParams(dimension_semantics=sem, vmem_limit_bytes=VMEM_LIMIT)


def _tile(n, pref, mult=SUBLANE):
    if n <= pref:
        return n
    t = (pref // mult) * mult
    while t > mult and n % t:
        t -= mult
    assert n % t == 0, (n, pref, mult)
    return t


def _const_spec(shape):
    nd = len(shape)
    return pl.BlockSpec(shape, lambda *_: (0,) * nd, pipeline_mode=pl.Buffered(1))


def _silu(x):
    return x * jax.nn.sigmoid(x)


def _norm_mod(x, g, sh, sc):
    y = x * lax.rsqrt(jnp.mean(x * x, axis=-1, keepdims=True) + EPS)
    return (y * g) * (1.0 + sc) + sh


def _mods_kernel(c_ref, w_ref, b_ref, o_ref):
    cond = _silu(c_ref[...])
    o_ref[...] = jnp.dot(cond, w_ref[...], preferred_element_type=F32,
                         precision=lax.Precision.HIGHEST) + b_ref[...]


def ada_mods(cvec, w, b):
    r, d = cvec.shape
    n = w.shape[1]
    tn = _tile(n, 512, LANE)
    return pl.pallas_call(
        _mods_kernel,
        grid=(n // tn,),
        in_specs=[pl.BlockSpec((r, d), lambda j: (0, 0)),
                  pl.BlockSpec((d, tn), lambda j: (0, j)),
                  pl.BlockSpec((1, tn), lambda j: (0, j))],
        out_specs=pl.BlockSpec((r, tn), lambda j: (0, j)),
        out_shape=jax.ShapeDtypeStruct((r, n), F32),
        compiler_params=_cparams("arbitrary"),
        name="ada_mods",
    )(cvec, w, b.reshape(1, n))


def _swap16(y):
    n = y.shape[-1]
    lane = lax.broadcasted_iota(jnp.int32, y.shape, y.ndim - 1)
    fwd = pltpu.roll(y, n - 16, y.ndim - 1)
    bwd = pltpu.roll(y, 16, y.ndim - 1)
    return jnp.where((lane % 32) < 16, fwd, bwd)


PROJ_CHUNK = 512


def _proj_kernel(x_ref, g_ref, sh_ref, sc_ref, w_ref, *rest, splits, rope_cols):
    n_out = len(splits)
    if rope_cols:
        cq_ref, sq_ref, ck_ref, sk_ref = rest[:4]
        rest = rest[4:]
    o_refs = rest[:n_out]
    h = _norm_mod(x_ref[0], g_ref[...], sh_ref[0], sc_ref[0]).astype(BF16)
    for o_ref, (s0, s1) in zip(o_refs, splits):
        tc = PROJ_CHUNK if (s1 - s0) % PROJ_CHUNK == 0 and s0 % PROJ_CHUNK == 0 else LANE
        for c0 in range(s0, s1, tc):
            y = jnp.dot(h, w_ref[:, c0:c0 + tc], preferred_element_type=F32)
            if rope_cols and rope_cols[0] <= c0 < rope_cols[2]:
                is_q = c0 < rope_cols[1]
                cos = (cq_ref if is_q else ck_ref)[...]
                sin = (sq_ref if is_q else sk_ref)[...]
                reps = tc // LANE
                cos = jnp.concatenate([cos] * reps, axis=1)
                sin = jnp.concatenate([sin] * reps, axis=1)
                y = y * cos + _swap16(y) * sin
            o_ref[0, :, c0 - s0:c0 - s0 + tc] = y.astype(o_ref.dtype)


def norm_mod_proj(x, g, sh, sc, w, splits, dtypes, rope=None, rope_cols=None):
    b, l, d = x.shape
    n = w.shape[1]
    tm = _tile(l, 512)
    assert all(s % LANE == 0 for sp in splits for s in sp)
    if rope_cols:
        assert all(c % PROJ_CHUNK == 0 for c in rope_cols)
    row = lambda bi, i: (bi, i, 0)
    in_specs = [pl.BlockSpec((1, tm, d), row),
                _const_spec((1, d)),
                pl.BlockSpec((1, 1, d), lambda bi, i: (bi, 0, 0)),
                pl.BlockSpec((1, 1, d), lambda bi, i: (bi, 0, 0)),
                _const_spec((d, n))]
    args = [x, g.reshape(1, d), sh, sc, w]
    if rope_cols:
        in_specs += [pl.BlockSpec((tm, LANE), lambda bi, i: (i, 0))] * 4
        args += list(rope)
    return pl.pallas_call(
        functools.partial(_proj_kernel, splits=tuple(splits), rope_cols=rope_cols),
        grid=(b, l // tm),
        in_specs=in_specs,
        out_specs=[pl.BlockSpec((1, tm, s1 - s0), row) for s0, s1 in splits],
        out_shape=[jax.ShapeDtypeStruct((b, l, s1 - s0), dt) for (s0, s1), dt in zip(splits, dtypes)],
        compiler_params=_cparams("parallel", "parallel"),
        name="norm_mod_proj",
    )(*args)


CONV_HALO = 16


def _conformer_kernel(ac_ref, gc_ref, ap_ref, gp_ref, an_ref, gn_ref, w_ref, b_ref, lg_ref, lb_ref, o_ref,
                      u_ref, sh_ref, y_ref, *, tl, ch):
    i = pl.program_id(1)
    last = pl.num_programs(1) - 1
    left = (CONV_W - 1) // 2
    glu = lambda a, g: a * jax.nn.sigmoid(g)
    u_ref[CONV_HALO:CONV_HALO + tl, :] = glu(ac_ref[0], gc_ref[0])
    prev = glu(ap_ref[0], gp_ref[0])
    nxt = glu(an_ref[0], gn_ref[0])
    u_ref[0:CONV_HALO, :] = jnp.where(i > 0, prev, 0.0)
    u_ref[CONV_HALO + tl:CONV_HALO + tl + CONV_HALO, :] = jnp.where(i < last, nxt, 0.0)
    n = sh_ref.shape[1]
    for p in range(SUBLANE):
        sh_ref[p] = u_ref[p:p + n, :]
    rows = min(tl, 128)
    for r0 in range(0, tl, rows):
        for c0 in range(0, ch, LANE):
            acc = jnp.zeros((rows, LANE), F32) + b_ref[:, c0:c0 + LANE]
            for k in range(CONV_W):
                s = CONV_HALO - left + k + r0
                p, base = s % SUBLANE, s - s % SUBLANE
                acc = acc + w_ref[k:k + 1, c0:c0 + LANE] * sh_ref[p, base:base + rows, c0:c0 + LANE]
            y_ref[r0:r0 + rows, c0:c0 + LANE] = acc
    y = y_ref[...]
    mu = jnp.mean(y, axis=-1, keepdims=True)
    yc = y - mu
    var = jnp.mean(yc * yc, axis=-1, keepdims=True)
    z = yc * lax.rsqrt(var + EPS) * lg_ref[...] + lb_ref[...]
    o_ref[0] = _silu(z).astype(o_ref.dtype)


def conformer_conv(p, conv_w, conv_b, ln_g, ln_b):
    b, l, _ = p.shape
    ch = conv_w.shape[1]
    tl = _tile(l, 256, CONV_HALO)
    nh = tl // CONV_HALO
    nblk = l // CONV_HALO
    cur = lambda col: pl.BlockSpec((1, tl, ch), lambda bi, i: (bi, i, col))
    prv = lambda col: pl.BlockSpec((1, CONV_HALO, ch), lambda bi, i: (bi, jnp.maximum(i * nh - 1, 0), col))
    nxt = lambda col: pl.BlockSpec((1, CONV_HALO, ch), lambda bi, i: (bi, jnp.minimum((i + 1) * nh, nblk - 1), col))
    return pl.pallas_call(
        functools.partial(_conformer_kernel, tl=tl, ch=ch),
        grid=(b, l // tl),
        in_specs=[cur(0), cur(1), prv(0), prv(1), nxt(0), nxt(1),
                  _const_spec((CONV_W, ch)), _const_spec((1, ch)), _const_spec((1, ch)), _const_spec((1, ch))],
        out_specs=pl.BlockSpec((1, tl, ch), lambda bi, i: (bi, i, 0)),
        out_shape=jax.ShapeDtypeStruct((b, l, ch), BF16),
        scratch_shapes=[pltpu.VMEM((tl + 2 * CONV_HALO, ch), F32),
                        pltpu.VMEM((SUBLANE, tl + 2 * CONV_HALO - SUBLANE, ch), F32),
                        pltpu.VMEM((tl, ch), F32)],
        compiler_params=_cparams("parallel", "parallel"),
        name="conformer_conv",
    )(p, p, p, p, p, p, conv_w, conv_b.reshape(1, ch), ln_g.reshape(1, ch), ln_b.reshape(1, ch))


ATTN_QBLK = 256


ATTN_ROWS = 128
ATTN_ONES = 16
ATTN_REDO = 120.0
ATTN_REF_ROW = (DIFF_DQK, 0)


def _round_bf16(x):
    return x.astype(BF16).astype(F32)


def _diff_attn_kernel(qt_ref, k_ref, vt_ref, lam_ref, g_ref, o_ref, qm_ref, m_ref, acc_ref, flag_ref,
                      *, lambda_init):
    kv = pl.program_id(3)
    tq = qm_ref.shape[2]
    qblk = min(ATTN_QBLK, tq)
    chains = [(mi, slice(q0, q0 + qblk)) for q0 in range(0, tq, qblk) for mi in range(2)]

    @pl.when(kv == 0)
    def _():
        qt = qt_ref[0].astype(F32)
        row = lax.broadcasted_iota(jnp.int32, qt.shape, 0)
        qm_ref[0] = jnp.where(row < DIFF_DQK, qt, 0.0)
        qm_ref[1] = jnp.where(row < DIFF_DQK, 0.0, qt)
        m_ref[...] = jnp.zeros(m_ref.shape, F32)
        acc_ref[...] = jnp.zeros(acc_ref.shape, F32)

    k = k_ref[0]
    lane = lax.broadcasted_iota(jnp.int32, k.shape, 1)
    kx = [jnp.where(lane == r, jnp.ones_like(k), k) for r in ATTN_REF_ROW]
    vtx = vt_ref[0, 0]

    def scores(mi, qs):
        return jnp.dot(kx[mi], qm_ref[mi, :, qs].astype(BF16), preferred_element_type=F32)

    def move_reference(mi, qs, m_new):
        m_ref[mi, :, qs] = m_new
        r = ATTN_REF_ROW[mi]
        qm_ref[mi, r:r + 1, qs] = -m_new

    flag_ref[0] = (kv == 0).astype(jnp.int32)

    @pl.when(kv > 0)
    def _():
        sc = [scores(mi, qs) for mi, qs in chains]
        mxs, pvs = [], []
        for s in sc:
            es, mx = [], None
            for r0 in range(0, s.shape[0], ATTN_ROWS):
                grp = s[r0:r0 + ATTN_ROWS]
                es.append(jnp.exp2(grp).astype(BF16))
                top = jnp.max(grp, axis=0, keepdims=True)
                mx = top if mx is None else jnp.maximum(mx, top)
            mxs.append(mx)
            pvs.append(jnp.dot(vtx, jnp.concatenate(es, axis=0), preferred_element_type=F32))
        redo = jnp.max(functools.reduce(jnp.maximum, mxs)) > ATTN_REDO
        flag_ref[0] = redo.astype(jnp.int32)

        @pl.when(jnp.logical_not(redo))
        def _():
            for (mi, qs), mx, pv in zip(chains, mxs, pvs):
                m_old = m_ref[mi, :, qs]
                m_new = _round_bf16(m_old + jnp.maximum(mx, 0.0))
                acc_ref[mi, :, qs] = (acc_ref[mi, :, qs] + pv) * jnp.exp2(m_old - m_new)
                move_reference(mi, qs, m_new)

    @pl.when(flag_ref[0] != 0)
    def _():
        first = kv == 0
        sc = [scores(mi, qs) for mi, qs in chains]
        for (mi, qs), s in zip(chains, sc):
            mx = jnp.max(s, axis=0, keepdims=True)
            m_old = m_ref[mi, :, qs]
            m_new = _round_bf16(m_old + jnp.where(first, mx, jnp.maximum(mx, 0.0)))
            delta = m_new - m_old
            e = jnp.exp2(s - delta)
            alpha = jnp.exp2(-jnp.maximum(delta, 0.0))
            acc_ref[mi, :, qs] = alpha * acc_ref[mi, :, qs] + jnp.dot(vtx, e.astype(BF16),
                                                                      preferred_element_type=F32)
            move_reference(mi, qs, m_new)

    @pl.when(kv == pl.num_programs(3) - 1)
    def _():
        lam = (jnp.exp(jnp.sum(lam_ref[0:1, :] * lam_ref[1:2, :], axis=-1, keepdims=True))
               - jnp.exp(jnp.sum(lam_ref[2:3, :] * lam_ref[3:4, :], axis=-1, keepdims=True)) + lambda_init)
        dv = DIFF_DV
        ot = (acc_ref[0, :dv, :] / acc_ref[0, dv:dv + 1, :]
              - lam * (acc_ref[1, :dv, :] / acc_ref[1, dv:dv + 1, :]))
        ot = ot * lax.rsqrt(jnp.mean(ot * ot, axis=0, keepdims=True) + EPS)
        o = ot.T * g_ref[...]
        o_ref[0] = (o * (1.0 - lambda_init)).astype(o_ref.dtype)


def diff_attention(qt, k, vt, lam_vecs, subln_g, lambda_init, k_blk):
    b, _, lq = qt.shape
    lk = k.shape[1]
    tq = _tile(lq, 2048, LANE)
    tk = _tile(lk, 1024, LANE)
    rows = vt.shape[2]
    return pl.pallas_call(
        functools.partial(_diff_attn_kernel, lambda_init=lambda_init),
        grid=(b, DIFF_HEADS, lq // tq, lk // tk),
        in_specs=[pl.BlockSpec((1, LANE, tq), lambda bi, h, i, j: (bi, h, i)),
                  pl.BlockSpec((1, tk, LANE), lambda bi, h, i, j: (bi, j, k_blk + h)),
                  pl.BlockSpec((1, 1, rows, tk), lambda bi, h, i, j: (bi, h, 0, j)),
                  _const_spec((4, DIFF_DQK)), _const_spec((1, DIFF_DV))],
        out_specs=pl.BlockSpec((1, tq, LANE), lambda bi, h, i, j: (bi, i, h)),
        out_shape=jax.ShapeDtypeStruct((b, lq, DIFF_HEADS * DIFF_DV), BF16),
        scratch_shapes=[pltpu.VMEM((2, LANE, tq), F32), pltpu.VMEM((2, 1, tq), F32),
                        pltpu.VMEM((2, rows, tq), F32), pltpu.SMEM((1,), jnp.int32)],
        compiler_params=_cparams("parallel", "parallel", "parallel", "arbitrary"),
        name="diff_attention",
    )(qt, k, vt, lam_vecs, subln_g.reshape(1, DIFF_DV))


def _out_mlp_kernel(x_ref, a_ref, b_ref, wo_ref, gt1_ref, g2_ref, sh2_ref, sc2_ref, gt2_ref, w1_ref, w2_ref,
                    fg_ref, o_ref, *, a_transposed, hidden_chunk, final_norm):
    half = wo_ref.shape[0] // 2
    if a_transposed:
        ya = lax.dot_general(a_ref[0], wo_ref[0:half, :], (((0,), (0,)), ((), ())), preferred_element_type=F32)
    else:
        ya = jnp.dot(a_ref[0], wo_ref[0:half, :], preferred_element_type=F32)
    y = ya + jnp.dot(b_ref[0], wo_ref[half:, :], preferred_element_type=F32)
    x1 = x_ref[0] + gt1_ref[0] * y
    h = _norm_mod(x1, g2_ref[...], sh2_ref[0], sc2_ref[0]).astype(BF16)
    acc = jnp.zeros_like(x1)
    for c0 in range(0, w1_ref.shape[1], hidden_chunk):
        t = jnp.maximum(jnp.dot(h, w1_ref[:, c0:c0 + hidden_chunk], preferred_element_type=F32), 0.0)
        acc = acc + jnp.dot((t * t).astype(BF16), w2_ref[c0:c0 + hidden_chunk, :], preferred_element_type=F32)
    x2 = x1 + gt2_ref[0] * acc
    if final_norm:
        x2 = x2 * lax.rsqrt(jnp.mean(x2 * x2, axis=-1, keepdims=True) + EPS) * fg_ref[...]
    o_ref[0] = x2


def out_proj_mlp(x, mix_a, mix_b, w_out, gt1, g2, sh2, sc2, gt2, w1, w2, final_g=None, a_transposed=False):
    b, l, d = x.shape
    c = mix_b.shape[2]
    hid = w1.shape[1]
    tm = _tile(l, 512, LANE)
    row = lambda bi, i: (bi, i, 0)
    vec = lambda bi, i: (bi, 0, 0)
    a_spec = (pl.BlockSpec((1, c, tm), lambda bi, i: (bi, 0, i)) if a_transposed
              else pl.BlockSpec((1, tm, c), row))
    fg = jnp.ones((1, d), F32) if final_g is None else final_g.reshape(1, d)
    return pl.pallas_call(
        functools.partial(_out_mlp_kernel, a_transposed=a_transposed, hidden_chunk=min(hid, 1024),
                          final_norm=final_g is not None),
        grid=(b, l // tm),
        in_specs=[pl.BlockSpec((1, tm, d), row), a_spec, pl.BlockSpec((1, tm, c), row),
                  _const_spec((2 * c, d)),
                  pl.BlockSpec((1, 1, d), vec), _const_spec((1, d)),
                  pl.BlockSpec((1, 1, d), vec), pl.BlockSpec((1, 1, d), vec), pl.BlockSpec((1, 1, d), vec),
                  _const_spec((d, hid)), _const_spec((hid, d)), _const_spec((1, d))],
        out_specs=pl.BlockSpec((1, tm, d), row),
        out_shape=jax.ShapeDtypeStruct((b, l, d), F32),
        compiler_params=_cparams("parallel", "parallel"),
        name="out_proj_mlp",
    )(x, mix_a, mix_b, w_out, gt1, g2.reshape(1, d), sh2, sc2, gt2, w1, w2, fg)


def _rope_tables(length, scale):
    f32 = np.float32
    rows = length // GRID_W
    row = np.repeat(np.arange(rows, dtype=f32), GRID_W)
    col = np.tile(np.arange(GRID_W, dtype=f32), rows)
    n_freq = DIFF_DQK // 4
    inv = (f32(ROPE_BASE) ** (-np.arange(n_freq, dtype=f32) / f32(n_freq))).astype(f32)
    ang_r = row[:, None] * inv
    ang_c = col[:, None] * inv
    cos = np.concatenate([np.cos(ang_r)] * 2 + [np.cos(ang_c)] * 2, axis=-1)
    sin = np.concatenate([-np.sin(ang_r), np.sin(ang_r), -np.sin(ang_c), np.sin(ang_c)], axis=-1)
    cos = np.concatenate([cos, cos], axis=-1) * f32(scale)
    sin = np.concatenate([sin, sin], axis=-1) * f32(scale)
    return cos.astype(f32), sin.astype(f32)


def _flat_tables(length, scale):
    return np.full((length, LANE), scale, np.float32), np.zeros((length, LANE), np.float32)


def _proj_t_kernel(x_ref, g_ref, sh_ref, sc_ref, wt_ref, o_ref, *, rc):
    h = _norm_mod(x_ref[0], g_ref[...], sh_ref[0], sc_ref[0]).astype(BF16)
    for r0 in range(0, wt_ref.shape[0], rc):
        o_ref[0, r0:r0 + rc, :] = lax.dot_general(wt_ref[r0:r0 + rc, :], h, (((1,), (1,)), ((), ())),
                                                  preferred_element_type=F32)


def norm_mod_proj_t(x, g, sh, sc, wt):
    b, l, d = x.shape
    c = wt.shape[0]
    tl = _tile(l, 512, LANE)
    return pl.pallas_call(
        functools.partial(_proj_t_kernel, rc=_tile(c, 512)),
        grid=(b, l // tl),
        in_specs=[pl.BlockSpec((1, tl, d), lambda bi, i: (bi, i, 0)),
                  _const_spec((1, d)),
                  pl.BlockSpec((1, 1, d), lambda bi, i: (bi, 0, 0)),
                  pl.BlockSpec((1, 1, d), lambda bi, i: (bi, 0, 0)),
                  _const_spec((c, d))],
        out_specs=pl.BlockSpec((1, c, tl), lambda bi, i: (bi, 0, i)),
        out_shape=jax.ShapeDtypeStruct((b, c, l), F32),
        compiler_params=_cparams("parallel", "parallel"),
        name="norm_mod_proj_t",
    )(x, g.reshape(1, d), sh, sc, wt)


def _dot_hi(a, b):
    return jnp.dot(a, b, preferred_element_type=F32, precision=lax.Precision.HIGHEST)


def _hy_hidden_kernel(z_ref, w1_ref, b1_ref, f1_ref, w2_ref, b2_ref, f2_ref, o_ref):
    hid = jnp.sin(f1_ref[...] * (_dot_hi(w1_ref[...], z_ref[...]) + b1_ref[...]))
    o_ref[...] = jnp.sin(f2_ref[...] * (_dot_hi(w2_ref[...], hid) + b2_ref[...]))


def hyena_hidden(zt, w1t, b1, f1, w2t, b2, f2):
    e, n = zt.shape
    f = w1t.shape[0]
    tn = _tile(n, 2048, LANE)
    col = lambda v: v.reshape(f, 1)
    return pl.pallas_call(
        _hy_hidden_kernel,
        grid=(n // tn,),
        in_specs=[pl.BlockSpec((e, tn), lambda j: (0, j)),
                  _const_spec((f, e)), _const_spec((f, 1)), _const_spec((f, 1)),
                  _const_spec((f, f)), _const_spec((f, 1)), _const_spec((f, 1))],
        out_specs=pl.BlockSpec((f, tn), lambda j: (0, j)),
        out_shape=jax.ShapeDtypeStruct((f, n), F32),
        compiler_params=_cparams("parallel"),
        name="hyena_hidden",
    )(zt, w1t, col(b1), col(f1), w2t, col(b2), col(f2))


def _hy_filter_kernel(hid_ref, t_ref, wf_ref, wb_ref, dl_ref, o_ref, *, length, tn):
    n2 = hid_ref.shape[1]
    delta = dl_ref[...]

    def piece(c0):
        hid = hid_ref[:, c0:c0 + tn]
        dec = jnp.exp(-t_ref[:, c0:c0 + tn] * delta)
        return _dot_hi(wf_ref[...], hid) * dec, _dot_hi(wb_ref[...], hid) * dec

    norm = jnp.zeros((wf_ref.shape[0], 1), F32)
    for c0 in range(length, n2, tn):
        ff, fb = piece(c0)
        norm = norm + jnp.sum(jnp.abs(ff) + jnp.abs(fb), axis=-1, keepdims=True)
    inv = 1.0 / norm
    for c0 in range(0, n2, tn):
        ff, fb = piece(c0)
        if c0 >= length:
            o_ref[0, :, c0:c0 + tn] = ff * inv
        else:
            col = lax.broadcasted_iota(jnp.int32, fb.shape, 1) + c0
            o_ref[0, :, c0:c0 + tn] = jnp.where(col == 0, 0.0, fb * inv)


def hyena_filters(hid, t_row, w3t, deltas, length):
    f, n2 = hid.shape
    ch = deltas.shape[0]
    cb = LANE
    nblk = ch // cb
    tn = _tile(length, 2048, LANE)
    return pl.pallas_call(
        functools.partial(_hy_filter_kernel, length=length, tn=tn),
        grid=(HY_ORDER, nblk),
        in_specs=[_const_spec((f, n2)), _const_spec((1, n2)),
                  pl.BlockSpec((cb, f), lambda o, c: (o * 2 * nblk + c, 0)),
                  pl.BlockSpec((cb, f), lambda o, c: (o * 2 * nblk + nblk + c, 0)),
                  pl.BlockSpec((cb, 1), lambda o, c: (c, 0))],
        out_specs=pl.BlockSpec((1, cb, n2), lambda o, c: (o, c, 0)),
        out_shape=jax.ShapeDtypeStruct((HY_ORDER, ch, n2), F32),
        compiler_params=_cparams("parallel", "parallel"),
        name="hyena_filters",
    )(hid, t_row, w3t, w3t, deltas.reshape(ch, 1))


HY_T = 256
HY_CB = SUBLANE

def _hyena_kernel(v_ref, x1_ref, x2_ref, g_ref, sw_ref, sb_ref, sk_ref_s, o_ref, sk_ref, gb_ref, u_ref, acc_ref,
                  *, length, nch):
    nb = v_ref.shape[0]
    nj = length // HY_T
    cblk = pl.program_id(0)
    lane = lax.broadcasted_iota(jnp.int32, (nb, length), 1)

    def short(ref, cc, part):
        x = ref[:, cc, :]
        ch = part * nch + cblk * HY_CB + cc
        prev = jnp.where(lane == 0, 0.0, pltpu.roll(x, 1, 1))
        nxt = jnp.where(lane == length - 1, 0.0, pltpu.roll(x, length - 1, 1))
        return sw_ref[0, ch] * prev + sw_ref[1, ch] * x + sw_ref[2, ch] * nxt + sb_ref[ch]

    def long_conv(u, order, cc):
        half = HY_T // 2
        grp = 2 * SUBLANE
        row = lax.broadcasted_iota(jnp.int32, (grp, LANE), 0)
        lane_i = lax.broadcasted_iota(jnp.int32, (grp, LANE), 1)
        starts = tuple(range(0, half, grp))

        gb_ref[...] = jnp.broadcast_to(g_ref[order, pl.ds(cc, 1), :], gb_ref.shape)

        def rotations(j):
            blk = gb_ref[:, j * LANE:(j + 1) * LANE]
            return tuple(pltpu.roll(blk, r0, 1, stride=1, stride_axis=0) for r0 in starts)

        nblk = 2 * length // LANE
        prev = rotations(0)
        for j in range(1, nblk):
            cur = rotations(j)
            for r0, p, c in zip(starts, prev, cur):
                tile_rows = jnp.where(lane_i < row + r0, p, c).astype(BF16)
                sk_ref[r0:r0 + grp, j * LANE:(j + 1) * LANE] = tile_rows
                if j + 1 < nblk:
                    sk_ref[half + r0:half + r0 + grp, (j + 1) * LANE:(j + 2) * LANE] = tile_rows
            prev = cur
        for j in range(nj):
            u_ref[j * nb:(j + 1) * nb, :] = u[:, j * HY_T:(j + 1) * HY_T]
        acc_ref[...] = jnp.zeros(acc_ref.shape, F32)
        for d in range(-(nj - 1), nj):
            m = (nj - abs(d)) * nb
            lo_in, lo_out = max(0, -d) * nb, max(0, d) * nb
            tile = sk_ref[:, length + d * HY_T:length + (d + 1) * HY_T]
            acc_ref[lo_out:lo_out + m, :] += jnp.dot(u_ref[lo_in:lo_in + m, :].astype(BF16), tile,
                                                     preferred_element_type=F32)
        return jnp.concatenate([acc_ref[i * nb:(i + 1) * nb, :] for i in range(nj)], axis=1)

    def channel(cc, carry):
        v = short(v_ref, cc, 0)
        x1 = short(x1_ref, cc, 1)
        x2 = short(x2_ref, cc, 2)
        ch = cblk * HY_CB + cc
        z = x1 * (long_conv(v, 0, cc) + sk_ref_s[0, ch] * v)
        z = x2 * (long_conv(z, 1, cc) + sk_ref_s[1, ch] * z)
        o_ref[:, cc, :] = z
        return carry

    lax.fori_loop(0, HY_CB, channel, 0)


def hyena_mix(pt, filt, short_w, short_b, skip):
    b, c3, l = pt.shape
    nch = c3 // 3
    assert l % HY_T == 0 and nch % HY_CB == 0 and b == SUBLANE
    nblk = nch // HY_CB
    smem = pl.BlockSpec(memory_space=pltpu.SMEM)
    part = lambda k: pl.BlockSpec((b, HY_CB, l), lambda c: (0, k * nblk + c, 0))
    return pl.pallas_call(
        functools.partial(_hyena_kernel, length=l, nch=nch),
        grid=(nblk,),
        in_specs=[part(0), part(1), part(2),
                  pl.BlockSpec((HY_ORDER, HY_CB, 2 * l), lambda c: (0, c, 0)),
                  smem, smem, smem],
        out_specs=pl.BlockSpec((b, HY_CB, l), lambda c: (0, c, 0)),
        out_shape=jax.ShapeDtypeStruct((b, nch, l), F32),
        scratch_shapes=[pltpu.VMEM((HY_T, 2 * l), BF16), pltpu.VMEM((2 * SUBLANE, 2 * l), F32),
                        pltpu.VMEM((l // HY_T * b, HY_T), F32), pltpu.VMEM((l // HY_T * b, HY_T), F32)],
        compiler_params=_cparams("parallel"),
        name="hyena_mix",
    )(pt, pt, pt, filt, short_w, short_b, skip)


DN_HALO = SUBLANE
DN_BETA_F, DN_BETA_B, DN_G_F, DN_G_B = 0, DN_HEADS, 2 * DN_HEADS, 3 * DN_HEADS


def _dn_feat_kernel(c_ref, p_ref, n_ref, s_ref, w_ref, al_ref, dt_ref, q_ref, k_ref, v_ref, f_ref, u_ref, *, tl):
    i = pl.program_id(1)
    last = pl.num_programs(1) - 1
    left = (DN_CONV_W - 1) // 2
    nc = c_ref.shape[2]
    u_ref[DN_HALO:DN_HALO + tl, :] = c_ref[0]
    u_ref[0:DN_HALO, :] = jnp.where(i > 0, p_ref[0], 0.0)
    u_ref[DN_HALO + tl:DN_HALO + tl + DN_HALO, :] = jnp.where(i < last, n_ref[0], 0.0)
    rows = min(tl, 128)
    outs = (q_ref, k_ref, v_ref)
    per = nc // len(outs)
    for r0 in range(0, tl, rows):
        for c0 in range(0, nc, LANE):
            acc = jnp.zeros((rows, LANE), F32)
            for k in range(DN_CONV_W):
                s = DN_HALO - left + k + r0
                acc = acc + w_ref[k:k + 1, c0:c0 + LANE] * u_ref[s:s + rows, c0:c0 + LANE]
            y = _silu(acc)
            which, off = divmod(c0, per)
            if which < 2:
                y = y * lax.rsqrt(jnp.sum(y * y, axis=-1, keepdims=True) + 1e-6)
            outs[which][0, r0:r0 + rows, off:off + LANE] = y
    s = s_ref[0]
    lane = lax.broadcasted_iota(jnp.int32, s.shape, 1)
    gate = -jnp.exp(al_ref[...]) * jax.nn.softplus(s + dt_ref[...])
    f_ref[0] = jnp.where(lane < DN_G_F, jax.nn.sigmoid(s), gate)


def dn_features(qkv, small, conv_w, alog_row, dtb_row):
    b, l, nc = qkv.shape
    tl = _tile(l, 256, DN_HALO)
    nh = tl // DN_HALO
    nblk = l // DN_HALO
    w = nc // 3
    row = lambda bi, i: (bi, i, 0)
    return pl.pallas_call(
        functools.partial(_dn_feat_kernel, tl=tl),
        grid=(b, l // tl),
        in_specs=[pl.BlockSpec((1, tl, nc), row),
                  pl.BlockSpec((1, DN_HALO, nc), lambda bi, i: (bi, jnp.maximum(i * nh - 1, 0), 0)),
                  pl.BlockSpec((1, DN_HALO, nc), lambda bi, i: (bi, jnp.minimum((i + 1) * nh, nblk - 1), 0)),
                  pl.BlockSpec((1, tl, LANE), row),
                  _const_spec((DN_CONV_W, nc)), _const_spec((1, LANE)), _const_spec((1, LANE))],
        out_specs=[pl.BlockSpec((1, tl, w), row)] * 3 + [pl.BlockSpec((1, tl, LANE), row)],
        out_shape=[jax.ShapeDtypeStruct((b, l, w), F32)] * 3 + [jax.ShapeDtypeStruct((b, l, LANE), F32)],
        scratch_shapes=[pltpu.VMEM((tl + 2 * DN_HALO, nc), F32)],
        compiler_params=_cparams("parallel", "parallel"),
        name="dn_features",
    )(qkv, qkv, qkv, small, conv_w, alog_row, dtb_row)


def _bdot(a, b):
    return jnp.dot(a.astype(BF16), b.astype(BF16), preferred_element_type=F32)


def _bdot_nt(a, b):
    return lax.dot_general(a.astype(BF16), b.astype(BF16), (((1,), (1,)), ((), ())), preferred_element_type=F32)


def _bdot_tn(a, b):
    return lax.dot_general(a.astype(BF16), b.astype(BF16), (((0,), (0,)), ((), ())), preferred_element_type=F32)


def _cumsum_rows(x, reverse):
    n = x.shape[0]
    row = lax.broadcasted_iota(jnp.int32, x.shape, 0)
    s = 1
    while s < n:
        if reverse:
            x = x + jnp.where(row < n - s, pltpu.roll(x, n - s, 0), 0.0)
        else:
            x = x + jnp.where(row >= s, pltpu.roll(x, s, 0), 0.0)
        s *= 2
    return x


def _dn_chunks(chains):
    c = chains[0]["k"].shape[0]
    ri = lax.broadcasted_iota(jnp.int32, (c, c), 0)
    ci = lax.broadcasted_iota(jnp.int32, (c, c), 1)
    work = []
    for ch in chains:
        rev = ch["reverse"]
        incl = (ri <= ci) if rev else (ri >= ci)
        strict = (ri < ci) if rev else (ri > ci)
        decay = jnp.where(incl, jnp.exp(jnp.where(incl, ch["gcol"] - ch["grow"], 0.0)), 0.0)
        kb = ch["k"] * ch["beta"]
        qs = ch["q"] * (ch["k"].shape[1] ** -0.5)
        work.append(dict(ch, strict=strict, decay=decay, kb=kb, qs=qs))
    grams = [_bdot_nt(jnp.concatenate([w["kb"], w["qs"]], axis=0), w["k"]) for w in work]
    lowers = [jnp.where(w["strict"], g[:c] * w["decay"], 0.0) for w, g in zip(work, grams)]
    same2 = (ri // 2) == (ci // 2)
    ymats = [-jnp.where(same2, lo, 0.0) for lo in lowers]
    m = 2
    while m < c:
        pair = ((ri // (2 * m)) == (ci // (2 * m))) & ((ri // m) != (ci // m))
        offs = [jnp.where(pair, lo, 0.0) for lo in lowers]
        zs = [off + _bdot(ym, off) for ym, off in zip(ymats, offs)]
        ymats = [ym - z - _bdot(z, ym) for ym, z in zip(ymats, zs)]
        m *= 2
    outs = []
    uws, e_gs = [], []
    for w, ym in zip(work, ymats):
        e_g = jnp.exp(w["gcol"])
        rhs = jnp.concatenate([w["v"] * w["beta"], w["kb"] * e_g], axis=1)
        uws.append(rhs + _bdot(ym, rhs))
        e_gs.append(e_g)
    dv = work[0]["v"].shape[1]
    wss = [_bdot(jnp.concatenate([uw[:, dv:], w["qs"] * e_g], axis=0), w["state"])
           for w, uw, e_g in zip(work, uws, e_gs)]
    v_news = [uw[:, :dv] - ws[:c] for uw, ws in zip(uws, wss)]
    for w, g, ws, v_new in zip(work, grams, wss, v_news):
        gcol = w["gcol"]
        g_last = gcol[0:1, :] if w["reverse"] else gcol[c - 1:c, :]
        k_dec = w["k"] * jnp.exp(g_last - gcol)
        o = ws[c:] + _bdot(g[c:] * w["decay"], v_new)
        new_state = w["state"] * jnp.exp(g_last) + _bdot_tn(k_dec, v_new)
        outs.append((o, new_state))
    return outs


def _dn_scan_kernel(qf_ref, kf_ref, vf_ref, ff_ref, qb_ref, kb_ref, vb_ref, fb_ref, s0f_ref, s0b_ref,
                    of_ref, ob_ref, sf_ref, sb_ref, st_ref):
    n = pl.program_id(1)

    @pl.when(n == 0)
    def _():
        st_ref[0] = s0f_ref[0]
        st_ref[1] = s0b_ref[0]

    dirs = ((qf_ref, kf_ref, vf_ref, ff_ref, of_ref, False), (qb_ref, kb_ref, vb_ref, fb_ref, ob_ref, True))
    chains, dests = [], []
    for di, (q_ref, k_ref, v_ref, f_ref, o_ref, reverse) in enumerate(dirs):
        feats = f_ref[0]
        gc = _cumsum_rows(feats, reverse)
        gct = gc.T
        for h in range(DN_HEADS):
            cols = slice(h * DN_DK, (h + 1) * DN_DK)
            bl = (DN_BETA_B if reverse else DN_BETA_F) + h
            gl = (DN_G_B if reverse else DN_G_F) + h
            chains.append(dict(q=q_ref[0, :, cols], k=k_ref[0, :, cols], v=v_ref[0, :, cols],
                               beta=feats[:, bl:bl + 1], gcol=gc[:, gl:gl + 1], grow=gct[gl:gl + 1, :],
                               state=st_ref[di, h], reverse=reverse))
            dests.append((o_ref, cols, di, h))
    for (o_ref, cols, di, h), (o, s_new) in zip(dests, _dn_chunks(chains)):
        o_ref[0, :, cols] = o
        st_ref[di, h] = s_new

    @pl.when(n == pl.num_programs(1) - 1)
    def _():
        sf_ref[0] = st_ref[0]
        sb_ref[0] = st_ref[1]


def dn_scan(q, k, v, feats, s0f, s0b):
    b, l, w = q.shape
    nchunk = l // DN_CHUNK
    fwd = lambda bi, n: (bi, n, 0)
    bwd = lambda bi, n: (bi, nchunk - 1 - n, 0)
    st = lambda bi, n: (bi, 0, 0, 0)
    blk = lambda im, width: pl.BlockSpec((1, DN_CHUNK, width), im)
    st_spec = pl.BlockSpec((1, DN_HEADS, DN_DK, DN_DK), st)
    st_shape = jax.ShapeDtypeStruct((b, DN_HEADS, DN_DK, DN_DK), F32)
    return pl.pallas_call(
        _dn_scan_kernel,
        grid=(b, nchunk),
        in_specs=[blk(fwd, w), blk(fwd, w), blk(fwd, w), blk(fwd, LANE),
                  blk(bwd, w), blk(bwd, w), blk(bwd, w), blk(bwd, LANE), st_spec, st_spec],
        out_specs=[blk(fwd, w), blk(bwd, w), st_spec, st_spec],
        out_shape=[jax.ShapeDtypeStruct((b, l, w), F32)] * 2 + [st_shape, st_shape],
        scratch_shapes=[pltpu.VMEM((2, DN_HEADS, DN_DK, DN_DK), F32)],
        compiler_params=_cparams("parallel", "arbitrary"),
        name="dn_scan",
    )(q, k, v, feats, q, k, v, feats, s0f, s0b)


def _dn_out_kernel(of_ref, ob_ref, gate_ref, g_ref, o_ref):
    o = of_ref[0] + ob_ref[0]
    for c0 in range(0, o.shape[1], DN_DK):
        t = o[:, c0:c0 + DN_DK]
        t = t * lax.rsqrt(jnp.mean(t * t, axis=-1, keepdims=True) + EPS) * g_ref[...]
        o_ref[0, :, c0:c0 + DN_DK] = (t * _silu(gate_ref[0, :, c0:c0 + DN_DK])).astype(o_ref.dtype)


def dn_output(o_f, o_b, gate, norm_g):
    b, l, w = o_f.shape
    tl = _tile(l, 512)
    row = lambda bi, i: (bi, i, 0)
    return pl.pallas_call(
        _dn_out_kernel,
        grid=(b, l // tl),
        in_specs=[pl.BlockSpec((1, tl, w), row)] * 3 + [_const_spec((1, DN_DK))],
        out_specs=pl.BlockSpec((1, tl, w), row),
        out_shape=jax.ShapeDtypeStruct((b, l, w), BF16),
        compiler_params=_cparams("parallel", "parallel"),
        name="dn_output",
    )(o_f, o_b, gate, norm_g.reshape(1, DN_DK))


def _split_mods(mods, b, d):
    lat = [mods[:b, k * d:(k + 1) * d].reshape(b, 1, d) for k in range(N_MOD)]
    ctx = [jnp.broadcast_to(mods[b, k * d:(k + 1) * d].reshape(1, 1, d), (b, 1, d)) for k in range(N_MOD)]
    return lat, ctx


def _even_layer(x, ctx, lat_m, ctx_m, n1g, n2g, w_in, conv_w, conv_b, ln_g, ln_b, lam_vecs, subln_g, w_out,
                w1, w2, lambda_init):
    b, l, d = x.shape
    lc = ctx.shape[1]
    ch = conv_w.shape[1]
    qk = DIFF_HEADS * 2 * DIFF_DQK
    q0, k0, v0, n_in = 2 * ch, 2 * ch + qk, 2 * ch + 2 * qk, 2 * ch + 2 * qk + DIFF_HEADS * DIFF_DV
    scale = DIFF_DQK ** -0.5 * math.log2(math.e)
    splits = [(0, q0), (q0, n_in)]
    w_in = w_in.astype(BF16)
    rope_lat = _rope_tables(l, scale) + _rope_tables(l, 1.0)
    rope_ctx = _flat_tables(lc, scale) + _flat_tables(lc, 1.0)
    pc_lat, qkv_lat = norm_mod_proj(x, n1g, lat_m[0], lat_m[1], w_in, splits, (F32, BF16), rope_lat, (q0, k0, v0))
    pc_ctx, qkv_ctx = norm_mod_proj(ctx, n1g, ctx_m[0], ctx_m[1], w_in, splits, (F32, BF16), rope_ctx, (q0, k0, v0))
    conv_lat = conformer_conv(pc_lat, conv_w, conv_b, ln_g, ln_b)
    conv_ctx = conformer_conv(pc_ctx, conv_w, conv_b, ln_g, ln_b)
    k_all = jnp.concatenate([qkv_lat[:, :, qk:2 * qk], qkv_ctx[:, :, qk:2 * qk]], axis=1)
    def heads_t(v):
        vt = jnp.swapaxes(v, 1, 2).reshape(b, DIFF_HEADS, DIFF_DV, v.shape[1])
        return jnp.concatenate([vt, jnp.ones((b, DIFF_HEADS, ATTN_ONES, v.shape[1]), vt.dtype)], axis=2)

    vt_ctx = heads_t(qkv_ctx[:, :, 2 * qk:])
    vt_all = jnp.concatenate([heads_t(qkv_lat[:, :, 2 * qk:]), vt_ctx], axis=3)
    qt_lat = jnp.swapaxes(qkv_lat[:, :, :qk], 1, 2)
    qt_ctx = jnp.swapaxes(qkv_ctx[:, :, :qk], 1, 2)
    o_lat = diff_attention(qt_lat, k_all, vt_all, lam_vecs, subln_g, lambda_init, 0)
    o_ctx = diff_attention(qt_ctx, qkv_ctx, vt_ctx, lam_vecs, subln_g, lambda_init, qk // LANE)
    w_out, w1, w2 = w_out.astype(BF16), w1.astype(BF16), w2.astype(BF16)
    x = out_proj_mlp(x, conv_lat, o_lat, w_out, lat_m[2], n2g, lat_m[3], lat_m[4], lat_m[5], w1, w2)
    ctx = out_proj_mlp(ctx, conv_ctx, o_ctx, w_out, ctx_m[2], n2g, ctx_m[3], ctx_m[4], ctx_m[5], w1, w2)
    return x, ctx


def _hyena_position_tables(length):
    f32 = np.float32
    t = np.linspace(0.0, 1.0, length, dtype=f32)[:, None]
    bands = (HY_EMB - 1) // 2
    omega = (f32(2.0 * math.pi) * np.arange(length, dtype=f32)[:, None] / f32(length)).astype(f32)
    ang = omega * np.linspace(1e-4, bands - 1, bands, dtype=f32)
    z = np.concatenate([t, np.cos(ang), -np.sin(ang)], axis=-1).astype(f32)
    pos = np.minimum(np.abs(np.arange(2 * length) - length), length - 1)
    pad = -HY_EMB % SUBLANE
    zt = np.pad(z[pos].T, ((0, pad), (0, 0)))
    return np.ascontiguousarray(zt), np.ascontiguousarray(t[pos].T)


def _odd_layer_last(x, ctx, lat_m, ctx_m, n1g, n2g, w_in, hy_short_w, hy_short_b, hy_w1, hy_b1, hy_f1, hy_w2, hy_b2,
                    hy_f2, hy_w3, hy_skip, dn_conv_w, alog_f, alog_b, dtb_f, dtb_b, dn_norm_g, w_out, w1, w2,
                    final_g):
    b, l, d = x.shape
    hy_in = hy_short_w.shape[1]
    hy_ch = hy_in // (HY_ORDER + 1)
    dn_w = DN_HEADS * DN_DK
    n_rest = w_in.shape[1] - hy_in
    n_pad = -n_rest % LANE
    w_rest = jnp.pad(w_in[:, hy_in:], ((0, 0), (0, n_pad))).astype(BF16)
    splits = [(0, dn_w), (dn_w, 4 * dn_w), (4 * dn_w, n_rest + n_pad)]
    f32x3 = (F32, F32, F32)
    gate_lat, qkv_lat, small_lat = norm_mod_proj(x, n1g, lat_m[0], lat_m[1], w_rest, splits, f32x3)
    _, qkv_ctx, small_ctx = norm_mod_proj(ctx, n1g, ctx_m[0], ctx_m[1], w_rest, splits, f32x3)

    pt = norm_mod_proj_t(x, n1g, lat_m[0], lat_m[1], w_in[:, :hy_in].T.astype(BF16))
    zt, t_row = _hyena_position_tables(l)
    w1t = jnp.pad(hy_w1.T, ((0, 0), (0, zt.shape[0] - hy_w1.shape[0])))
    hid = hyena_hidden(zt, w1t, hy_b1, hy_f1, hy_w2.T, hy_b2, hy_f2)
    deltas = jnp.abs(jnp.linspace(HY_MIN_DECAY, HY_MAX_DECAY, hy_ch, dtype=F32))
    filt = hyena_filters(hid, t_row, hy_w3.T, deltas, l)
    hy = hyena_mix(pt, filt, hy_short_w, hy_short_b, hy_skip)

    lanes = jnp.zeros((LANE,), F32)
    alog_row = lanes.at[DN_G_F:DN_G_F + DN_HEADS].set(alog_f).at[DN_G_B:DN_G_B + DN_HEADS].set(alog_b).reshape(1, LANE)
    dtb_row = lanes.at[DN_G_F:DN_G_F + DN_HEADS].set(dtb_f).at[DN_G_B:DN_G_B + DN_HEADS].set(dtb_b).reshape(1, LANE)
    q_c, k_c, v_c, f_c = dn_features(qkv_ctx, small_ctx, dn_conv_w, alog_row, dtb_row)
    q_l, k_l, v_l, f_l = dn_features(qkv_lat, small_lat, dn_conv_w, alog_row, dtb_row)
    s0 = jnp.zeros((b, DN_HEADS, DN_DK, DN_DK), F32)
    _, _, s_cf, s_cb = dn_scan(q_c, k_c, v_c, f_c, s0, s0)
    o_f, o_b, _, _ = dn_scan(q_l, k_l, v_l, f_l, s_cf, s_cb)
    dn = dn_output(o_f, o_b, gate_lat, dn_norm_g)

    return out_proj_mlp(x, hy, dn, w_out.astype(BF16), lat_m[2], n2g, lat_m[3], lat_m[4], lat_m[5],
                        w1.astype(BF16), w2.astype(BF16), final_g=final_g, a_transposed=True)


def kernel(x, c, ctx, c_ctx, ada_w, ada_b, norm1_g, norm2_g, mlp_w1, mlp_w2, ev_w_in, ev_conv_w, ev_conv_b, ev_ln_g,
           ev_ln_b, ev_lq1, ev_lk1, ev_lq2, ev_lk2, ev_subln_g, ev_w_out, od_w_in, od_hy_short_w, od_hy_short_b,
           od_hy_w1, od_hy_b1, od_hy_freq1, od_hy_w2, od_hy_b2, od_hy_freq2, od_hy_w3, od_hy_skip, od_dn_conv_w,
           od_dn_alog_f, od_dn_alog_b, od_dn_dtb_f, od_dn_dtb_b, od_dn_norm_g, od_w_out, final_g):
    b, _, d = x.shape
    assert ada_w.shape[0] == 2, "layer 0 = conformer/diff-attention, layer 1 (last) = Hyena/DeltaNet"
    rows = -(-(b + 1) // SUBLANE) * SUBLANE
    cvec = jnp.zeros((rows, d), F32).at[:b].set(c).at[b].set(c_ctx)

    lat_m, ctx_m = _split_mods(ada_mods(cvec, ada_w[0], ada_b[0]), b, d)
    lam_vecs = jnp.stack([ev_lq1[0], ev_lk1[0], ev_lq2[0], ev_lk2[0]])
    x, ctx = _even_layer(x, ctx, lat_m, ctx_m, norm1_g[0], norm2_g[0], ev_w_in[0], ev_conv_w[0], ev_conv_b[0],
                         ev_ln_g[0], ev_ln_b[0], lam_vecs, ev_subln_g[0], ev_w_out[0], mlp_w1[0], mlp_w2[0],
                         0.8 - 0.6 * math.exp(-0.3 * 0))

    lat_m, ctx_m = _split_mods(ada_mods(cvec, ada_w[1], ada_b[1]), b, d)
    return _odd_layer_last(x, ctx, lat_m, ctx_m, norm1_g[1], norm2_g[1], od_w_in[0], od_hy_short_w[0],
                           od_hy_short_b[0], od_hy_w1[0], od_hy_b1[0], od_hy_freq1[0], od_hy_w2[0], od_hy_b2[0],
                           od_hy_freq2[0], od_hy_w3[0], od_hy_skip[0], od_dn_conv_w[0], od_dn_alog_f[0],
                           od_dn_alog_b[0], od_dn_dtb_f[0], od_dn_dtb_b[0], od_dn_norm_g[0], od_w_out[0],
                           mlp_w1[1], mlp_w2[1], final_g)
```

```python
import functools
import math

import jax
import jax.numpy as jnp
import numpy as np
from jax import lax
from jax.experimental import pallas as pl
from jax.experimental.pallas import tpu as pltpu

F32 = jnp.float32
BF16 = jnp.bfloat16

EPS = 1e-6
N_MOD = 6
GRID_W = 64
ROPE_BASE = 10000.0
CONV_W = 31
DIFF_HEADS = 4
DIFF_DQK = 64
DIFF_DV = 128
HY_ORDER = 2
HY_EMB = 33
HY_FFN = 64
HY_MAX_DECAY = math.log(1e-2) / 0.3
HY_MIN_DECAY = math.log(1e-2) / 1.5
DN_HEADS = 4
DN_DK = 128
DN_CONV_W = 5
DN_CHUNK = 128

LANE = 128
SUBLANE = 8
VMEM_LIMIT = 56 * 1024 * 1024


def _cparams(*sem):
    return pltpu.CompilerParams(dimension_semantics=sem, vmem_limit_bytes=VMEM_LIMIT)


def _tile(n, pref, mult=SUBLANE):
    if n <= pref:
        return n
    t = (pref // mult) * mult
    while t > mult and n % t:
        t -= mult
    assert n % t == 0, (n, pref, mult)
    return t


def _const_spec(shape):
    nd = len(shape)
    return pl.BlockSpec(shape, lambda *_: (0,) * nd, pipeline_mode=pl.Buffered(1))


def _silu(x):
    return x * jax.nn.sigmoid(x)


def _norm_mod(x, g, sh, sc):
    y = x * lax.rsqrt(jnp.mean(x * x, axis=-1, keepdims=True) + EPS)
    return (y * g) * (1.0 + sc) + sh


def _mods_kernel(c_ref, w_ref, b_ref, o_ref):
    cond = _silu(c_ref[...])
    o_ref[...] = jnp.dot(cond, w_ref[...], preferred_element_type=F32,
                         precision=lax.Precision.HIGHEST) + b_ref[...]


def ada_mods(cvec, w, b):
    r, d = cvec.shape
    n = w.shape[1]
    tn = _tile(n, 512, LANE)
    return pl.pallas_call(
        _mods_kernel,
        grid=(n // tn,),
        in_specs=[pl.BlockSpec((r, d), lambda j: (0, 0)),
                  pl.BlockSpec((d, tn), lambda j: (0, j)),
                  pl.BlockSpec((1, tn), lambda j: (0, j))],
        out_specs=pl.BlockSpec((r, tn), lambda j: (0, j)),
        out_shape=jax.ShapeDtypeStruct((r, n), F32),
        compiler_params=_cparams("arbitrary"),
        name="ada_mods",
    )(cvec, w, b.reshape(1, n))


def _swap16(y):
    n = y.shape[-1]
    lane = lax.broadcasted_iota(jnp.int32, y.shape, y.ndim - 1)
    fwd = pltpu.roll(y, n - 16, y.ndim - 1)
    bwd = pltpu.roll(y, 16, y.ndim - 1)
    return jnp.where((lane % 32) < 16, fwd, bwd)


PROJ_CHUNK = 512


def _proj_kernel(x_ref, g_ref, sh_ref, sc_ref, w_ref, *rest, splits, rope_cols):
    n_out = len(splits)
    if rope_cols:
        cq_ref, sq_ref, ck_ref, sk_ref = rest[:4]
        rest = rest[4:]
    o_refs = rest[:n_out]
    h = _norm_mod(x_ref[0], g_ref[...], sh_ref[0], sc_ref[0]).astype(BF16)
    for o_ref, (s0, s1) in zip(o_refs, splits):
        tc = PROJ_CHUNK if (s1 - s0) % PROJ_CHUNK == 0 and s0 % PROJ_CHUNK == 0 else LANE
        for c0 in range(s0, s1, tc):
            y = jnp.dot(h, w_ref[:, c0:c0 + tc], preferred_element_type=F32)
            if rope_cols and rope_cols[0] <= c0 < rope_cols[2]:
                is_q = c0 < rope_cols[1]
                cos = (cq_ref if is_q else ck_ref)[...]
                sin = (sq_ref if is_q else sk_ref)[...]
                reps = tc // LANE
                cos = jnp.concatenate([cos] * reps, axis=1)
                sin = jnp.concatenate([sin] * reps, axis=1)
                y = y * cos + _swap16(y) * sin
            o_ref[0, :, c0 - s0:c0 - s0 + tc] = y.astype(o_ref.dtype)


def norm_mod_proj(x, g, sh, sc, w, splits, dtypes, rope=None, rope_cols=None):
    b, l, d = x.shape
    n = w.shape[1]
    tm = _tile(l, 512)
    assert all(s % LANE == 0 for sp in splits for s in sp)
    if rope_cols:
        assert all(c % PROJ_CHUNK == 0 for c in rope_cols)
    row = lambda bi, i: (bi, i, 0)
    in_specs = [pl.BlockSpec((1, tm, d), row),
                _const_spec((1, d)),
                pl.BlockSpec((1, 1, d), lambda bi, i: (bi, 0, 0)),
                pl.BlockSpec((1, 1, d), lambda bi, i: (bi, 0, 0)),
                _const_spec((d, n))]
    args = [x, g.reshape(1, d), sh, sc, w]
    if rope_cols:
        in_specs += [pl.BlockSpec((tm, LANE), lambda bi, i: (i, 0))] * 4
        args += list(rope)
    return pl.pallas_call(
        functools.partial(_proj_kernel, splits=tuple(splits), rope_cols=rope_cols),
        grid=(b, l // tm),
        in_specs=in_specs,
        out_specs=[pl.BlockSpec((1, tm, s1 - s0), row) for s0, s1 in splits],
        out_shape=[jax.ShapeDtypeStruct((b, l, s1 - s0), dt) for (s0, s1), dt in zip(splits, dtypes)],
        compiler_params=_cparams("parallel", "parallel"),
        name="norm_mod_proj",
    )(*args)


CONV_HALO = 16


def _conformer_kernel(ac_ref, gc_ref, ap_ref, gp_ref, an_ref, gn_ref, w_ref, b_ref, lg_ref, lb_ref, o_ref,
                      u_ref, sh_ref, y_ref, *, tl, ch):
    i = pl.program_id(1)
    last = pl.num_programs(1) - 1
    left = (CONV_W - 1) // 2
    glu = lambda a, g: a * jax.nn.sigmoid(g)
    u_ref[CONV_HALO:CONV_HALO + tl, :] = glu(ac_ref[0], gc_ref[0])
    prev = glu(ap_ref[0], gp_ref[0])
    nxt = glu(an_ref[0], gn_ref[0])
    u_ref[0:CONV_HALO, :] = jnp.where(i > 0, prev, 0.0)
    u_ref[CONV_HALO + tl:CONV_HALO + tl + CONV_HALO, :] = jnp.where(i < last, nxt, 0.0)
    n = sh_ref.shape[1]
    for p in range(SUBLANE):
        sh_ref[p] = u_ref[p:p + n, :]
    rows = min(tl, 128)
    for r0 in range(0, tl, rows):
        for c0 in range(0, ch, LANE):
            acc = jnp.zeros((rows, LANE), F32) + b_ref[:, c0:c0 + LANE]
            for k in range(CONV_W):
                s = CONV_HALO - left + k + r0
                p, base = s % SUBLANE, s - s % SUBLANE
                acc = acc + w_ref[k:k + 1, c0:c0 + LANE] * sh_ref[p, base:base + rows, c0:c0 + LANE]
            y_ref[r0:r0 + rows, c0:c0 + LANE] = acc
    y = y_ref[...]
    mu = jnp.mean(y, axis=-1, keepdims=True)
    yc = y - mu
    var = jnp.mean(yc * yc, axis=-1, keepdims=True)
    z = yc * lax.rsqrt(var + EPS) * lg_ref[...] + lb_ref[...]
    o_ref[0] = _silu(z).astype(o_ref.dtype)


def conformer_conv(p, conv_w, conv_b, ln_g, ln_b):
    b, l, _ = p.shape
    ch = conv_w.shape[1]
    tl = _tile(l, 256, CONV_HALO)
    nh = tl // CONV_HALO
    nblk = l // CONV_HALO
    cur = lambda col: pl.BlockSpec((1, tl, ch), lambda bi, i: (bi, i, col))
    prv = lambda col: pl.BlockSpec((1, CONV_HALO, ch), lambda bi, i: (bi, jnp.maximum(i * nh - 1, 0), col))
    nxt = lambda col: pl.BlockSpec((1, CONV_HALO, ch), lambda bi, i: (bi, jnp.minimum((i + 1) * nh, nblk - 1), col))
    return pl.pallas_call(
        functools.partial(_conformer_kernel, tl=tl, ch=ch),
        grid=(b, l // tl),
        in_specs=[cur(0), cur(1), prv(0), prv(1), nxt(0), nxt(1),
                  _const_spec((CONV_W, ch)), _const_spec((1, ch)), _const_spec((1, ch)), _const_spec((1, ch))],
        out_specs=pl.BlockSpec((1, tl, ch), lambda bi, i: (bi, i, 0)),
        out_shape=jax.ShapeDtypeStruct((b, l, ch), BF16),
        scratch_shapes=[pltpu.VMEM((tl + 2 * CONV_HALO, ch), F32),
                        pltpu.VMEM((SUBLANE, tl + 2 * CONV_HALO - SUBLANE, ch), F32),
                        pltpu.VMEM((tl, ch), F32)],
        compiler_params=_cparams("parallel", "parallel"),
        name="conformer_conv",
    )(p, p, p, p, p, p, conv_w, conv_b.reshape(1, ch), ln_g.reshape(1, ch), ln_b.reshape(1, ch))


ATTN_QBLK = 256


ATTN_ROWS = 128
ATTN_ONES = 16
ATTN_REDO = 120.0
ATTN_REF_ROW = (DIFF_DQK, 0)


def _round_bf16(x):
    return x.astype(BF16).astype(F32)


def _diff_attn_kernel(qt_ref, k_ref, vt_ref, lam_ref, g_ref, o_ref, qm_ref, m_ref, acc_ref, *, lambda_init):
    kv = pl.program_id(3)
    tq = qm_ref.shape[2]
    qblk = min(ATTN_QBLK, tq)
    chains = [(mi, slice(q0, q0 + qblk)) for q0 in range(0, tq, qblk) for mi in range(2)]

    k = k_ref[0]
    lane = lax.broadcasted_iota(jnp.int32, k.shape, 1)
    kx = [jnp.where(lane == r, jnp.ones_like(k), k) for r in ATTN_REF_ROW]
    vtx = vt_ref[0, 0]

    def scores(mi, qs, rows=None):
        keys = kx[mi] if rows is None else kx[mi][:rows]
        return jnp.dot(keys, qm_ref[mi, :, qs].astype(BF16), preferred_element_type=F32)

    def move_reference(mi, qs, m_new):
        m_ref[mi, :, qs] = m_new
        r = ATTN_REF_ROW[mi]
        qm_ref[mi, r:r + 1, qs] = -m_new

    @pl.when(kv == 0)
    def _():
        qt = qt_ref[0].astype(F32)
        row = lax.broadcasted_iota(jnp.int32, qt.shape, 0)
        qm_ref[0] = jnp.where(row < DIFF_DQK, qt, 0.0)
        qm_ref[1] = jnp.where(row < DIFF_DQK, 0.0, qt)
        acc_ref[...] = jnp.zeros(acc_ref.shape, F32)
        first = [scores(mi, qs, ATTN_ROWS) for mi, qs in chains]
        for (mi, qs), s in zip(chains, first):
            move_reference(mi, qs, _round_bf16(jnp.max(s, axis=0, keepdims=True)))

    sc = [scores(mi, qs) for mi, qs in chains]
    mxs, pvs = [], []
    for s in sc:
        es, mx = [], None
        for r0 in range(0, s.shape[0], ATTN_ROWS):
            grp = s[r0:r0 + ATTN_ROWS]
            es.append(jnp.exp2(grp).astype(BF16))
            top = jnp.max(grp, axis=0, keepdims=True)
            mx = top if mx is None else jnp.maximum(mx, top)
        mxs.append(mx)
        pvs.append(jnp.dot(vtx, jnp.concatenate(es, axis=0), preferred_element_type=F32))
    redo = jnp.max(functools.reduce(jnp.maximum, mxs)) > ATTN_REDO

    @pl.when(jnp.logical_not(redo))
    def _():
        for (mi, qs), mx, pv in zip(chains, mxs, pvs):
            m_old = m_ref[mi, :, qs]
            m_new = _round_bf16(m_old + jnp.maximum(mx, 0.0))
            acc_ref[mi, :, qs] = (acc_ref[mi, :, qs] + pv) * jnp.exp2(m_old - m_new)
            move_reference(mi, qs, m_new)

    @pl.when(redo)
    def _():
        for (mi, qs), s in zip(chains, sc):
            m_old = m_ref[mi, :, qs]
            m_new = _round_bf16(m_old + jnp.maximum(jnp.max(s, axis=0, keepdims=True), 0.0))
            delta = m_new - m_old
            e = jnp.exp2(s - delta)
            acc_ref[mi, :, qs] = jnp.exp2(-delta) * acc_ref[mi, :, qs] + jnp.dot(vtx, e.astype(BF16),
                                                                                 preferred_element_type=F32)
            move_reference(mi, qs, m_new)

    @pl.when(kv == pl.num_programs(3) - 1)
    def _():
        lam = (jnp.exp(jnp.sum(lam_ref[0:1, :] * lam_ref[1:2, :], axis=-1, keepdims=True))
               - jnp.exp(jnp.sum(lam_ref[2:3, :] * lam_ref[3:4, :], axis=-1, keepdims=True)) + lambda_init)
        dv = DIFF_DV
        ot = (acc_ref[0, :dv, :] / acc_ref[0, dv:dv + 1, :]
              - lam * (acc_ref[1, :dv, :] / acc_ref[1, dv:dv + 1, :]))
        ot = ot * lax.rsqrt(jnp.mean(ot * ot, axis=0, keepdims=True) + EPS)
        o = ot.T * g_ref[...]
        o_ref[0] = (o * (1.0 - lambda_init)).astype(o_ref.dtype)


def diff_attention(qt, k, vt, lam_vecs, subln_g, lambda_init, k_blk):
    b, _, lq = qt.shape
    lk = k.shape[1]
    tq = _tile(lq, 2048, LANE)
    tk = _tile(lk, 1024, LANE)
    rows = vt.shape[2]
    return pl.pallas_call(
        functools.partial(_diff_attn_kernel, lambda_init=lambda_init),
        grid=(b, DIFF_HEADS, lq // tq, lk // tk),
        in_specs=[pl.BlockSpec((1, LANE, tq), lambda bi, h, i, j: (bi, h, i)),
                  pl.BlockSpec((1, tk, LANE), lambda bi, h, i, j: (bi, j, k_blk + h)),
                  pl.BlockSpec((1, 1, rows, tk), lambda bi, h, i, j: (bi, h, 0, j)),
                  _const_spec((4, DIFF_DQK)), _const_spec((1, DIFF_DV))],
        out_specs=pl.BlockSpec((1, tq, LANE), lambda bi, h, i, j: (bi, i, h)),
        out_shape=jax.ShapeDtypeStruct((b, lq, DIFF_HEADS * DIFF_DV), BF16),
        scratch_shapes=[pltpu.VMEM((2, LANE, tq), F32), pltpu.VMEM((2, 1, tq), F32),
                        pltpu.VMEM((2, rows, tq), F32)],
        compiler_params=_cparams("parallel", "parallel", "parallel", "arbitrary"),
        name="diff_attention",
    )(qt, k, vt, lam_vecs, subln_g.reshape(1, DIFF_DV))


def _out_mlp_kernel(x_ref, a_ref, b_ref, wo_ref, gt1_ref, g2_ref, sh2_ref, sc2_ref, gt2_ref, w1_ref, w2_ref,
                    fg_ref, o_ref, *, a_transposed, hidden_chunk, final_norm):
    half = wo_ref.shape[0] // 2
    if a_transposed:
        ya = lax.dot_general(a_ref[0], wo_ref[0:half, :], (((0,), (0,)), ((), ())), preferred_element_type=F32)
    else:
        ya = jnp.dot(a_ref[0], wo_ref[0:half, :], preferred_element_type=F32)
    y = ya + jnp.dot(b_ref[0], wo_ref[half:, :], preferred_element_type=F32)
    x1 = x_ref[0] + gt1_ref[0] * y
    h = _norm_mod(x1, g2_ref[...], sh2_ref[0], sc2_ref[0]).astype(BF16)
    acc = jnp.zeros_like(x1)
    for c0 in range(0, w1_ref.shape[1], hidden_chunk):
        t = jnp.maximum(jnp.dot(h, w1_ref[:, c0:c0 + hidden_chunk], preferred_element_type=F32), 0.0)
        acc = acc + jnp.dot((t * t).astype(BF16), w2_ref[c0:c0 + hidden_chunk, :], preferred_element_type=F32)
    x2 = x1 + gt2_ref[0] * acc
    if final_norm:
        x2 = x2 * lax.rsqrt(jnp.mean(x2 * x2, axis=-1, keepdims=True) + EPS) * fg_ref[...]
    o_ref[0] = x2


def out_proj_mlp(x, mix_a, mix_b, w_out, gt1, g2, sh2, sc2, gt2, w1, w2, final_g=None, a_transposed=False):
    b, l, d = x.shape
    c = mix_b.shape[2]
    hid = w1.shape[1]
    tm = _tile(l, 512, LANE)
    row = lambda bi, i: (bi, i, 0)
    vec = lambda bi, i: (bi, 0, 0)
    a_spec = (pl.BlockSpec((1, c, tm), lambda bi, i: (bi, 0, i)) if a_transposed
              else pl.BlockSpec((1, tm, c), row))
    fg = jnp.ones((1, d), F32) if final_g is None else final_g.reshape(1, d)
    return pl.pallas_call(
        functools.partial(_out_mlp_kernel, a_transposed=a_transposed, hidden_chunk=min(hid, 1024),
                          final_norm=final_g is not None),
        grid=(b, l // tm),
        in_specs=[pl.BlockSpec((1, tm, d), row), a_spec, pl.BlockSpec((1, tm, c), row),
                  _const_spec((2 * c, d)),
                  pl.BlockSpec((1, 1, d), vec), _const_spec((1, d)),
                  pl.BlockSpec((1, 1, d), vec), pl.BlockSpec((1, 1, d), vec), pl.BlockSpec((1, 1, d), vec),
                  _const_spec((d, hid)), _const_spec((hid, d)), _const_spec((1, d))],
        out_specs=pl.BlockSpec((1, tm, d), row),
        out_shape=jax.ShapeDtypeStruct((b, l, d), F32),
        compiler_params=_cparams("parallel", "parallel"),
        name="out_proj_mlp",
    )(x, mix_a, mix_b, w_out, gt1, g2.reshape(1, d), sh2, sc2, gt2, w1, w2, fg)


def _rope_tables(length, scale):
    f32 = np.float32
    rows = length // GRID_W
    row = np.repeat(np.arange(rows, dtype=f32), GRID_W)
    col = np.tile(np.arange(GRID_W, dtype=f32), rows)
    n_freq = DIFF_DQK // 4
    inv = (f32(ROPE_BASE) ** (-np.arange(n_freq, dtype=f32) / f32(n_freq))).astype(f32)
    ang_r = row[:, None] * inv
    ang_c = col[:, None] * inv
    cos = np.concatenate([np.cos(ang_r)] * 2 + [np.cos(ang_c)] * 2, axis=-1)
    sin = np.concatenate([-np.sin(ang_r), np.sin(ang_r), -np.sin(ang_c), np.sin(ang_c)], axis=-1)
    cos = np.concatenate([cos, cos], axis=-1) * f32(scale)
    sin = np.concatenate([sin, sin], axis=-1) * f32(scale)
    return cos.astype(f32), sin.astype(f32)


def _flat_tables(length, scale):
    return np.full((length, LANE), scale, np.float32), np.zeros((length, LANE), np.float32)


def _proj_t_kernel(x_ref, g_ref, sh_ref, sc_ref, wt_ref, o_ref, *, rc):
    h = _norm_mod(x_ref[0], g_ref[...], sh_ref[0], sc_ref[0]).astype(BF16)
    for r0 in range(0, wt_ref.shape[0], rc):
        o_ref[0, r0:r0 + rc, :] = lax.dot_general(wt_ref[r0:r0 + rc, :], h, (((1,), (1,)), ((), ())),
                                                  preferred_element_type=F32)


def norm_mod_proj_t(x, g, sh, sc, wt):
    b, l, d = x.shape
    c = wt.shape[0]
    tl = _tile(l, 512, LANE)
    return pl.pallas_call(
        functools.partial(_proj_t_kernel, rc=_tile(c, 512)),
        grid=(b, l // tl),
        in_specs=[pl.BlockSpec((1, tl, d), lambda bi, i: (bi, i, 0)),
                  _const_spec((1, d)),
                  pl.BlockSpec((1, 1, d), lambda bi, i: (bi, 0, 0)),
                  pl.BlockSpec((1, 1, d), lambda bi, i: (bi, 0, 0)),
                  _const_spec((c, d))],
        out_specs=pl.BlockSpec((1, c, tl), lambda bi, i: (bi, 0, i)),
        out_shape=jax.ShapeDtypeStruct((b, c, l), F32),
        compiler_params=_cparams("parallel", "parallel"),
        name="norm_mod_proj_t",
    )(x, g.reshape(1, d), sh, sc, wt)


def _dot_hi(a, b):
    return jnp.dot(a, b, preferred_element_type=F32, precision=lax.Precision.HIGHEST)


def _hy_hidden_kernel(z_ref, w1_ref, b1_ref, f1_ref, w2_ref, b2_ref, f2_ref, o_ref):
    hid = jnp.sin(f1_ref[...] * (_dot_hi(w1_ref[...], z_ref[...]) + b1_ref[...]))
    o_ref[...] = jnp.sin(f2_ref[...] * (_dot_hi(w2_ref[...], hid) + b2_ref[...]))


def hyena_hidden(zt, w1t, b1, f1, w2t, b2, f2):
    e, n = zt.shape
    f = w1t.shape[0]
    tn = _tile(n, 2048, LANE)
    col = lambda v: v.reshape(f, 1)
    return pl.pallas_call(
        _hy_hidden_kernel,
        grid=(n // tn,),
        in_specs=[pl.BlockSpec((e, tn), lambda j: (0, j)),
                  _const_spec((f, e)), _const_spec((f, 1)), _const_spec((f, 1)),
                  _const_spec((f, f)), _const_spec((f, 1)), _const_spec((f, 1))],
        out_specs=pl.BlockSpec((f, tn), lambda j: (0, j)),
        out_shape=jax.ShapeDtypeStruct((f, n), F32),
        compiler_params=_cparams("parallel"),
        name="hyena_hidden",
    )(zt, w1t, col(b1), col(f1), w2t, col(b2), col(f2))


def _hy_filter_kernel(hid_ref, t_ref, wf_ref, wb_ref, dl_ref, o_ref, *, length, tn):
    n2 = hid_ref.shape[1]
    delta = dl_ref[...]

    def piece(c0):
        hid = hid_ref[:, c0:c0 + tn]
        dec = jnp.exp(-t_ref[:, c0:c0 + tn] * delta)
        return _dot_hi(wf_ref[...], hid) * dec, _dot_hi(wb_ref[...], hid) * dec

    norm = jnp.zeros((wf_ref.shape[0], 1), F32)
    for c0 in range(length, n2, tn):
        ff, fb = piece(c0)
        norm = norm + jnp.sum(jnp.abs(ff) + jnp.abs(fb), axis=-1, keepdims=True)
    inv = 1.0 / norm
    for c0 in range(0, n2, tn):
        ff, fb = piece(c0)
        if c0 >= length:
            o_ref[0, :, c0:c0 + tn] = ff * inv
        else:
            col = lax.broadcasted_iota(jnp.int32, fb.shape, 1) + c0
            o_ref[0, :, c0:c0 + tn] = jnp.where(col == 0, 0.0, fb * inv)


def hyena_filters(hid, t_row, w3t, deltas, length):
    f, n2 = hid.shape
    ch = deltas.shape[0]
    cb = LANE
    nblk = ch // cb
    tn = _tile(length, 2048, LANE)
    return pl.pallas_call(
        functools.partial(_hy_filter_kernel, length=length, tn=tn),
        grid=(HY_ORDER, nblk),
        in_specs=[_const_spec((f, n2)), _const_spec((1, n2)),
                  pl.BlockSpec((cb, f), lambda o, c: (o * 2 * nblk + c, 0)),
                  pl.BlockSpec((cb, f), lambda o, c: (o * 2 * nblk + nblk + c, 0)),
                  pl.BlockSpec((cb, 1), lambda o, c: (c, 0))],
        out_specs=pl.BlockSpec((1, cb, n2), lambda o, c: (o, c, 0)),
        out_shape=jax.ShapeDtypeStruct((HY_ORDER, ch, n2), F32),
        compiler_params=_cparams("parallel", "parallel"),
        name="hyena_filters",
    )(hid, t_row, w3t, w3t, deltas.reshape(ch, 1))


HY_T = 256
HY_CB = SUBLANE

def _hyena_kernel(v_ref, x1_ref, x2_ref, g_ref, sw_ref, sb_ref, sk_ref_s, o_ref, sk_ref, u_ref, acc_ref,
                  *, length, nch):
    nb = v_ref.shape[0]
    nj = length // HY_T
    cblk = pl.program_id(0)
    lane = lax.broadcasted_iota(jnp.int32, (nb, length), 1)

    def short(ref, cc, part):
        x = ref[:, cc, :]
        ch = part * nch + cblk * HY_CB + cc
        prev = jnp.where(lane == 0, 0.0, pltpu.roll(x, 1, 1))
        nxt = jnp.where(lane == length - 1, 0.0, pltpu.roll(x, length - 1, 1))
        return sw_ref[0, ch] * prev + sw_ref[1, ch] * x + sw_ref[2, ch] * nxt + sb_ref[ch]

    def long_conv(u, order, cc):
        g = jnp.broadcast_to(g_ref[order, pl.ds(cc, 1), :], (2 * SUBLANE, 2 * length))
        half = HY_T // 2
        for r0 in range(0, half, 2 * SUBLANE):
            pair = pltpu.roll(g, r0, 1, stride=1, stride_axis=0).astype(BF16)
            sk_ref[r0:r0 + 2 * SUBLANE, :] = pair
            sk_ref[half + r0:half + r0 + 2 * SUBLANE, half:] = pair[:, :2 * length - half]
        for j in range(nj):
            u_ref[j * nb:(j + 1) * nb, :] = u[:, j * HY_T:(j + 1) * HY_T]
        acc_ref[...] = jnp.zeros(acc_ref.shape, F32)
        for d in range(-(nj - 1), nj):
            m = (nj - abs(d)) * nb
            lo_in, lo_out = max(0, -d) * nb, max(0, d) * nb
            tile = sk_ref[:, length + d * HY_T:length + (d + 1) * HY_T]
            acc_ref[lo_out:lo_out + m, :] += jnp.dot(u_ref[lo_in:lo_in + m, :].astype(BF16), tile,
                                                     preferred_element_type=F32)
        return jnp.concatenate([acc_ref[i * nb:(i + 1) * nb, :] for i in range(nj)], axis=1)

    def channel(cc, carry):
        v = short(v_ref, cc, 0)
        x1 = short(x1_ref, cc, 1)
        x2 = short(x2_ref, cc, 2)
        ch = cblk * HY_CB + cc
        z = x1 * (long_conv(v, 0, cc) + sk_ref_s[0, ch] * v)
        z = x2 * (long_conv(z, 1, cc) + sk_ref_s[1, ch] * z)
        o_ref[:, cc, :] = z
        return carry

    lax.fori_loop(0, HY_CB, channel, 0)


def hyena_mix(pt, filt, short_w, short_b, skip):
    b, c3, l = pt.shape
    nch = c3 // 3
    assert l % HY_T == 0 and nch % HY_CB == 0 and b == SUBLANE
    nblk = nch // HY_CB
    smem = pl.BlockSpec(memory_space=pltpu.SMEM)
    part = lambda k: pl.BlockSpec((b, HY_CB, l), lambda c: (0, k * nblk + c, 0))
    return pl.pallas_call(
        functools.partial(_hyena_kernel, length=l, nch=nch),
        grid=(nblk,),
        in_specs=[part(0), part(1), part(2),
                  pl.BlockSpec((HY_ORDER, HY_CB, 2 * l), lambda c: (0, c, 0)),
                  smem, smem, smem],
        out_specs=pl.BlockSpec((b, HY_CB, l), lambda c: (0, c, 0)),
        out_shape=jax.ShapeDtypeStruct((b, nch, l), F32),
        scratch_shapes=[pltpu.VMEM((HY_T, 2 * l), BF16), pltpu.VMEM((l // HY_T * b, HY_T), F32),
                        pltpu.VMEM((l // HY_T * b, HY_T), F32)],
        compiler_params=_cparams("parallel"),
        name="hyena_mix",
    )(pt, pt, pt, filt, short_w, short_b, skip)


DN_HALO = SUBLANE
DN_BETA_F, DN_BETA_B, DN_G_F, DN_G_B = 0, DN_HEADS, 2 * DN_HEADS, 3 * DN_HEADS


def _dn_feat_kernel(c_ref, p_ref, n_ref, s_ref, w_ref, al_ref, dt_ref, q_ref, k_ref, v_ref, f_ref, u_ref, *, tl):
    i = pl.program_id(1)
    last = pl.num_programs(1) - 1
    left = (DN_CONV_W - 1) // 2
    nc = c_ref.shape[2]
    u_ref[DN_HALO:DN_HALO + tl, :] = c_ref[0]
    u_ref[0:DN_HALO, :] = jnp.where(i > 0, p_ref[0], 0.0)
    u_ref[DN_HALO + tl:DN_HALO + tl + DN_HALO, :] = jnp.where(i < last, n_ref[0], 0.0)
    rows = min(tl, 128)
    outs = (q_ref, k_ref, v_ref)
    per = nc // len(outs)
    for r0 in range(0, tl, rows):
        for c0 in range(0, nc, LANE):
            acc = jnp.zeros((rows, LANE), F32)
            for k in range(DN_CONV_W):
                s = DN_HALO - left + k + r0
                acc = acc + w_ref[k:k + 1, c0:c0 + LANE] * u_ref[s:s + rows, c0:c0 + LANE]
            y = _silu(acc)
            which, off = divmod(c0, per)
            if which < 2:
                y = y * lax.rsqrt(jnp.sum(y * y, axis=-1, keepdims=True) + 1e-6)
            outs[which][0, r0:r0 + rows, off:off + LANE] = y
    s = s_ref[0]
    lane = lax.broadcasted_iota(jnp.int32, s.shape, 1)
    gate = -jnp.exp(al_ref[...]) * jax.nn.softplus(s + dt_ref[...])
    f_ref[0] = jnp.where(lane < DN_G_F, jax.nn.sigmoid(s), gate)


def dn_features(qkv, small, conv_w, alog_row, dtb_row):
    b, l, nc = qkv.shape
    tl = _tile(l, 256, DN_HALO)
    nh = tl // DN_HALO
    nblk = l // DN_HALO
    w = nc // 3
    row = lambda bi, i: (bi, i, 0)
    return pl.pallas_call(
        functools.partial(_dn_feat_kernel, tl=tl),
        grid=(b, l // tl),
        in_specs=[pl.BlockSpec((1, tl, nc), row),
                  pl.BlockSpec((1, DN_HALO, nc), lambda bi, i: (bi, jnp.maximum(i * nh - 1, 0), 0)),
                  pl.BlockSpec((1, DN_HALO, nc), lambda bi, i: (bi, jnp.minimum((i + 1) * nh, nblk - 1), 0)),
                  pl.BlockSpec((1, tl, LANE), row),
                  _const_spec((DN_CONV_W, nc)), _const_spec((1, LANE)), _const_spec((1, LANE))],
        out_specs=[pl.BlockSpec((1, tl, w), row)] * 3 + [pl.BlockSpec((1, tl, LANE), row)],
        out_shape=[jax.ShapeDtypeStruct((b, l, w), F32)] * 3 + [jax.ShapeDtypeStruct((b, l, LANE), F32)],
        scratch_shapes=[pltpu.VMEM((tl + 2 * DN_HALO, nc), F32)],
        compiler_params=_cparams("parallel", "parallel"),
        name="dn_features",
    )(qkv, qkv, qkv, small, conv_w, alog_row, dtb_row)


def _bdot(a, b):
    return jnp.dot(a.astype(BF16), b.astype(BF16), preferred_element_type=F32)


def _bdot_nt(a, b):
    return lax.dot_general(a.astype(BF16), b.astype(BF16), (((1,), (1,)), ((), ())), preferred_element_type=F32)


def _bdot_tn(a, b):
    return lax.dot_general(a.astype(BF16), b.astype(BF16), (((0,), (0,)), ((), ())), preferred_element_type=F32)


def _cumsum_rows(x, reverse):
    n = x.shape[0]
    row = lax.broadcasted_iota(jnp.int32, x.shape, 0)
    s = 1
    while s < n:
        if reverse:
            x = x + jnp.where(row < n - s, pltpu.roll(x, n - s, 0), 0.0)
        else:
            x = x + jnp.where(row >= s, pltpu.roll(x, s, 0), 0.0)
        s *= 2
    return x


def _dn_chunks(chains):
    c = chains[0]["k"].shape[0]
    ri = lax.broadcasted_iota(jnp.int32, (c, c), 0)
    ci = lax.broadcasted_iota(jnp.int32, (c, c), 1)
    work = []
    for ch in chains:
        rev = ch["reverse"]
        incl = (ri <= ci) if rev else (ri >= ci)
        strict = (ri < ci) if rev else (ri > ci)
        decay = jnp.where(incl, jnp.exp(jnp.where(incl, ch["gcol"] - ch["grow"], 0.0)), 0.0)
        kb = ch["k"] * ch["beta"]
        qs = ch["q"] * (ch["k"].shape[1] ** -0.5)
        work.append(dict(ch, strict=strict, decay=decay, kb=kb, qs=qs))
    grams = [_bdot_nt(jnp.concatenate([w["kb"], w["qs"]], axis=0), w["k"]) for w in work]
    lowers = [jnp.where(w["strict"], g[:c] * w["decay"], 0.0) for w, g in zip(work, grams)]
    same2 = (ri // 2) == (ci // 2)
    ymats = [-jnp.where(same2, lo, 0.0) for lo in lowers]
    m = 2
    while m < c:
        pair = ((ri // (2 * m)) == (ci // (2 * m))) & ((ri // m) != (ci // m))
        offs = [jnp.where(pair, lo, 0.0) for lo in lowers]
        zs = [off + _bdot(ym, off) for ym, off in zip(ymats, offs)]
        ymats = [ym - z - _bdot(z, ym) for ym, z in zip(ymats, zs)]
        m *= 2
    outs = []
    uws, e_gs = [], []
    for w, ym in zip(work, ymats):
        e_g = jnp.exp(w["gcol"])
        rhs = jnp.concatenate([w["v"] * w["beta"], w["kb"] * e_g], axis=1)
        uws.append(rhs + _bdot(ym, rhs))
        e_gs.append(e_g)
    dv = work[0]["v"].shape[1]
    wss = [_bdot(jnp.concatenate([uw[:, dv:], w["qs"] * e_g], axis=0), w["state"])
           for w, uw, e_g in zip(work, uws, e_gs)]
    v_news = [uw[:, :dv] - ws[:c] for uw, ws in zip(uws, wss)]
    for w, g, ws, v_new in zip(work, grams, wss, v_news):
        gcol = w["gcol"]
        g_last = gcol[0:1, :] if w["reverse"] else gcol[c - 1:c, :]
        k_dec = w["k"] * jnp.exp(g_last - gcol)
        o = ws[c:] + _bdot(g[c:] * w["decay"], v_new)
        new_state = w["state"] * jnp.exp(g_last) + _bdot_tn(k_dec, v_new)
        outs.append((o, new_state))
    return outs


def _dn_scan_kernel(qf_ref, kf_ref, vf_ref, ff_ref, qb_ref, kb_ref, vb_ref, fb_ref, s0f_ref, s0b_ref,
                    of_ref, ob_ref, sf_ref, sb_ref, st_ref):
    n = pl.program_id(1)

    @pl.when(n == 0)
    def _():
        st_ref[0] = s0f_ref[0]
        st_ref[1] = s0b_ref[0]

    dirs = ((qf_ref, kf_ref, vf_ref, ff_ref, of_ref, False), (qb_ref, kb_ref, vb_ref, fb_ref, ob_ref, True))
    chains, dests = [], []
    for di, (q_ref, k_ref, v_ref, f_ref, o_ref, reverse) in enumerate(dirs):
        feats = f_ref[0]
        gc = _cumsum_rows(feats, reverse)
        gct = gc.T
        for h in range(DN_HEADS):
            cols = slice(h * DN_DK, (h + 1) * DN_DK)
            bl = (DN_BETA_B if reverse else DN_BETA_F) + h
            gl = (DN_G_B if reverse else DN_G_F) + h
            chains.append(dict(q=q_ref[0, :, cols], k=k_ref[0, :, cols], v=v_ref[0, :, cols],
                               beta=feats[:, bl:bl + 1], gcol=gc[:, gl:gl + 1], grow=gct[gl:gl + 1, :],
                               state=st_ref[di, h], reverse=reverse))
            dests.append((o_ref, cols, di, h))
    for (o_ref, cols, di, h), (o, s_new) in zip(dests, _dn_chunks(chains)):
        o_ref[0, :, cols] = o
        st_ref[di, h] = s_new

    @pl.when(n == pl.num_programs(1) - 1)
    def _():
        sf_ref[0] = st_ref[0]
        sb_ref[0] = st_ref[1]


def dn_scan(q, k, v, feats, s0f, s0b):
    b, l, w = q.shape
    nchunk = l // DN_CHUNK
    fwd = lambda bi, n: (bi, n, 0)
    bwd = lambda bi, n: (bi, nchunk - 1 - n, 0)
    st = lambda bi, n: (bi, 0, 0, 0)
    blk = lambda im, width: pl.BlockSpec((1, DN_CHUNK, width), im)
    st_spec = pl.BlockSpec((1, DN_HEADS, DN_DK, DN_DK), st)
    st_shape = jax.ShapeDtypeStruct((b, DN_HEADS, DN_DK, DN_DK), F32)
    return pl.pallas_call(
        _dn_scan_kernel,
        grid=(b, nchunk),
        in_specs=[blk(fwd, w), blk(fwd, w), blk(fwd, w), blk(fwd, LANE),
                  blk(bwd, w), blk(bwd, w), blk(bwd, w), blk(bwd, LANE), st_spec, st_spec],
        out_specs=[blk(fwd, w), blk(bwd, w), st_spec, st_spec],
        out_shape=[jax.ShapeDtypeStruct((b, l, w), F32)] * 2 + [st_shape, st_shape],
        scratch_shapes=[pltpu.VMEM((2, DN_HEADS, DN_DK, DN_DK), F32)],
        compiler_params=_cparams("parallel", "arbitrary"),
        name="dn_scan",
    )(q, k, v, feats, q, k, v, feats, s0f, s0b)


def _dn_out_kernel(of_ref, ob_ref, gate_ref, g_ref, o_ref):
    o = of_ref[0] + ob_ref[0]
    for c0 in range(0, o.shape[1], DN_DK):
        t = o[:, c0:c0 + DN_DK]
        t = t * lax.rsqrt(jnp.mean(t * t, axis=-1, keepdims=True) + EPS) * g_ref[...]
        o_ref[0, :, c0:c0 + DN_DK] = (t * _silu(gate_ref[0, :, c0:c0 + DN_DK])).astype(o_ref.dtype)


def dn_output(o_f, o_b, gate, norm_g):
    b, l, w = o_f.shape
    tl = _tile(l, 512)
    row = lambda bi, i: (bi, i, 0)
    return pl.pallas_call(
        _dn_out_kernel,
        grid=(b, l // tl),
        in_specs=[pl.BlockSpec((1, tl, w), row)] * 3 + [_const_spec((1, DN_DK))],
        out_specs=pl.BlockSpec((1, tl, w), row),
        out_shape=jax.ShapeDtypeStruct((b, l, w), BF16),
        compiler_params=_cparams("parallel", "parallel"),
        name="dn_output",
    )(o_f, o_b, gate, norm_g.reshape(1, DN_DK))


def _split_mods(mods, b, d):
    lat = [mods[:b, k * d:(k + 1) * d].reshape(b, 1, d) for k in range(N_MOD)]
    ctx = [jnp.broadcast_to(mods[b, k * d:(k + 1) * d].reshape(1, 1, d), (b, 1, d)) for k in range(N_MOD)]
    return lat, ctx


def _even_layer(x, ctx, lat_m, ctx_m, n1g, n2g, w_in, conv_w, conv_b, ln_g, ln_b, lam_vecs, subln_g, w_out,
                w1, w2, lambda_init):
    b, l, d = x.shape
    lc = ctx.shape[1]
    ch = conv_w.shape[1]
    qk = DIFF_HEADS * 2 * DIFF_DQK
    q0, k0, v0, n_in = 2 * ch, 2 * ch + qk, 2 * ch + 2 * qk, 2 * ch + 2 * qk + DIFF_HEADS * DIFF_DV
    scale = DIFF_DQK ** -0.5 * math.log2(math.e)
    splits = [(0, q0), (q0, n_in)]
    w_in = w_in.astype(BF16)
    rope_lat = _rope_tables(l, scale) + _rope_tables(l, 1.0)
    rope_ctx = _flat_tables(lc, scale) + _flat_tables(lc, 1.0)
    pc_lat, qkv_lat = norm_mod_proj(x, n1g, lat_m[0], lat_m[1], w_in, splits, (F32, BF16), rope_lat, (q0, k0, v0))
    pc_ctx, qkv_ctx = norm_mod_proj(ctx, n1g, ctx_m[0], ctx_m[1], w_in, splits, (F32, BF16), rope_ctx, (q0, k0, v0))
    conv_lat = conformer_conv(pc_lat, conv_w, conv_b, ln_g, ln_b)
    conv_ctx = conformer_conv(pc_ctx, conv_w, conv_b, ln_g, ln_b)
    k_all = jnp.concatenate([qkv_lat[:, :, qk:2 * qk], qkv_ctx[:, :, qk:2 * qk]], axis=1)
    def heads_t(v):
        vt = jnp.swapaxes(v, 1, 2).reshape(b, DIFF_HEADS, DIFF_DV, v.shape[1])
        return jnp.concatenate([vt, jnp.ones((b, DIFF_HEADS, ATTN_ONES, v.shape[1]), vt.dtype)], axis=2)

    vt_ctx = heads_t(qkv_ctx[:, :, 2 * qk:])
    vt_all = jnp.concatenate([heads_t(qkv_lat[:, :, 2 * qk:]), vt_ctx], axis=3)
    qt_lat = jnp.swapaxes(qkv_lat[:, :, :qk], 1, 2)
    qt_ctx = jnp.swapaxes(qkv_ctx[:, :, :qk], 1, 2)
    o_lat = diff_attention(qt_lat, k_all, vt_all, lam_vecs, subln_g, lambda_init, 0)
    o_ctx = diff_attention(qt_ctx, qkv_ctx, vt_ctx, lam_vecs, subln_g, lambda_init, qk // LANE)
    w_out, w1, w2 = w_out.astype(BF16), w1.astype(BF16), w2.astype(BF16)
    x = out_proj_mlp(x, conv_lat, o_lat, w_out, lat_m[2], n2g, lat_m[3], lat_m[4], lat_m[5], w1, w2)
    ctx = out_proj_mlp(ctx, conv_ctx, o_ctx, w_out, ctx_m[2], n2g, ctx_m[3], ctx_m[4], ctx_m[5], w1, w2)
    return x, ctx


def _hyena_position_tables(length):
    f32 = np.float32
    t = np.linspace(0.0, 1.0, length, dtype=f32)[:, None]
    bands = (HY_EMB - 1) // 2
    omega = (f32(2.0 * math.pi) * np.arange(length, dtype=f32)[:, None] / f32(length)).astype(f32)
    ang = omega * np.linspace(1e-4, bands - 1, bands, dtype=f32)
    z = np.concatenate([t, np.cos(ang), -np.sin(ang)], axis=-1).astype(f32)
    pos = np.minimum(np.abs(np.arange(2 * length) - length), length - 1)
    pad = -HY_EMB % SUBLANE
    zt = np.pad(z[pos].T, ((0, pad), (0, 0)))
    return np.ascontiguousarray(zt), np.ascontiguousarray(t[pos].T)


def _odd_layer_last(x, ctx, lat_m, ctx_m, n1g, n2g, w_in, hy_short_w, hy_short_b, hy_w1, hy_b1, hy_f1, hy_w2, hy_b2,
                    hy_f2, hy_w3, hy_skip, dn_conv_w, alog_f, alog_b, dtb_f, dtb_b, dn_norm_g, w_out, w1, w2,
                    final_g):
    b, l, d = x.shape
    hy_in = hy_short_w.shape[1]
    hy_ch = hy_in // (HY_ORDER + 1)
    dn_w = DN_HEADS * DN_DK
    n_rest = w_in.shape[1] - hy_in
    n_pad = -n_rest % LANE
    w_rest = jnp.pad(w_in[:, hy_in:], ((0, 0), (0, n_pad))).astype(BF16)
    splits = [(0, dn_w), (dn_w, 4 * dn_w), (4 * dn_w, n_rest + n_pad)]
    f32x3 = (F32, F32, F32)
    gate_lat, qkv_lat, small_lat = norm_mod_proj(x, n1g, lat_m[0], lat_m[1], w_rest, splits, f32x3)
    _, qkv_ctx, small_ctx = norm_mod_proj(ctx, n1g, ctx_m[0], ctx_m[1], w_rest, splits, f32x3)

    pt = norm_mod_proj_t(x, n1g, lat_m[0], lat_m[1], w_in[:, :hy_in].T.astype(BF16))
    zt, t_row = _hyena_position_tables(l)
    w1t = jnp.pad(hy_w1.T, ((0, 0), (0, zt.shape[0] - hy_w1.shape[0])))
    hid = hyena_hidden(zt, w1t, hy_b1, hy_f1, hy_w2.T, hy_b2, hy_f2)
    deltas = jnp.abs(jnp.linspace(HY_MIN_DECAY, HY_MAX_DECAY, hy_ch, dtype=F32))
    filt = hyena_filters(hid, t_row, hy_w3.T, deltas, l)
    hy = hyena_mix(pt, filt, hy_short_w, hy_short_b, hy_skip)

    lanes = jnp.zeros((LANE,), F32)
    alog_row = lanes.at[DN_G_F:DN_G_F + DN_HEADS].set(alog_f).at[DN_G_B:DN_G_B + DN_HEADS].set(alog_b).reshape(1, LANE)
    dtb_row = lanes.at[DN_G_F:DN_G_F + DN_HEADS].set(dtb_f).at[DN_G_B:DN_G_B + DN_HEADS].set(dtb_b).reshape(1, LANE)
    q_c, k_c, v_c, f_c = dn_features(qkv_ctx, small_ctx, dn_conv_w, alog_row, dtb_row)
    q_l, k_l, v_l, f_l = dn_features(qkv_lat, small_lat, dn_conv_w, alog_row, dtb_row)
    s0 = jnp.zeros((b, DN_HEADS, DN_DK, DN_DK), F32)
    _, _, s_cf, s_cb = dn_scan(q_c, k_c, v_c, f_c, s0, s0)
    o_f, o_b, _, _ = dn_scan(q_l, k_l, v_l, f_l, s_cf, s_cb)
    dn = dn_output(o_f, o_b, gate_lat, dn_norm_g)

    return out_proj_mlp(x, hy, dn, w_out.astype(BF16), lat_m[2], n2g, lat_m[3], lat_m[4], lat_m[5],
                        w1.astype(BF16), w2.astype(BF16), final_g=final_g, a_transposed=True)


def kernel(x, c, ctx, c_ctx, ada_w, ada_b, norm1_g, norm2_g, mlp_w1, mlp_w2, ev_w_in, ev_conv_w, ev_conv_b, ev_ln_g,
           ev_ln_b, ev_lq1, ev_lk1, ev_lq2, ev_lk2, ev_subln_g, ev_w_out, od_w_in, od_hy_short_w, od_hy_short_b,
           od_hy_w1, od_hy_b1, od_hy_freq1, od_hy_w2, od_hy_b2, od_hy_freq2, od_hy_w3, od_hy_skip, od_dn_conv_w,
           od_dn_alog_f, od_dn_alog_b, od_dn_dtb_f, od_dn_dtb_b, od_dn_norm_g, od_w_out, final_g):
    b, _, d = x.shape
    assert ada_w.shape[0] == 2, "layer 0 = conformer/diff-attention, layer 1 (last) = Hyena/DeltaNet"
    rows = -(-(b + 1) // SUBLANE) * SUBLANE
    cvec = jnp.zeros((rows, d), F32).at[:b].set(c).at[b].set(c_ctx)

    lat_m, ctx_m = _split_mods(ada_mods(cvec, ada_w[0], ada_b[0]), b, d)
    lam_vecs = jnp.stack([ev_lq1[0], ev_lk1[0], ev_lq2[0], ev_lk2[0]])
    x, ctx = _even_layer(x, ctx, lat_m, ctx_m, norm1_g[0], norm2_g[0], ev_w_in[0], ev_conv_w[0], ev_conv_b[0],
                         ev_ln_g[0], ev_ln_b[0], lam_vecs, ev_subln_g[0], ev_w_out[0], mlp_w1[0], mlp_w2[0],
                         0.8 - 0.6 * math.exp(-0.3 * 0))

    lat_m, ctx_m = _split_mods(ada_mods(cvec, ada_w[1], ada_b[1]), b, d)
    return _odd_layer_last(x, ctx, lat_m, ctx_m, norm1_g[1], norm2_g[1], od_w_in[0], od_hy_short_w[0],
                           od_hy_short_b[0], od_hy_w1[0], od_hy_b1[0], od_hy_freq1[0], od_hy_w2[0], od_hy_b2[0],
                           od_hy_freq2[0], od_hy_w3[0], od_hy_skip[0], od_dn_conv_w[0], od_dn_alog_f[0],
                           od_dn_alog_b[0], od_dn_dtb_f[0], od_dn_dtb_b[0], od_dn_norm_g[0], od_w_out[0],
                           mlp_w1[1], mlp_w2[1], final_g)
```

```python
import functools
import math

import jax
import jax.numpy as jnp
import numpy as np
from jax import lax
from jax.experimental import pallas as pl
from jax.experimental.pallas import tpu as pltpu

F32 = jnp.float32
BF16 = jnp.bfloat16

EPS = 1e-6
N_MOD = 6
GRID_W = 64
ROPE_BASE = 10000.0
CONV_W = 31
DIFF_HEADS = 4
DIFF_DQK = 64
DIFF_DV = 128
HY_ORDER = 2
HY_EMB = 33
HY_FFN = 64
HY_MAX_DECAY = math.log(1e-2) / 0.3
HY_MIN_DECAY = math.log(1e-2) / 1.5
DN_HEADS = 4
DN_DK = 128
DN_CONV_W = 5
DN_CHUNK = 128

LANE = 128
SUBLANE = 8
VMEM_LIMIT = 56 * 1024 * 1024


def _cparams(*sem):
    return pltpu.CompilerParams(dimension_semantics=sem, vmem_limit_bytes=VMEM_LIMIT)


def _tile(n, pref, mult=SUBLANE):
    if n <= pref:
        return n
    t = (pref // mult) * mult
    while t > mult and n % t:
        t -= mult
    assert n % t == 0, (n, pref, mult)
    return t


def _const_spec(shape):
    nd = len(shape)
    return pl.BlockSpec(shape, lambda *_: (0,) * nd, pipeline_mode=pl.Buffered(1))


def _silu(x):
    return x * jax.nn.sigmoid(x)


def _norm_mod(x, g, sh, sc):
    y = x * lax.rsqrt(jnp.mean(x * x, axis=-1, keepdims=True) + EPS)
    return (y * g) * (1.0 + sc) + sh


def _mods_kernel(c_ref, w_ref, b_ref, o_ref):
    cond = _silu(c_ref[...])
    o_ref[...] = jnp.dot(cond, w_ref[...], preferred_element_type=F32,
                         precision=lax.Precision.HIGHEST) + b_ref[...]


def ada_mods(cvec, w, b):
    r, d = cvec.shape
    n = w.shape[1]
    tn = _tile(n, 512, LANE)
    return pl.pallas_call(
        _mods_kernel,
        grid=(n // tn,),
        in_specs=[pl.BlockSpec((r, d), lambda j: (0, 0)),
                  pl.BlockSpec((d, tn), lambda j: (0, j)),
                  pl.BlockSpec((1, tn), lambda j: (0, j))],
        out_specs=pl.BlockSpec((r, tn), lambda j: (0, j)),
        out_shape=jax.ShapeDtypeStruct((r, n), F32),
        compiler_params=_cparams("arbitrary"),
        name="ada_mods",
    )(cvec, w, b.reshape(1, n))


def _swap16(y):
    n = y.shape[-1]
    lane = lax.broadcasted_iota(jnp.int32, y.shape, y.ndim - 1)
    fwd = pltpu.roll(y, n - 16, y.ndim - 1)
    bwd = pltpu.roll(y, 16, y.ndim - 1)
    return jnp.where((lane % 32) < 16, fwd, bwd)


PROJ_CHUNK = 512


def _proj_kernel(x_ref, g_ref, sh_ref, sc_ref, w_ref, *rest, splits, rope_cols):
    n_out = len(splits)
    if rope_cols:
        cq_ref, sq_ref, ck_ref, sk_ref = rest[:4]
        rest = rest[4:]
    o_refs = rest[:n_out]
    h = _norm_mod(x_ref[0], g_ref[...], sh_ref[0], sc_ref[0]).astype(BF16)
    for o_ref, (s0, s1) in zip(o_refs, splits):
        tc = PROJ_CHUNK if (s1 - s0) % PROJ_CHUNK == 0 and s0 % PROJ_CHUNK == 0 else LANE
        for c0 in range(s0, s1, tc):
            y = jnp.dot(h, w_ref[:, c0:c0 + tc], preferred_element_type=F32)
            if rope_cols and rope_cols[0] <= c0 < rope_cols[2]:
                is_q = c0 < rope_cols[1]
                cos = (cq_ref if is_q else ck_ref)[...]
                sin = (sq_ref if is_q else sk_ref)[...]
                reps = tc // LANE
                cos = jnp.concatenate([cos] * reps, axis=1)
                sin = jnp.concatenate([sin] * reps, axis=1)
                y = y * cos + _swap16(y) * sin
            o_ref[0, :, c0 - s0:c0 - s0 + tc] = y.astype(o_ref.dtype)


def norm_mod_proj(x, g, sh, sc, w, splits, dtypes, rope=None, rope_cols=None):
    b, l, d = x.shape
    n = w.shape[1]
    tm = _tile(l, 512)
    assert all(s % LANE == 0 for sp in splits for s in sp)
    if rope_cols:
        assert all(c % PROJ_CHUNK == 0 for c in rope_cols)
    row = lambda bi, i: (bi, i, 0)
    in_specs = [pl.BlockSpec((1, tm, d), row),
                _const_spec((1, d)),
                pl.BlockSpec((1, 1, d), lambda bi, i: (bi, 0, 0)),
                pl.BlockSpec((1, 1, d), lambda bi, i: (bi, 0, 0)),
                _const_spec((d, n))]
    args = [x, g.reshape(1, d), sh, sc, w]
    if rope_cols:
        in_specs += [pl.BlockSpec((tm, LANE), lambda bi, i: (i, 0))] * 4
        args += list(rope)
    return pl.pallas_call(
        functools.partial(_proj_kernel, splits=tuple(splits), rope_cols=rope_cols),
        grid=(b, l // tm),
        in_specs=in_specs,
        out_specs=[pl.BlockSpec((1, tm, s1 - s0), row) for s0, s1 in splits],
        out_shape=[jax.ShapeDtypeStruct((b, l, s1 - s0), dt) for (s0, s1), dt in zip(splits, dtypes)],
        compiler_params=_cparams("parallel", "parallel"),
        name="norm_mod_proj",
    )(*args)


CONV_HALO = 16


def _conformer_kernel(ac_ref, gc_ref, ap_ref, gp_ref, an_ref, gn_ref, w_ref, b_ref, lg_ref, lb_ref, o_ref,
                      u_ref, sh_ref, y_ref, *, tl, ch):
    i = pl.program_id(1)
    last = pl.num_programs(1) - 1
    left = (CONV_W - 1) // 2
    glu = lambda a, g: a * jax.nn.sigmoid(g)
    u_ref[CONV_HALO:CONV_HALO + tl, :] = glu(ac_ref[0], gc_ref[0])
    prev = glu(ap_ref[0], gp_ref[0])
    nxt = glu(an_ref[0], gn_ref[0])
    u_ref[0:CONV_HALO, :] = jnp.where(i > 0, prev, 0.0)
    u_ref[CONV_HALO + tl:CONV_HALO + tl + CONV_HALO, :] = jnp.where(i < last, nxt, 0.0)
    n = sh_ref.shape[1]
    for p in range(SUBLANE):
        sh_ref[p] = u_ref[p:p + n, :]
    rows = min(tl, 128)
    for r0 in range(0, tl, rows):
        for c0 in range(0, ch, LANE):
            acc = jnp.zeros((rows, LANE), F32) + b_ref[:, c0:c0 + LANE]
            for k in range(CONV_W):
                s = CONV_HALO - left + k + r0
                p, base = s % SUBLANE, s - s % SUBLANE
                acc = acc + w_ref[k:k + 1, c0:c0 + LANE] * sh_ref[p, base:base + rows, c0:c0 + LANE]
            y_ref[r0:r0 + rows, c0:c0 + LANE] = acc
    y = y_ref[...]
    mu = jnp.mean(y, axis=-1, keepdims=True)
    yc = y - mu
    var = jnp.mean(yc * yc, axis=-1, keepdims=True)
    z = yc * lax.rsqrt(var + EPS) * lg_ref[...] + lb_ref[...]
    o_ref[0] = _silu(z).astype(o_ref.dtype)


def conformer_conv(p, conv_w, conv_b, ln_g, ln_b):
    b, l, _ = p.shape
    ch = conv_w.shape[1]
    tl = _tile(l, 256, CONV_HALO)
    nh = tl // CONV_HALO
    nblk = l // CONV_HALO
    cur = lambda col: pl.BlockSpec((1, tl, ch), lambda bi, i: (bi, i, col))
    prv = lambda col: pl.BlockSpec((1, CONV_HALO, ch), lambda bi, i: (bi, jnp.maximum(i * nh - 1, 0), col))
    nxt = lambda col: pl.BlockSpec((1, CONV_HALO, ch), lambda bi, i: (bi, jnp.minimum((i + 1) * nh, nblk - 1), col))
    return pl.pallas_call(
        functools.partial(_conformer_kernel, tl=tl, ch=ch),
        grid=(b, l // tl),
        in_specs=[cur(0), cur(1), prv(0), prv(1), nxt(0), nxt(1),
                  _const_spec((CONV_W, ch)), _const_spec((1, ch)), _const_spec((1, ch)), _const_spec((1, ch))],
        out_specs=pl.BlockSpec((1, tl, ch), lambda bi, i: (bi, i, 0)),
        out_shape=jax.ShapeDtypeStruct((b, l, ch), BF16),
        scratch_shapes=[pltpu.VMEM((tl + 2 * CONV_HALO, ch), F32),
                        pltpu.VMEM((SUBLANE, tl + 2 * CONV_HALO - SUBLANE, ch), F32),
                        pltpu.VMEM((tl, ch), F32)],
        compiler_params=_cparams("parallel", "parallel"),
        name="conformer_conv",
    )(p, p, p, p, p, p, conv_w, conv_b.reshape(1, ch), ln_g.reshape(1, ch), ln_b.reshape(1, ch))


ATTN_QBLK = 256


ATTN_ROWS = 128
ATTN_ONES = 16
ATTN_REDO = 120.0
ATTN_REF_ROW = (DIFF_DQK, 0)


def _round_bf16(x):
    return x.astype(BF16).astype(F32)


def _diff_attn_kernel(qt_ref, k_ref, vt_ref, lam_ref, g_ref, o_ref, qm_ref, m_ref, acc_ref, *, lambda_init):
    kv = pl.program_id(3)
    tq = qm_ref.shape[2]
    qblk = min(ATTN_QBLK, tq)
    chains = [(mi, slice(q0, q0 + qblk)) for q0 in range(0, tq, qblk) for mi in range(2)]

    k = k_ref[0]
    lane = lax.broadcasted_iota(jnp.int32, k.shape, 1)
    kx = [jnp.where(lane == r, jnp.ones_like(k), k) for r in ATTN_REF_ROW]
    vtx = vt_ref[0, 0]

    def scores(mi, qs, rows=None):
        keys = kx[mi] if rows is None else kx[mi][:rows]
        return jnp.dot(keys, qm_ref[mi, :, qs].astype(BF16), preferred_element_type=F32)

    def move_reference(mi, qs, m_new):
        m_ref[mi, :, qs] = m_new
        r = ATTN_REF_ROW[mi]
        qm_ref[mi, r:r + 1, qs] = -m_new

    @pl.when(kv == 0)
    def _():
        qt = qt_ref[0].astype(F32)
        row = lax.broadcasted_iota(jnp.int32, qt.shape, 0)
        qm_ref[0] = jnp.where(row < DIFF_DQK, qt, 0.0)
        qm_ref[1] = jnp.where(row < DIFF_DQK, 0.0, qt)
        acc_ref[...] = jnp.zeros(acc_ref.shape, F32)
        first = [scores(mi, qs, ATTN_ROWS) for mi, qs in chains]
        for (mi, qs), s in zip(chains, first):
            move_reference(mi, qs, _round_bf16(jnp.max(s, axis=0, keepdims=True)))

    sc = [scores(mi, qs) for mi, qs in chains]
    mxs, pvs = [], []
    for s in sc:
        es, mx = [], None
        for r0 in range(0, s.shape[0], ATTN_ROWS):
            grp = s[r0:r0 + ATTN_ROWS]
            es.append(jnp.exp2(grp).astype(BF16))
            top = jnp.max(grp, axis=0, keepdims=True)
            mx = top if mx is None else jnp.maximum(mx, top)
        mxs.append(mx)
        pvs.append(jnp.dot(vtx, jnp.concatenate(es, axis=0), preferred_element_type=F32))
    redo = jnp.max(functools.reduce(jnp.maximum, mxs)) > ATTN_REDO

    @pl.when(jnp.logical_not(redo))
    def _():
        for (mi, qs), mx, pv in zip(chains, mxs, pvs):
            m_old = m_ref[mi, :, qs]
            m_new = _round_bf16(m_old + jnp.maximum(mx, 0.0))
            acc_ref[mi, :, qs] = (acc_ref[mi, :, qs] + pv) * jnp.exp2(m_old - m_new)
            move_reference(mi, qs, m_new)

    @pl.when(redo)
    def _():
        for (mi, qs), s in zip(chains, sc):
            m_old = m_ref[mi, :, qs]
            m_new = _round_bf16(m_old + jnp.maximum(jnp.max(s, axis=0, keepdims=True), 0.0))
            delta = m_new - m_old
            e = jnp.exp2(s - delta)
            acc_ref[mi, :, qs] = jnp.exp2(-delta) * acc_ref[mi, :, qs] + jnp.dot(vtx, e.astype(BF16),
                                                                                 preferred_element_type=F32)
            move_reference(mi, qs, m_new)

    @pl.when(kv == pl.num_programs(3) - 1)
    def _():
        lam = (jnp.exp(jnp.sum(lam_ref[0:1, :] * lam_ref[1:2, :], axis=-1, keepdims=True))
               - jnp.exp(jnp.sum(lam_ref[2:3, :] * lam_ref[3:4, :], axis=-1, keepdims=True)) + lambda_init)
        dv = DIFF_DV
        ot = (acc_ref[0, :dv, :] / acc_ref[0, dv:dv + 1, :]
              - lam * (acc_ref[1, :dv, :] / acc_ref[1, dv:dv + 1, :]))
        ot = ot * lax.rsqrt(jnp.mean(ot * ot, axis=0, keepdims=True) + EPS)
        o = ot.T * g_ref[...]
        o_ref[0] = (o * (1.0 - lambda_init)).astype(o_ref.dtype)


def diff_attention(qt, k, vt, lam_vecs, subln_g, lambda_init, k_blk):
    b, _, lq = qt.shape
    lk = k.shape[1]
    tq = _tile(lq, 2048, LANE)
    tk = _tile(lk, 1536, LANE)
    rows = vt.shape[2]
    return pl.pallas_call(
        functools.partial(_diff_attn_kernel, lambda_init=lambda_init),
        grid=(b, DIFF_HEADS, lq // tq, lk // tk),
        in_specs=[pl.BlockSpec((1, LANE, tq), lambda bi, h, i, j: (bi, h, i)),
                  pl.BlockSpec((1, tk, LANE), lambda bi, h, i, j: (bi, j, k_blk + h)),
                  pl.BlockSpec((1, 1, rows, tk), lambda bi, h, i, j: (bi, h, 0, j)),
                  _const_spec((4, DIFF_DQK)), _const_spec((1, DIFF_DV))],
        out_specs=pl.BlockSpec((1, tq, LANE), lambda bi, h, i, j: (bi, i, h)),
        out_shape=jax.ShapeDtypeStruct((b, lq, DIFF_HEADS * DIFF_DV), BF16),
        scratch_shapes=[pltpu.VMEM((2, LANE, tq), F32), pltpu.VMEM((2, 1, tq), F32),
                        pltpu.VMEM((2, rows, tq), F32)],
        compiler_params=_cparams("parallel", "parallel", "parallel", "arbitrary"),
        name="diff_attention",
    )(qt, k, vt, lam_vecs, subln_g.reshape(1, DIFF_DV))


def _out_mlp_kernel(x_ref, a_ref, b_ref, wo_ref, gt1_ref, g2_ref, sh2_ref, sc2_ref, gt2_ref, w1_ref, w2_ref,
                    fg_ref, o_ref, *, a_transposed, hidden_chunk, final_norm):
    half = wo_ref.shape[0] // 2
    if a_transposed:
        ya = lax.dot_general(a_ref[0], wo_ref[0:half, :], (((0,), (0,)), ((), ())), preferred_element_type=F32)
    else:
        ya = jnp.dot(a_ref[0], wo_ref[0:half, :], preferred_element_type=F32)
    y = ya + jnp.dot(b_ref[0], wo_ref[half:, :], preferred_element_type=F32)
    x1 = x_ref[0] + gt1_ref[0] * y
    h = _norm_mod(x1, g2_ref[...], sh2_ref[0], sc2_ref[0]).astype(BF16)
    acc = jnp.zeros_like(x1)
    for c0 in range(0, w1_ref.shape[1], hidden_chunk):
        t = jnp.maximum(jnp.dot(h, w1_ref[:, c0:c0 + hidden_chunk], preferred_element_type=F32), 0.0)
        acc = acc + jnp.dot((t * t).astype(BF16), w2_ref[c0:c0 + hidden_chunk, :], preferred_element_type=F32)
    x2 = x1 + gt2_ref[0] * acc
    if final_norm:
        x2 = x2 * lax.rsqrt(jnp.mean(x2 * x2, axis=-1, keepdims=True) + EPS) * fg_ref[...]
    o_ref[0] = x2


def out_proj_mlp(x, mix_a, mix_b, w_out, gt1, g2, sh2, sc2, gt2, w1, w2, final_g=None, a_transposed=False):
    b, l, d = x.shape
    c = mix_b.shape[2]
    hid = w1.shape[1]
    tm = _tile(l, 512, LANE)
    row = lambda bi, i: (bi, i, 0)
    vec = lambda bi, i: (bi, 0, 0)
    a_spec = (pl.BlockSpec((1, c, tm), lambda bi, i: (bi, 0, i)) if a_transposed
              else pl.BlockSpec((1, tm, c), row))
    fg = jnp.ones((1, d), F32) if final_g is None else final_g.reshape(1, d)
    return pl.pallas_call(
        functools.partial(_out_mlp_kernel, a_transposed=a_transposed, hidden_chunk=min(hid, 1024),
                          final_norm=final_g is not None),
        grid=(b, l // tm),
        in_specs=[pl.BlockSpec((1, tm, d), row), a_spec, pl.BlockSpec((1, tm, c), row),
                  _const_spec((2 * c, d)),
                  pl.BlockSpec((1, 1, d), vec), _const_spec((1, d)),
                  pl.BlockSpec((1, 1, d), vec), pl.BlockSpec((1, 1, d), vec), pl.BlockSpec((1, 1, d), vec),
                  _const_spec((d, hid)), _const_spec((hid, d)), _const_spec((1, d))],
        out_specs=pl.BlockSpec((1, tm, d), row),
        out_shape=jax.ShapeDtypeStruct((b, l, d), F32),
        compiler_params=_cparams("parallel", "parallel"),
        name="out_proj_mlp",
    )(x, mix_a, mix_b, w_out, gt1, g2.reshape(1, d), sh2, sc2, gt2, w1, w2, fg)


def _rope_tables(length, scale):
    f32 = np.float32
    rows = length // GRID_W
    row = np.repeat(np.arange(rows, dtype=f32), GRID_W)
    col = np.tile(np.arange(GRID_W, dtype=f32), rows)
    n_freq = DIFF_DQK // 4
    inv = (f32(ROPE_BASE) ** (-np.arange(n_freq, dtype=f32) / f32(n_freq))).astype(f32)
    ang_r = row[:, None] * inv
    ang_c = col[:, None] * inv
    cos = np.concatenate([np.cos(ang_r)] * 2 + [np.cos(ang_c)] * 2, axis=-1)
    sin = np.concatenate([-np.sin(ang_r), np.sin(ang_r), -np.sin(ang_c), np.sin(ang_c)], axis=-1)
    cos = np.concatenate([cos, cos], axis=-1) * f32(scale)
    sin = np.concatenate([sin, sin], axis=-1) * f32(scale)
    return cos.astype(f32), sin.astype(f32)


def _flat_tables(length, scale):
    return np.full((length, LANE), scale, np.float32), np.zeros((length, LANE), np.float32)


def _proj_t_kernel(x_ref, g_ref, sh_ref, sc_ref, wt_ref, o_ref, *, rc):
    h = _norm_mod(x_ref[0], g_ref[...], sh_ref[0], sc_ref[0]).astype(BF16)
    for r0 in range(0, wt_ref.shape[0], rc):
        o_ref[0, r0:r0 + rc, :] = lax.dot_general(wt_ref[r0:r0 + rc, :], h, (((1,), (1,)), ((), ())),
                                                  preferred_element_type=F32)


def norm_mod_proj_t(x, g, sh, sc, wt):
    b, l, d = x.shape
    c = wt.shape[0]
    tl = _tile(l, 512, LANE)
    return pl.pallas_call(
        functools.partial(_proj_t_kernel, rc=_tile(c, 512)),
        grid=(b, l // tl),
        in_specs=[pl.BlockSpec((1, tl, d), lambda bi, i: (bi, i, 0)),
                  _const_spec((1, d)),
                  pl.BlockSpec((1, 1, d), lambda bi, i: (bi, 0, 0)),
                  pl.BlockSpec((1, 1, d), lambda bi, i: (bi, 0, 0)),
                  _const_spec((c, d))],
        out_specs=pl.BlockSpec((1, c, tl), lambda bi, i: (bi, 0, i)),
        out_shape=jax.ShapeDtypeStruct((b, c, l), F32),
        compiler_params=_cparams("parallel", "parallel"),
        name="norm_mod_proj_t",
    )(x, g.reshape(1, d), sh, sc, wt)


def _dot_hi(a, b):
    return jnp.dot(a, b, preferred_element_type=F32, precision=lax.Precision.HIGHEST)


def _hy_hidden_kernel(z_ref, w1_ref, b1_ref, f1_ref, w2_ref, b2_ref, f2_ref, o_ref):
    hid = jnp.sin(f1_ref[...] * (_dot_hi(w1_ref[...], z_ref[...]) + b1_ref[...]))
    o_ref[...] = jnp.sin(f2_ref[...] * (_dot_hi(w2_ref[...], hid) + b2_ref[...]))


def hyena_hidden(zt, w1t, b1, f1, w2t, b2, f2):
    e, n = zt.shape
    f = w1t.shape[0]
    tn = _tile(n, 2048, LANE)
    col = lambda v: v.reshape(f, 1)
    return pl.pallas_call(
        _hy_hidden_kernel,
        grid=(n // tn,),
        in_specs=[pl.BlockSpec((e, tn), lambda j: (0, j)),
                  _const_spec((f, e)), _const_spec((f, 1)), _const_spec((f, 1)),
                  _const_spec((f, f)), _const_spec((f, 1)), _const_spec((f, 1))],
        out_specs=pl.BlockSpec((f, tn), lambda j: (0, j)),
        out_shape=jax.ShapeDtypeStruct((f, n), F32),
        compiler_params=_cparams("parallel"),
        name="hyena_hidden",
    )(zt, w1t, col(b1), col(f1), w2t, col(b2), col(f2))


def _hy_filter_kernel(hid_ref, t_ref, wf_ref, wb_ref, dl_ref, o_ref, *, length, tn):
    n2 = hid_ref.shape[1]
    delta = dl_ref[...]

    def piece(c0):
        hid = hid_ref[:, c0:c0 + tn]
        dec = jnp.exp(-t_ref[:, c0:c0 + tn] * delta)
        return _dot_hi(wf_ref[...], hid) * dec, _dot_hi(wb_ref[...], hid) * dec

    norm = jnp.zeros((wf_ref.shape[0], 1), F32)
    for c0 in range(length, n2, tn):
        ff, fb = piece(c0)
        norm = norm + jnp.sum(jnp.abs(ff) + jnp.abs(fb), axis=-1, keepdims=True)
    inv = 1.0 / norm
    for c0 in range(0, n2, tn):
        ff, fb = piece(c0)
        if c0 >= length:
            o_ref[0, :, c0:c0 + tn] = ff * inv
        else:
            col = lax.broadcasted_iota(jnp.int32, fb.shape, 1) + c0
            o_ref[0, :, c0:c0 + tn] = jnp.where(col == 0, 0.0, fb * inv)


def hyena_filters(hid, t_row, w3t, deltas, length):
    f, n2 = hid.shape
    ch = deltas.shape[0]
    cb = LANE
    nblk = ch // cb
    tn = _tile(length, 2048, LANE)
    return pl.pallas_call(
        functools.partial(_hy_filter_kernel, length=length, tn=tn),
        grid=(HY_ORDER, nblk),
        in_specs=[_const_spec((f, n2)), _const_spec((1, n2)),
                  pl.BlockSpec((cb, f), lambda o, c: (o * 2 * nblk + c, 0)),
                  pl.BlockSpec((cb, f), lambda o, c: (o * 2 * nblk + nblk + c, 0)),
                  pl.BlockSpec((cb, 1), lambda o, c: (c, 0))],
        out_specs=pl.BlockSpec((1, cb, n2), lambda o, c: (o, c, 0)),
        out_shape=jax.ShapeDtypeStruct((HY_ORDER, ch, n2), F32),
        compiler_params=_cparams("parallel", "parallel"),
        name="hyena_filters",
    )(hid, t_row, w3t, w3t, deltas.reshape(ch, 1))


HY_T = 256
HY_CB = SUBLANE

def _hyena_kernel(v_ref, x1_ref, x2_ref, g_ref, sw_ref, sb_ref, sk_ref_s, o_ref, sk_ref, u_ref, acc_ref,
                  *, length, nch):
    nb = v_ref.shape[0]
    nj = length // HY_T
    cblk = pl.program_id(0)
    lane = lax.broadcasted_iota(jnp.int32, (nb, length), 1)

    def short(ref, cc, part):
        x = ref[:, cc, :]
        ch = part * nch + cblk * HY_CB + cc
        prev = jnp.where(lane == 0, 0.0, pltpu.roll(x, 1, 1))
        nxt = jnp.where(lane == length - 1, 0.0, pltpu.roll(x, length - 1, 1))
        return sw_ref[0, ch] * prev + sw_ref[1, ch] * x + sw_ref[2, ch] * nxt + sb_ref[ch]

    def long_conv(u, order, cc):
        g = jnp.broadcast_to(g_ref[order, pl.ds(cc, 1), :], (2 * SUBLANE, 2 * length))
        half = HY_T // 2
        for r0 in range(0, half, 2 * SUBLANE):
            pair = pltpu.roll(g, r0, 1, stride=1, stride_axis=0).astype(BF16)
            sk_ref[r0:r0 + 2 * SUBLANE, :] = pair
            sk_ref[half + r0:half + r0 + 2 * SUBLANE, half:] = pair[:, :2 * length - half]
        rows = nj * nb
        x2 = jnp.concatenate([u[:, j * HY_T:(j + 1) * HY_T] for j in range(nj)]
                             + [jnp.zeros((2 * nb, HY_T), F32)], axis=0)
        u_ref[0] = x2.astype(BF16)
        u_ref[1] = jnp.concatenate([x2[nb:, :], x2[rows:rows + nb, :]], axis=0).astype(BF16)
        acc_ref[...] = jnp.zeros(acc_ref.shape, F32)
        for d in range(-(nj - 1), nj):
            m = (nj - abs(d)) * nb
            lo_in, lo_out = max(0, -d) * nb, max(0, d) * nb
            which, off = (0, lo_in) if lo_in % (2 * nb) == 0 else (1, lo_in - nb)
            mm = -(-m // (2 * nb)) * (2 * nb)
            tile = sk_ref[:, length + d * HY_T:length + (d + 1) * HY_T]
            part = jnp.dot(u_ref[which, off:off + mm, :], tile, preferred_element_type=F32)
            acc_ref[lo_out:lo_out + m, :] += part[:m]
        return jnp.concatenate([acc_ref[i * nb:(i + 1) * nb, :] for i in range(nj)], axis=1)

    def channel(cc, carry):
        v = short(v_ref, cc, 0)
        x1 = short(x1_ref, cc, 1)
        x2 = short(x2_ref, cc, 2)
        ch = cblk * HY_CB + cc
        z = x1 * (long_conv(v, 0, cc) + sk_ref_s[0, ch] * v)
        z = x2 * (long_conv(z, 1, cc) + sk_ref_s[1, ch] * z)
        o_ref[:, cc, :] = z
        return carry

    lax.fori_loop(0, HY_CB, channel, 0)


def hyena_mix(pt, filt, short_w, short_b, skip):
    b, c3, l = pt.shape
    nch = c3 // 3
    assert l % HY_T == 0 and nch % HY_CB == 0 and b == SUBLANE
    nblk = nch // HY_CB
    smem = pl.BlockSpec(memory_space=pltpu.SMEM)
    part = lambda k: pl.BlockSpec((b, HY_CB, l), lambda c: (0, k * nblk + c, 0))
    return pl.pallas_call(
        functools.partial(_hyena_kernel, length=l, nch=nch),
        grid=(nblk,),
        in_specs=[part(0), part(1), part(2),
                  pl.BlockSpec((HY_ORDER, HY_CB, 2 * l), lambda c: (0, c, 0)),
                  smem, smem, smem],
        out_specs=pl.BlockSpec((b, HY_CB, l), lambda c: (0, c, 0)),
        out_shape=jax.ShapeDtypeStruct((b, nch, l), F32),
        scratch_shapes=[pltpu.VMEM((HY_T, 2 * l), BF16),
                        pltpu.VMEM((2, l // HY_T * b + 2 * b, HY_T), BF16),
                        pltpu.VMEM((l // HY_T * b, HY_T), F32)],
        compiler_params=_cparams("parallel"),
        name="hyena_mix",
    )(pt, pt, pt, filt, short_w, short_b, skip)


DN_HALO = SUBLANE
DN_BETA_F, DN_BETA_B, DN_G_F, DN_G_B = 0, DN_HEADS, 2 * DN_HEADS, 3 * DN_HEADS


def _dn_feat_kernel(c_ref, p_ref, n_ref, s_ref, w_ref, al_ref, dt_ref, q_ref, k_ref, v_ref, f_ref, u_ref, *, tl):
    i = pl.program_id(1)
    last = pl.num_programs(1) - 1
    left = (DN_CONV_W - 1) // 2
    nc = c_ref.shape[2]
    u_ref[DN_HALO:DN_HALO + tl, :] = c_ref[0]
    u_ref[0:DN_HALO, :] = jnp.where(i > 0, p_ref[0], 0.0)
    u_ref[DN_HALO + tl:DN_HALO + tl + DN_HALO, :] = jnp.where(i < last, n_ref[0], 0.0)
    rows = min(tl, 128)
    outs = (q_ref, k_ref, v_ref)
    per = nc // len(outs)
    for r0 in range(0, tl, rows):
        for c0 in range(0, nc, LANE):
            acc = jnp.zeros((rows, LANE), F32)
            for k in range(DN_CONV_W):
                s = DN_HALO - left + k + r0
                acc = acc + w_ref[k:k + 1, c0:c0 + LANE] * u_ref[s:s + rows, c0:c0 + LANE]
            y = _silu(acc)
            which, off = divmod(c0, per)
            if which < 2:
                y = y * lax.rsqrt(jnp.sum(y * y, axis=-1, keepdims=True) + 1e-6)
            outs[which][0, r0:r0 + rows, off:off + LANE] = y
    s = s_ref[0]
    lane = lax.broadcasted_iota(jnp.int32, s.shape, 1)
    gate = -jnp.exp(al_ref[...]) * jax.nn.softplus(s + dt_ref[...])
    f_ref[0] = jnp.where(lane < DN_G_F, jax.nn.sigmoid(s), gate)


def dn_features(qkv, small, conv_w, alog_row, dtb_row):
    b, l, nc = qkv.shape
    tl = _tile(l, 256, DN_HALO)
    nh = tl // DN_HALO
    nblk = l // DN_HALO
    w = nc // 3
    row = lambda bi, i: (bi, i, 0)
    return pl.pallas_call(
        functools.partial(_dn_feat_kernel, tl=tl),
        grid=(b, l // tl),
        in_specs=[pl.BlockSpec((1, tl, nc), row),
                  pl.BlockSpec((1, DN_HALO, nc), lambda bi, i: (bi, jnp.maximum(i * nh - 1, 0), 0)),
                  pl.BlockSpec((1, DN_HALO, nc), lambda bi, i: (bi, jnp.minimum((i + 1) * nh, nblk - 1), 0)),
                  pl.BlockSpec((1, tl, LANE), row),
                  _const_spec((DN_CONV_W, nc)), _const_spec((1, LANE)), _const_spec((1, LANE))],
        out_specs=[pl.BlockSpec((1, tl, w), row)] * 3 + [pl.BlockSpec((1, tl, LANE), row)],
        out_shape=[jax.ShapeDtypeStruct((b, l, w), F32)] * 3 + [jax.ShapeDtypeStruct((b, l, LANE), F32)],
        scratch_shapes=[pltpu.VMEM((tl + 2 * DN_HALO, nc), F32)],
        compiler_params=_cparams("parallel", "parallel"),
        name="dn_features",
    )(qkv, qkv, qkv, small, conv_w, alog_row, dtb_row)


def _bdot(a, b):
    return jnp.dot(a.astype(BF16), b.astype(BF16), preferred_element_type=F32)


def _bdot_nt(a, b):
    return lax.dot_general(a.astype(BF16), b.astype(BF16), (((1,), (1,)), ((), ())), preferred_element_type=F32)


def _bdot_tn(a, b):
    return lax.dot_general(a.astype(BF16), b.astype(BF16), (((0,), (0,)), ((), ())), preferred_element_type=F32)


def _cumsum_rows(x, reverse):
    n = x.shape[0]
    row = lax.broadcasted_iota(jnp.int32, x.shape, 0)
    s = 1
    while s < n:
        if reverse:
            x = x + jnp.where(row < n - s, pltpu.roll(x, n - s, 0), 0.0)
        else:
            x = x + jnp.where(row >= s, pltpu.roll(x, s, 0), 0.0)
        s *= 2
    return x


def _dn_chunks(chains):
    c = chains[0]["k"].shape[0]
    ri = lax.broadcasted_iota(jnp.int32, (c, c), 0)
    ci = lax.broadcasted_iota(jnp.int32, (c, c), 1)
    work = []
    for ch in chains:
        rev = ch["reverse"]
        incl = (ri <= ci) if rev else (ri >= ci)
        strict = (ri < ci) if rev else (ri > ci)
        decay = jnp.where(incl, jnp.exp(jnp.where(incl, ch["gcol"] - ch["grow"], 0.0)), 0.0)
        kb = ch["k"] * ch["beta"]
        qs = ch["q"] * (ch["k"].shape[1] ** -0.5)
        work.append(dict(ch, strict=strict, decay=decay, kb=kb, qs=qs))
    grams = [_bdot_nt(jnp.concatenate([w["kb"], w["qs"]], axis=0), w["k"]) for w in work]
    lowers = [jnp.where(w["strict"], g[:c] * w["decay"], 0.0) for w, g in zip(work, grams)]
    same2 = (ri // 2) == (ci // 2)
    ymats = [-jnp.where(same2, lo, 0.0) for lo in lowers]
    m = 2
    while m < c:
        pair = ((ri // (2 * m)) == (ci // (2 * m))) & ((ri // m) != (ci // m))
        offs = [jnp.where(pair, lo, 0.0) for lo in lowers]
        zs = [off + _bdot(ym, off) for ym, off in zip(ymats, offs)]
        ymats = [ym - z - _bdot(z, ym) for ym, z in zip(ymats, zs)]
        m *= 2
    outs = []
    uws, e_gs = [], []
    for w, ym in zip(work, ymats):
        e_g = jnp.exp(w["gcol"])
        rhs = jnp.concatenate([w["v"] * w["beta"], w["kb"] * e_g], axis=1)
        uws.append(rhs + _bdot(ym, rhs))
        e_gs.append(e_g)
    dv = work[0]["v"].shape[1]
    wss = [_bdot(jnp.concatenate([uw[:, dv:], w["qs"] * e_g], axis=0), w["state"])
           for w, uw, e_g in zip(work, uws, e_gs)]
    v_news = [uw[:, :dv] - ws[:c] for uw, ws in zip(uws, wss)]
    for w, g, ws, v_new in zip(work, grams, wss, v_news):
        gcol = w["gcol"]
        g_last = gcol[0:1, :] if w["reverse"] else gcol[c - 1:c, :]
        k_dec = w["k"] * jnp.exp(g_last - gcol)
        o = ws[c:] + _bdot(g[c:] * w["decay"], v_new)
        new_state = w["state"] * jnp.exp(g_last) + _bdot_tn(k_dec, v_new)
        outs.append((o, new_state))
    return outs


def _dn_scan_kernel(qf_ref, kf_ref, vf_ref, ff_ref, qb_ref, kb_ref, vb_ref, fb_ref, s0f_ref, s0b_ref,
                    of_ref, ob_ref, sf_ref, sb_ref, st_ref):
    n = pl.program_id(1)

    @pl.when(n == 0)
    def _():
        st_ref[0] = s0f_ref[0]
        st_ref[1] = s0b_ref[0]

    dirs = ((qf_ref, kf_ref, vf_ref, ff_ref, of_ref, False), (qb_ref, kb_ref, vb_ref, fb_ref, ob_ref, True))
    chains, dests = [], []
    for di, (q_ref, k_ref, v_ref, f_ref, o_ref, reverse) in enumerate(dirs):
        feats = f_ref[0]
        gc = _cumsum_rows(feats, reverse)
        gct = gc.T
        for h in range(DN_HEADS):
            cols = slice(h * DN_DK, (h + 1) * DN_DK)
            bl = (DN_BETA_B if reverse else DN_BETA_F) + h
            gl = (DN_G_B if reverse else DN_G_F) + h
            chains.append(dict(q=q_ref[0, :, cols], k=k_ref[0, :, cols], v=v_ref[0, :, cols],
                               beta=feats[:, bl:bl + 1], gcol=gc[:, gl:gl + 1], grow=gct[gl:gl + 1, :],
                               state=st_ref[di, h], reverse=reverse))
            dests.append((o_ref, cols, di, h))
    for (o_ref, cols, di, h), (o, s_new) in zip(dests, _dn_chunks(chains)):
        o_ref[0, :, cols] = o
        st_ref[di, h] = s_new

    @pl.when(n == pl.num_programs(1) - 1)
    def _():
        sf_ref[0] = st_ref[0]
        sb_ref[0] = st_ref[1]


def dn_scan(q, k, v, feats, s0f, s0b):
    b, l, w = q.shape
    nchunk = l // DN_CHUNK
    fwd = lambda bi, n: (bi, n, 0)
    bwd = lambda bi, n: (bi, nchunk - 1 - n, 0)
    st = lambda bi, n: (bi, 0, 0, 0)
    blk = lambda im, width: pl.BlockSpec((1, DN_CHUNK, width), im)
    st_spec = pl.BlockSpec((1, DN_HEADS, DN_DK, DN_DK), st)
    st_shape = jax.ShapeDtypeStruct((b, DN_HEADS, DN_DK, DN_DK), F32)
    return pl.pallas_call(
        _dn_scan_kernel,
        grid=(b, nchunk),
        in_specs=[blk(fwd, w), blk(fwd, w), blk(fwd, w), blk(fwd, LANE),
                  blk(bwd, w), blk(bwd, w), blk(bwd, w), blk(bwd, LANE), st_spec, st_spec],
        out_specs=[blk(fwd, w), blk(bwd, w), st_spec, st_spec],
        out_shape=[jax.ShapeDtypeStruct((b, l, w), F32)] * 2 + [st_shape, st_shape],
        scratch_shapes=[pltpu.VMEM((2, DN_HEADS, DN_DK, DN_DK), F32)],
        compiler_params=_cparams("parallel", "arbitrary"),
        name="dn_scan",
    )(q, k, v, feats, q, k, v, feats, s0f, s0b)


def _dn_out_kernel(of_ref, ob_ref, gate_ref, g_ref, o_ref):
    o = of_ref[0] + ob_ref[0]
    for c0 in range(0, o.shape[1], DN_DK):
        t = o[:, c0:c0 + DN_DK]
        t = t * lax.rsqrt(jnp.mean(t * t, axis=-1, keepdims=True) + EPS) * g_ref[...]
        o_ref[0, :, c0:c0 + DN_DK] = (t * _silu(gate_ref[0, :, c0:c0 + DN_DK])).astype(o_ref.dtype)


def dn_output(o_f, o_b, gate, norm_g):
    b, l, w = o_f.shape
    tl = _tile(l, 512)
    row = lambda bi, i: (bi, i, 0)
    return pl.pallas_call(
        _dn_out_kernel,
        grid=(b, l // tl),
        in_specs=[pl.BlockSpec((1, tl, w), row)] * 3 + [_const_spec((1, DN_DK))],
        out_specs=pl.BlockSpec((1, tl, w), row),
        out_shape=jax.ShapeDtypeStruct((b, l, w), BF16),
        compiler_params=_cparams("parallel", "parallel"),
        name="dn_output",
    )(o_f, o_b, gate, norm_g.reshape(1, DN_DK))


def _split_mods(mods, b, d):
    lat = [mods[:b, k * d:(k + 1) * d].reshape(b, 1, d) for k in range(N_MOD)]
    ctx = [jnp.broadcast_to(mods[b, k * d:(k + 1) * d].reshape(1, 1, d), (b, 1, d)) for k in range(N_MOD)]
    return lat, ctx


def _even_layer(x, ctx, lat_m, ctx_m, n1g, n2g, w_in, conv_w, conv_b, ln_g, ln_b, lam_vecs, subln_g, w_out,
                w1, w2, lambda_init):
    b, l, d = x.shape
    lc = ctx.shape[1]
    ch = conv_w.shape[1]
    qk = DIFF_HEADS * 2 * DIFF_DQK
    q0, k0, v0, n_in = 2 * ch, 2 * ch + qk, 2 * ch + 2 * qk, 2 * ch + 2 * qk + DIFF_HEADS * DIFF_DV
    scale = DIFF_DQK ** -0.5 * math.log2(math.e)
    splits = [(0, q0), (q0, n_in)]
    w_in = w_in.astype(BF16)
    rope_lat = _rope_tables(l, scale) + _rope_tables(l, 1.0)
    rope_ctx = _flat_tables(lc, scale) + _flat_tables(lc, 1.0)
    pc_lat, qkv_lat = norm_mod_proj(x, n1g, lat_m[0], lat_m[1], w_in, splits, (F32, BF16), rope_lat, (q0, k0, v0))
    pc_ctx, qkv_ctx = norm_mod_proj(ctx, n1g, ctx_m[0], ctx_m[1], w_in, splits, (F32, BF16), rope_ctx, (q0, k0, v0))
    conv_lat = conformer_conv(pc_lat, conv_w, conv_b, ln_g, ln_b)
    conv_ctx = conformer_conv(pc_ctx, conv_w, conv_b, ln_g, ln_b)
    k_all = jnp.concatenate([qkv_lat[:, :, qk:2 * qk], qkv_ctx[:, :, qk:2 * qk]], axis=1)
    def heads_t(v):
        vt = jnp.swapaxes(v, 1, 2).reshape(b, DIFF_HEADS, DIFF_DV, v.shape[1])
        return jnp.concatenate([vt, jnp.ones((b, DIFF_HEADS, ATTN_ONES, v.shape[1]), vt.dtype)], axis=2)

    vt_ctx = heads_t(qkv_ctx[:, :, 2 * qk:])
    vt_all = jnp.concatenate([heads_t(qkv_lat[:, :, 2 * qk:]), vt_ctx], axis=3)
    qt_lat = jnp.swapaxes(qkv_lat[:, :, :qk], 1, 2)
    qt_ctx = jnp.swapaxes(qkv_ctx[:, :, :qk], 1, 2)
    o_lat = diff_attention(qt_lat, k_all, vt_all, lam_vecs, subln_g, lambda_init, 0)
    o_ctx = diff_attention(qt_ctx, qkv_ctx, vt_ctx, lam_vecs, subln_g, lambda_init, qk // LANE)
    w_out, w1, w2 = w_out.astype(BF16), w1.astype(BF16), w2.astype(BF16)
    x = out_proj_mlp(x, conv_lat, o_lat, w_out, lat_m[2], n2g, lat_m[3], lat_m[4], lat_m[5], w1, w2)
    ctx = out_proj_mlp(ctx, conv_ctx, o_ctx, w_out, ctx_m[2], n2g, ctx_m[3], ctx_m[4], ctx_m[5], w1, w2)
    return x, ctx


def _hyena_position_tables(length):
    f32 = np.float32
    t = np.linspace(0.0, 1.0, length, dtype=f32)[:, None]
    bands = (HY_EMB - 1) // 2
    omega = (f32(2.0 * math.pi) * np.arange(length, dtype=f32)[:, None] / f32(length)).astype(f32)
    ang = omega * np.linspace(1e-4, bands - 1, bands, dtype=f32)
    z = np.concatenate([t, np.cos(ang), -np.sin(ang)], axis=-1).astype(f32)
    pos = np.minimum(np.abs(np.arange(2 * length) - length), length - 1)
    pad = -HY_EMB % SUBLANE
    zt = np.pad(z[pos].T, ((0, pad), (0, 0)))
    return np.ascontiguousarray(zt), np.ascontiguousarray(t[pos].T)


def _odd_layer_last(x, ctx, lat_m, ctx_m, n1g, n2g, w_in, hy_short_w, hy_short_b, hy_w1, hy_b1, hy_f1, hy_w2, hy_b2,
                    hy_f2, hy_w3, hy_skip, dn_conv_w, alog_f, alog_b, dtb_f, dtb_b, dn_norm_g, w_out, w1, w2,
                    final_g):
    b, l, d = x.shape
    hy_in = hy_short_w.shape[1]
    hy_ch = hy_in // (HY_ORDER + 1)
    dn_w = DN_HEADS * DN_DK
    n_rest = w_in.shape[1] - hy_in
    n_pad = -n_rest % LANE
    w_rest = jnp.pad(w_in[:, hy_in:], ((0, 0), (0, n_pad))).astype(BF16)
    splits = [(0, dn_w), (dn_w, 4 * dn_w), (4 * dn_w, n_rest + n_pad)]
    f32x3 = (F32, F32, F32)
    gate_lat, qkv_lat, small_lat = norm_mod_proj(x, n1g, lat_m[0], lat_m[1], w_rest, splits, f32x3)
    _, qkv_ctx, small_ctx = norm_mod_proj(ctx, n1g, ctx_m[0], ctx_m[1], w_rest, splits, f32x3)

    pt = norm_mod_proj_t(x, n1g, lat_m[0], lat_m[1], w_in[:, :hy_in].T.astype(BF16))
    zt, t_row = _hyena_position_tables(l)
    w1t = jnp.pad(hy_w1.T, ((0, 0), (0, zt.shape[0] - hy_w1.shape[0])))
    hid = hyena_hidden(zt, w1t, hy_b1, hy_f1, hy_w2.T, hy_b2, hy_f2)
    deltas = jnp.abs(jnp.linspace(HY_MIN_DECAY, HY_MAX_DECAY, hy_ch, dtype=F32))
    filt = hyena_filters(hid, t_row, hy_w3.T, deltas, l)
    hy = hyena_mix(pt, filt, hy_short_w, hy_short_b, hy_skip)

    lanes = jnp.zeros((LANE,), F32)
    alog_row = lanes.at[DN_G_F:DN_G_F + DN_HEADS].set(alog_f).at[DN_G_B:DN_G_B + DN_HEADS].set(alog_b).reshape(1, LANE)
    dtb_row = lanes.at[DN_G_F:DN_G_F + DN_HEADS].set(dtb_f).at[DN_G_B:DN_G_B + DN_HEADS].set(dtb_b).reshape(1, LANE)
    q_c, k_c, v_c, f_c = dn_features(qkv_ctx, small_ctx, dn_conv_w, alog_row, dtb_row)
    q_l, k_l, v_l, f_l = dn_features(qkv_lat, small_lat, dn_conv_w, alog_row, dtb_row)
    s0 = jnp.zeros((b, DN_HEADS, DN_DK, DN_DK), F32)
    _, _, s_cf, s_cb = dn_scan(q_c, k_c, v_c, f_c, s0, s0)
    o_f, o_b, _, _ = dn_scan(q_l, k_l, v_l, f_l, s_cf, s_cb)
    dn = dn_output(o_f, o_b, gate_lat, dn_norm_g)

    return out_proj_mlp(x, hy, dn, w_out.astype(BF16), lat_m[2], n2g, lat_m[3], lat_m[4], lat_m[5],
                        w1.astype(BF16), w2.astype(BF16), final_g=final_g, a_transposed=True)


def kernel(x, c, ctx, c_ctx, ada_w, ada_b, norm1_g, norm2_g, mlp_w1, mlp_w2, ev_w_in, ev_conv_w, ev_conv_b, ev_ln_g,
           ev_ln_b, ev_lq1, ev_lk1, ev_lq2, ev_lk2, ev_subln_g, ev_w_out, od_w_in, od_hy_short_w, od_hy_short_b,
           od_hy_w1, od_hy_b1, od_hy_freq1, od_hy_w2, od_hy_b2, od_hy_freq2, od_hy_w3, od_hy_skip, od_dn_conv_w,
           od_dn_alog_f, od_dn_alog_b, od_dn_dtb_f, od_dn_dtb_b, od_dn_norm_g, od_w_out, final_g):
    b, _, d = x.shape
    assert ada_w.shape[0] == 2, "layer 0 = conformer/diff-attention, layer 1 (last) = Hyena/DeltaNet"
    rows = -(-(b + 1) // SUBLANE) * SUBLANE
    cvec = jnp.zeros((rows, d), F32).at[:b].set(c).at[b].set(c_ctx)

    lat_m, ctx_m = _split_mods(ada_mods(cvec, ada_w[0], ada_b[0]), b, d)
    lam_vecs = jnp.stack([ev_lq1[0], ev_lk1[0], ev_lq2[0], ev_lk2[0]])
    x, ctx = _even_layer(x, ctx, lat_m, ctx_m, norm1_g[0], norm2_g[0], ev_w_in[0], ev_conv_w[0], ev_conv_b[0],
                         ev_ln_g[0], ev_ln_b[0], lam_vecs, ev_subln_g[0], ev_w_out[0], mlp_w1[0], mlp_w2[0],
                         0.8 - 0.6 * math.exp(-0.3 * 0))

    lat_m, ctx_m = _split_mods(ada_mods(cvec, ada_w[1], ada_b[1]), b, d)
    return _odd_layer_last(x, ctx, lat_m, ctx_m, norm1_g[1], norm2_g[1], od_w_in[0], od_hy_short_w[0],
                           od_hy_short_b[0], od_hy_w1[0], od_hy_b1[0], od_hy_freq1[0], od_hy_w2[0], od_hy_b2[0],
                           od_hy_freq2[0], od_hy_w3[0], od_hy_skip[0], od_dn_conv_w[0], od_dn_alog_f[0],
                           od_dn_alog_b[0], od_dn_dtb_f[0], od_dn_dtb_b[0], od_dn_norm_g[0], od_w_out[0],
                           mlp_w1[1], mlp_w2[1], final_g)
```

```python
import functools
import math

import jax
import jax.numpy as jnp
import numpy as np
from jax import lax
from jax.experimental import pallas as pl
from jax.experimental.pallas import tpu as pltpu

F32 = jnp.float32
BF16 = jnp.bfloat16

EPS = 1e-6
N_MOD = 6
GRID_W = 64
ROPE_BASE = 10000.0
CONV_W = 31
DIFF_HEADS = 4
DIFF_DQK = 64
DIFF_DV = 128
HY_ORDER = 2
HY_EMB = 33
HY_FFN = 64
HY_MAX_DECAY = math.log(1e-2) / 0.3
HY_MIN_DECAY = math.log(1e-2) / 1.5
DN_HEADS = 4
DN_DK = 128
DN_CONV_W = 5
DN_CHUNK = 128

LANE = 128
SUBLANE = 8
VMEM_LIMIT = 56 * 1024 * 1024


def _cparams(*sem):
    return pltpu.CompilerParams(dimension_semantics=sem, vmem_limit_bytes=VMEM_LIMIT)


def _tile(n, pref, mult=SUBLANE):
    if n <= pref:
        return n
    t = (pref // mult) * mult
    while t > mult and n % t:
        t -= mult
    assert n % t == 0, (n, pref, mult)
    return t


def _const_spec(shape):
    nd = len(shape)
    return pl.BlockSpec(shape, lambda *_: (0,) * nd, pipeline_mode=pl.Buffered(1))


def _silu(x):
    return x * jax.nn.sigmoid(x)


def _norm_mod(x, g, sh, sc):
    y = x * lax.rsqrt(jnp.mean(x * x, axis=-1, keepdims=True) + EPS)
    return (y * g) * (1.0 + sc) + sh


def _mods_kernel(c_ref, w_ref, b_ref, o_ref):
    cond = _silu(c_ref[...])
    o_ref[...] = jnp.dot(cond, w_ref[...], preferred_element_type=F32,
                         precision=lax.Precision.HIGHEST) + b_ref[...]


def ada_mods(cvec, w, b):
    r, d = cvec.shape
    n = w.shape[1]
    tn = _tile(n, 512, LANE)
    return pl.pallas_call(
        _mods_kernel,
        grid=(n // tn,),
        in_specs=[pl.BlockSpec((r, d), lambda j: (0, 0)),
                  pl.BlockSpec((d, tn), lambda j: (0, j)),
                  pl.BlockSpec((1, tn), lambda j: (0, j))],
        out_specs=pl.BlockSpec((r, tn), lambda j: (0, j)),
        out_shape=jax.ShapeDtypeStruct((r, n), F32),
        compiler_params=_cparams("arbitrary"),
        name="ada_mods",
    )(cvec, w, b.reshape(1, n))


def _swap16(y):
    n = y.shape[-1]
    lane = lax.broadcasted_iota(jnp.int32, y.shape, y.ndim - 1)
    fwd = pltpu.roll(y, n - 16, y.ndim - 1)
    bwd = pltpu.roll(y, 16, y.ndim - 1)
    return jnp.where((lane % 32) < 16, fwd, bwd)


PROJ_CHUNK = 512


def _proj_kernel(x_ref, g_ref, sh_ref, sc_ref, w_ref, *rest, splits, rope_cols):
    n_out = len(splits)
    if rope_cols:
        cq_ref, sq_ref, ck_ref, sk_ref = rest[:4]
        rest = rest[4:]
    o_refs = rest[:n_out]
    h = _norm_mod(x_ref[0], g_ref[...], sh_ref[0], sc_ref[0]).astype(BF16)
    for o_ref, (s0, s1) in zip(o_refs, splits):
        tc = PROJ_CHUNK if (s1 - s0) % PROJ_CHUNK == 0 and s0 % PROJ_CHUNK == 0 else LANE
        for c0 in range(s0, s1, tc):
            y = jnp.dot(h, w_ref[:, c0:c0 + tc], preferred_element_type=F32)
            if rope_cols and rope_cols[0] <= c0 < rope_cols[2]:
                is_q = c0 < rope_cols[1]
                cos = (cq_ref if is_q else ck_ref)[...]
                sin = (sq_ref if is_q else sk_ref)[...]
                reps = tc // LANE
                cos = jnp.concatenate([cos] * reps, axis=1)
                sin = jnp.concatenate([sin] * reps, axis=1)
                y = y * cos + _swap16(y) * sin
            o_ref[0, :, c0 - s0:c0 - s0 + tc] = y.astype(o_ref.dtype)


def norm_mod_proj(x, g, sh, sc, w, splits, dtypes, rope=None, rope_cols=None):
    b, l, d = x.shape
    n = w.shape[1]
    tm = _tile(l, 512)
    assert all(s % LANE == 0 for sp in splits for s in sp)
    if rope_cols:
        assert all(c % PROJ_CHUNK == 0 for c in rope_cols)
    row = lambda bi, i: (bi, i, 0)
    in_specs = [pl.BlockSpec((1, tm, d), row),
                _const_spec((1, d)),
                pl.BlockSpec((1, 1, d), lambda bi, i: (bi, 0, 0)),
                pl.BlockSpec((1, 1, d), lambda bi, i: (bi, 0, 0)),
                _const_spec((d, n))]
    args = [x, g.reshape(1, d), sh, sc, w]
    if rope_cols:
        in_specs += [pl.BlockSpec((tm, LANE), lambda bi, i: (i, 0))] * 4
        args += list(rope)
    return pl.pallas_call(
        functools.partial(_proj_kernel, splits=tuple(splits), rope_cols=rope_cols),
        grid=(b, l // tm),
        in_specs=in_specs,
        out_specs=[pl.BlockSpec((1, tm, s1 - s0), row) for s0, s1 in splits],
        out_shape=[jax.ShapeDtypeStruct((b, l, s1 - s0), dt) for (s0, s1), dt in zip(splits, dtypes)],
        compiler_params=_cparams("parallel", "parallel"),
        name="norm_mod_proj",
    )(*args)


CONV_HALO = 16


def _conformer_kernel(ac_ref, gc_ref, ap_ref, gp_ref, an_ref, gn_ref, w_ref, b_ref, lg_ref, lb_ref, o_ref,
                      u_ref, sh_ref, y_ref, *, tl, ch):
    i = pl.program_id(1)
    last = pl.num_programs(1) - 1
    left = (CONV_W - 1) // 2
    glu = lambda a, g: a * jax.nn.sigmoid(g)
    u_ref[CONV_HALO:CONV_HALO + tl, :] = glu(ac_ref[0], gc_ref[0])
    prev = glu(ap_ref[0], gp_ref[0])
    nxt = glu(an_ref[0], gn_ref[0])
    u_ref[0:CONV_HALO, :] = jnp.where(i > 0, prev, 0.0)
    u_ref[CONV_HALO + tl:CONV_HALO + tl + CONV_HALO, :] = jnp.where(i < last, nxt, 0.0)
    n = sh_ref.shape[1]
    for p in range(SUBLANE):
        sh_ref[p] = u_ref[p:p + n, :]
    rows = min(tl, 128)
    for r0 in range(0, tl, rows):
        for c0 in range(0, ch, LANE):
            acc = jnp.zeros((rows, LANE), F32) + b_ref[:, c0:c0 + LANE]
            for k in range(CONV_W):
                s = CONV_HALO - left + k + r0
                p, base = s % SUBLANE, s - s % SUBLANE
                acc = acc + w_ref[k:k + 1, c0:c0 + LANE] * sh_ref[p, base:base + rows, c0:c0 + LANE]
            y_ref[r0:r0 + rows, c0:c0 + LANE] = acc
    y = y_ref[...]
    mu = jnp.mean(y, axis=-1, keepdims=True)
    yc = y - mu
    var = jnp.mean(yc * yc, axis=-1, keepdims=True)
    z = yc * lax.rsqrt(var + EPS) * lg_ref[...] + lb_ref[...]
    o_ref[0] = _silu(z).astype(o_ref.dtype)


def conformer_conv(p, conv_w, conv_b, ln_g, ln_b):
    b, l, _ = p.shape
    ch = conv_w.shape[1]
    tl = _tile(l, 256, CONV_HALO)
    nh = tl // CONV_HALO
    nblk = l // CONV_HALO
    cur = lambda col: pl.BlockSpec((1, tl, ch), lambda bi, i: (bi, i, col))
    prv = lambda col: pl.BlockSpec((1, CONV_HALO, ch), lambda bi, i: (bi, jnp.maximum(i * nh - 1, 0), col))
    nxt = lambda col: pl.BlockSpec((1, CONV_HALO, ch), lambda bi, i: (bi, jnp.minimum((i + 1) * nh, nblk - 1), col))
    return pl.pallas_call(
        functools.partial(_conformer_kernel, tl=tl, ch=ch),
        grid=(b, l // tl),
        in_specs=[cur(0), cur(1), prv(0), prv(1), nxt(0), nxt(1),
                  _const_spec((CONV_W, ch)), _const_spec((1, ch)), _const_spec((1, ch)), _const_spec((1, ch))],
        out_specs=pl.BlockSpec((1, tl, ch), lambda bi, i: (bi, i, 0)),
        out_shape=jax.ShapeDtypeStruct((b, l, ch), BF16),
        scratch_shapes=[pltpu.VMEM((tl + 2 * CONV_HALO, ch), F32),
                        pltpu.VMEM((SUBLANE, tl + 2 * CONV_HALO - SUBLANE, ch), F32),
                        pltpu.VMEM((tl, ch), F32)],
        compiler_params=_cparams("parallel", "parallel"),
        name="conformer_conv",
    )(p, p, p, p, p, p, conv_w, conv_b.reshape(1, ch), ln_g.reshape(1, ch), ln_b.reshape(1, ch))


ATTN_QBLK = 256


ATTN_ROWS = 128
ATTN_ONES = 16
ATTN_REDO = 120.0
ATTN_REF_ROW = (DIFF_DQK, 0)


def _round_bf16(x):
    return x.astype(BF16).astype(F32)


def _diff_attn_kernel(qt_ref, k_ref, vt_ref, lam_ref, g_ref, o_ref, qm_ref, m_ref, acc_ref, *, lambda_init):
    kv = pl.program_id(3)
    tq = qm_ref.shape[2]
    qblk = min(ATTN_QBLK, tq)
    chains = [(mi, slice(q0, q0 + qblk)) for q0 in range(0, tq, qblk) for mi in range(2)]

    k = k_ref[0]
    lane = lax.broadcasted_iota(jnp.int32, k.shape, 1)
    kx = [jnp.where(lane == r, jnp.ones_like(k), k) for r in ATTN_REF_ROW]
    vtx = vt_ref[0, 0]

    def scores(mi, qs, rows=None):
        keys = kx[mi] if rows is None else kx[mi][:rows]
        return jnp.dot(keys, qm_ref[mi, :, qs].astype(BF16), preferred_element_type=F32)

    def move_reference(mi, qs, m_new):
        m_ref[mi, :, qs] = m_new
        r = ATTN_REF_ROW[mi]
        qm_ref[mi, r:r + 1, qs] = -m_new

    @pl.when(kv == 0)
    def _():
        qt = qt_ref[0].astype(F32)
        row = lax.broadcasted_iota(jnp.int32, qt.shape, 0)
        qm_ref[0] = jnp.where(row < DIFF_DQK, qt, 0.0)
        qm_ref[1] = jnp.where(row < DIFF_DQK, 0.0, qt)
        acc_ref[...] = jnp.zeros(acc_ref.shape, F32)
        first = [scores(mi, qs, ATTN_ROWS) for mi, qs in chains]
        for (mi, qs), s in zip(chains, first):
            move_reference(mi, qs, _round_bf16(jnp.max(s, axis=0, keepdims=True)))

    sc = [scores(mi, qs) for mi, qs in chains]
    mxs, pvs = [], []
    for s in sc:
        es, mx = [], None
        for r0 in range(0, s.shape[0], ATTN_ROWS):
            grp = s[r0:r0 + ATTN_ROWS]
            es.append(jnp.exp2(grp).astype(BF16))
            top = jnp.max(grp, axis=0, keepdims=True)
            mx = top if mx is None else jnp.maximum(mx, top)
        mxs.append(mx)
        pvs.append(jnp.dot(vtx, jnp.concatenate(es, axis=0), preferred_element_type=F32))
    redo = jnp.max(functools.reduce(jnp.maximum, mxs)) > ATTN_REDO

    @pl.when(jnp.logical_not(redo))
    def _():
        for (mi, qs), mx, pv in zip(chains, mxs, pvs):
            m_old = m_ref[mi, :, qs]
            m_new = _round_bf16(m_old + jnp.maximum(mx, 0.0))
            acc_ref[mi, :, qs] = (acc_ref[mi, :, qs] + pv) * jnp.exp2(m_old - m_new)
            move_reference(mi, qs, m_new)

    @pl.when(redo)
    def _():
        for (mi, qs), s in zip(chains, sc):
            m_old = m_ref[mi, :, qs]
            m_new = _round_bf16(m_old + jnp.maximum(jnp.max(s, axis=0, keepdims=True), 0.0))
            delta = m_new - m_old
            e = jnp.exp2(s - delta)
            acc_ref[mi, :, qs] = jnp.exp2(-delta) * acc_ref[mi, :, qs] + jnp.dot(vtx, e.astype(BF16),
                                                                                 preferred_element_type=F32)
            move_reference(mi, qs, m_new)

    @pl.when(kv == pl.num_programs(3) - 1)
    def _():
        lam = (jnp.exp(jnp.sum(lam_ref[0:1, :] * lam_ref[1:2, :], axis=-1, keepdims=True))
               - jnp.exp(jnp.sum(lam_ref[2:3, :] * lam_ref[3:4, :], axis=-1, keepdims=True)) + lambda_init)
        dv = DIFF_DV
        ot = (acc_ref[0, :dv, :] / acc_ref[0, dv:dv + 1, :]
              - lam * (acc_ref[1, :dv, :] / acc_ref[1, dv:dv + 1, :]))
        ot = ot * lax.rsqrt(jnp.mean(ot * ot, axis=0, keepdims=True) + EPS)
        o = ot.T * g_ref[...]
        o_ref[0] = (o * (1.0 - lambda_init)).astype(o_ref.dtype)


def diff_attention(qt, k, vt, lam_vecs, subln_g, lambda_init, k_blk):
    b, _, lq = qt.shape
    lk = k.shape[1]
    tq = _tile(lq, 2048, LANE)
    tk = _tile(lk, 1536, LANE)
    rows = vt.shape[2]
    return pl.pallas_call(
        functools.partial(_diff_attn_kernel, lambda_init=lambda_init),
        grid=(b, DIFF_HEADS, lq // tq, lk // tk),
        in_specs=[pl.BlockSpec((1, LANE, tq), lambda bi, h, i, j: (bi, h, i)),
                  pl.BlockSpec((1, tk, LANE), lambda bi, h, i, j: (bi, j, k_blk + h)),
                  pl.BlockSpec((1, 1, rows, tk), lambda bi, h, i, j: (bi, h, 0, j)),
                  _const_spec((4, DIFF_DQK)), _const_spec((1, DIFF_DV))],
        out_specs=pl.BlockSpec((1, tq, LANE), lambda bi, h, i, j: (bi, i, h)),
        out_shape=jax.ShapeDtypeStruct((b, lq, DIFF_HEADS * DIFF_DV), BF16),
        scratch_shapes=[pltpu.VMEM((2, LANE, tq), F32), pltpu.VMEM((2, 1, tq), F32),
                        pltpu.VMEM((2, rows, tq), F32)],
        compiler_params=_cparams("parallel", "parallel", "parallel", "arbitrary"),
        name="diff_attention",
    )(qt, k, vt, lam_vecs, subln_g.reshape(1, DIFF_DV))


def _delta_gate(of_ref, ob_ref, gate_ref, g_ref):
    o = of_ref[0] + ob_ref[0]
    parts = []
    for c0 in range(0, o.shape[1], DN_DK):
        t = o[:, c0:c0 + DN_DK]
        t = t * lax.rsqrt(jnp.mean(t * t, axis=-1, keepdims=True) + EPS) * g_ref[...]
        parts.append((t * _silu(gate_ref[0, :, c0:c0 + DN_DK])).astype(BF16))
    return jnp.concatenate(parts, axis=1)


def _out_mlp_kernel(x_ref, a_ref, *refs, a_transposed, hidden_chunk, final_norm, delta_b):
    if delta_b:
        mix_b = _delta_gate(*refs[:4])
        refs = refs[4:]
    else:
        mix_b = refs[0][0]
        refs = refs[1:]
    wo_ref, gt1_ref, g2_ref, sh2_ref, sc2_ref, gt2_ref, w1_ref, w2_ref, fg_ref, o_ref = refs
    half = wo_ref.shape[0] // 2
    if a_transposed:
        ya = lax.dot_general(a_ref[0].astype(BF16), wo_ref[0:half, :], (((0,), (0,)), ((), ())),
                             preferred_element_type=F32)
    else:
        ya = jnp.dot(a_ref[0], wo_ref[0:half, :], preferred_element_type=F32)
    y = ya + jnp.dot(mix_b, wo_ref[half:, :], preferred_element_type=F32)
    x1 = x_ref[0] + gt1_ref[0] * y
    h = _norm_mod(x1, g2_ref[...], sh2_ref[0], sc2_ref[0]).astype(BF16)
    acc = jnp.zeros_like(x1)
    for c0 in range(0, w1_ref.shape[1], hidden_chunk):
        t = jnp.maximum(jnp.dot(h, w1_ref[:, c0:c0 + hidden_chunk], preferred_element_type=F32), 0.0)
        acc = acc + jnp.dot((t * t).astype(BF16), w2_ref[c0:c0 + hidden_chunk, :], preferred_element_type=F32)
    x2 = x1 + gt2_ref[0] * acc
    if final_norm:
        x2 = x2 * lax.rsqrt(jnp.mean(x2 * x2, axis=-1, keepdims=True) + EPS) * fg_ref[...]
    o_ref[0] = x2


def out_proj_mlp(x, mix_a, mix_b, w_out, gt1, g2, sh2, sc2, gt2, w1, w2, final_g=None, a_transposed=False):
    b, l, d = x.shape
    delta_b = isinstance(mix_b, tuple)
    c = (mix_b[0] if delta_b else mix_b).shape[2]
    hid = w1.shape[1]
    tm = _tile(l, 512, LANE)
    row = lambda bi, i: (bi, i, 0)
    vec = lambda bi, i: (bi, 0, 0)
    a_spec = (pl.BlockSpec((1, c, tm), lambda bi, i: (bi, 0, i)) if a_transposed
              else pl.BlockSpec((1, tm, c), row))
    if delta_b:
        b_args = list(mix_b[:3]) + [mix_b[3].reshape(1, DN_DK)]
        b_specs = [pl.BlockSpec((1, tm, c), row)] * 3 + [_const_spec((1, DN_DK))]
    else:
        b_args, b_specs = [mix_b], [pl.BlockSpec((1, tm, c), row)]
    fg = jnp.ones((1, d), F32) if final_g is None else final_g.reshape(1, d)
    return pl.pallas_call(
        functools.partial(_out_mlp_kernel, a_transposed=a_transposed, hidden_chunk=min(hid, 1024),
                          final_norm=final_g is not None, delta_b=delta_b),
        grid=(b, l // tm),
        in_specs=[pl.BlockSpec((1, tm, d), row), a_spec] + b_specs + [
                  _const_spec((2 * c, d)),
                  pl.BlockSpec((1, 1, d), vec), _const_spec((1, d)),
                  pl.BlockSpec((1, 1, d), vec), pl.BlockSpec((1, 1, d), vec), pl.BlockSpec((1, 1, d), vec),
                  _const_spec((d, hid)), _const_spec((hid, d)), _const_spec((1, d))],
        out_specs=pl.BlockSpec((1, tm, d), row),
        out_shape=jax.ShapeDtypeStruct((b, l, d), F32),
        compiler_params=_cparams("parallel", "parallel"),
        name="out_proj_mlp",
    )(x, mix_a, *b_args, w_out, gt1, g2.reshape(1, d), sh2, sc2, gt2, w1, w2, fg)


def _rope_tables(length, scale):
    f32 = np.float32
    rows = length // GRID_W
    row = np.repeat(np.arange(rows, dtype=f32), GRID_W)
    col = np.tile(np.arange(GRID_W, dtype=f32), rows)
    n_freq = DIFF_DQK // 4
    inv = (f32(ROPE_BASE) ** (-np.arange(n_freq, dtype=f32) / f32(n_freq))).astype(f32)
    ang_r = row[:, None] * inv
    ang_c = col[:, None] * inv
    cos = np.concatenate([np.cos(ang_r)] * 2 + [np.cos(ang_c)] * 2, axis=-1)
    sin = np.concatenate([-np.sin(ang_r), np.sin(ang_r), -np.sin(ang_c), np.sin(ang_c)], axis=-1)
    cos = np.concatenate([cos, cos], axis=-1) * f32(scale)
    sin = np.concatenate([sin, sin], axis=-1) * f32(scale)
    return cos.astype(f32), sin.astype(f32)


def _flat_tables(length, scale):
    return np.full((length, LANE), scale, np.float32), np.zeros((length, LANE), np.float32)


def _proj_t_kernel(x_ref, g_ref, sh_ref, sc_ref, wt_ref, o_ref, *, rc):
    h = _norm_mod(x_ref[0], g_ref[...], sh_ref[0], sc_ref[0]).astype(BF16)
    for r0 in range(0, wt_ref.shape[0], rc):
        o_ref[0, r0:r0 + rc, :] = lax.dot_general(wt_ref[r0:r0 + rc, :], h, (((1,), (1,)), ((), ())),
                                                  preferred_element_type=F32)


def norm_mod_proj_t(x, g, sh, sc, wt):
    b, l, d = x.shape
    c = wt.shape[0]
    tl = _tile(l, 512, LANE)
    return pl.pallas_call(
        functools.partial(_proj_t_kernel, rc=_tile(c, 512)),
        grid=(b, l // tl),
        in_specs=[pl.BlockSpec((1, tl, d), lambda bi, i: (bi, i, 0)),
                  _const_spec((1, d)),
                  pl.BlockSpec((1, 1, d), lambda bi, i: (bi, 0, 0)),
                  pl.BlockSpec((1, 1, d), lambda bi, i: (bi, 0, 0)),
                  _const_spec((c, d))],
        out_specs=pl.BlockSpec((1, c, tl), lambda bi, i: (bi, 0, i)),
        out_shape=jax.ShapeDtypeStruct((b, c, l), F32),
        compiler_params=_cparams("parallel", "parallel"),
        name="norm_mod_proj_t",
    )(x, g.reshape(1, d), sh, sc, wt)


def _dot_hi(a, b):
    return jnp.dot(a, b, preferred_element_type=F32, precision=lax.Precision.HIGHEST)


def _hy_hidden_kernel(z_ref, w1_ref, b1_ref, f1_ref, w2_ref, b2_ref, f2_ref, o_ref):
    hid = jnp.sin(f1_ref[...] * (_dot_hi(w1_ref[...], z_ref[...]) + b1_ref[...]))
    o_ref[...] = jnp.sin(f2_ref[...] * (_dot_hi(w2_ref[...], hid) + b2_ref[...]))


def hyena_hidden(zt, w1t, b1, f1, w2t, b2, f2):
    e, n = zt.shape
    f = w1t.shape[0]
    tn = _tile(n, 2048, LANE)
    col = lambda v: v.reshape(f, 1)
    return pl.pallas_call(
        _hy_hidden_kernel,
        grid=(n // tn,),
        in_specs=[pl.BlockSpec((e, tn), lambda j: (0, j)),
                  _const_spec((f, e)), _const_spec((f, 1)), _const_spec((f, 1)),
                  _const_spec((f, f)), _const_spec((f, 1)), _const_spec((f, 1))],
        out_specs=pl.BlockSpec((f, tn), lambda j: (0, j)),
        out_shape=jax.ShapeDtypeStruct((f, n), F32),
        compiler_params=_cparams("parallel"),
        name="hyena_hidden",
    )(zt, w1t, col(b1), col(f1), w2t, col(b2), col(f2))


def _hy_filter_kernel(hid_ref, t_ref, wf_ref, wb_ref, dl_ref, o_ref, *, length, tn):
    n2 = hid_ref.shape[1]
    delta = dl_ref[...]

    def piece(c0):
        hid = hid_ref[:, c0:c0 + tn]
        dec = jnp.exp(-t_ref[:, c0:c0 + tn] * delta)
        return _dot_hi(wf_ref[...], hid) * dec, _dot_hi(wb_ref[...], hid) * dec

    norm = jnp.zeros((wf_ref.shape[0], 1), F32)
    for c0 in range(length, n2, tn):
        ff, fb = piece(c0)
        norm = norm + jnp.sum(jnp.abs(ff) + jnp.abs(fb), axis=-1, keepdims=True)
    inv = 1.0 / norm
    for c0 in range(0, n2, tn):
        ff, fb = piece(c0)
        if c0 >= length:
            o_ref[0, :, c0:c0 + tn] = ff * inv
        else:
            col = lax.broadcasted_iota(jnp.int32, fb.shape, 1) + c0
            o_ref[0, :, c0:c0 + tn] = jnp.where(col == 0, 0.0, fb * inv)


def hyena_filters(hid, t_row, w3t, deltas, length):
    f, n2 = hid.shape
    ch = deltas.shape[0]
    cb = LANE
    nblk = ch // cb
    tn = _tile(length, 2048, LANE)
    return pl.pallas_call(
        functools.partial(_hy_filter_kernel, length=length, tn=tn),
        grid=(HY_ORDER, nblk),
        in_specs=[_const_spec((f, n2)), _const_spec((1, n2)),
                  pl.BlockSpec((cb, f), lambda o, c: (o * 2 * nblk + c, 0)),
                  pl.BlockSpec((cb, f), lambda o, c: (o * 2 * nblk + nblk + c, 0)),
                  pl.BlockSpec((cb, 1), lambda o, c: (c, 0))],
        out_specs=pl.BlockSpec((1, cb, n2), lambda o, c: (o, c, 0)),
        out_shape=jax.ShapeDtypeStruct((HY_ORDER, ch, n2), F32),
        compiler_params=_cparams("parallel", "parallel"),
        name="hyena_filters",
    )(hid, t_row, w3t, w3t, deltas.reshape(ch, 1))


HY_T = 256
HY_CB = SUBLANE

def _hyena_kernel(v_ref, x1_ref, x2_ref, g_ref, sw_ref, sb_ref, sk_ref_s, o_ref, sk_ref, u_ref, acc_ref,
                  *, length, nch):
    nb = v_ref.shape[0]
    nj = length // HY_T
    cblk = pl.program_id(0)
    lane = lax.broadcasted_iota(jnp.int32, (nb, length), 1)

    def short(ref, cc, part):
        x = ref[:, cc, :]
        ch = part * nch + cblk * HY_CB + cc
        prev = jnp.where(lane == 0, 0.0, pltpu.roll(x, 1, 1))
        nxt = jnp.where(lane == length - 1, 0.0, pltpu.roll(x, length - 1, 1))
        return sw_ref[0, ch] * prev + sw_ref[1, ch] * x + sw_ref[2, ch] * nxt + sb_ref[ch]

    def long_conv(u, order, cc):
        g = jnp.broadcast_to(g_ref[order, pl.ds(cc, 1), :], (2 * SUBLANE, 2 * length))
        half = HY_T // 2
        for r0 in range(0, half, 2 * SUBLANE):
            pair = pltpu.roll(g, r0, 1, stride=1, stride_axis=0).astype(BF16)
            sk_ref[r0:r0 + 2 * SUBLANE, :] = pair
            sk_ref[half + r0:half + r0 + 2 * SUBLANE, half:] = pair[:, :2 * length - half]
        rows = nj * nb
        x2 = jnp.concatenate([u[:, j * HY_T:(j + 1) * HY_T] for j in range(nj)]
                             + [jnp.zeros((2 * nb, HY_T), F32)], axis=0)
        u_ref[0] = x2.astype(BF16)
        u_ref[1] = jnp.concatenate([x2[nb:, :], x2[rows:rows + nb, :]], axis=0).astype(BF16)
        acc_ref[...] = jnp.zeros(acc_ref.shape, F32)
        for d in range(-(nj - 1), nj):
            m = (nj - abs(d)) * nb
            lo_in, lo_out = max(0, -d) * nb, max(0, d) * nb
            which, off = (0, lo_in) if lo_in % (2 * nb) == 0 else (1, lo_in - nb)
            mm = -(-m // (2 * nb)) * (2 * nb)
            tile = sk_ref[:, length + d * HY_T:length + (d + 1) * HY_T]
            part = jnp.dot(u_ref[which, off:off + mm, :], tile, preferred_element_type=F32)
            acc_ref[lo_out:lo_out + m, :] += part[:m]
        return jnp.concatenate([acc_ref[i * nb:(i + 1) * nb, :] for i in range(nj)], axis=1)

    def channel(cc, carry):
        v = short(v_ref, cc, 0)
        x1 = short(x1_ref, cc, 1)
        x2 = short(x2_ref, cc, 2)
        ch = cblk * HY_CB + cc
        z = x1 * (long_conv(v, 0, cc) + sk_ref_s[0, ch] * v)
        z = x2 * (long_conv(z, 1, cc) + sk_ref_s[1, ch] * z)
        o_ref[:, cc, :] = z
        return carry

    lax.fori_loop(0, HY_CB, channel, 0)


def hyena_mix(pt, filt, short_w, short_b, skip):
    b, c3, l = pt.shape
    nch = c3 // 3
    assert l % HY_T == 0 and nch % HY_CB == 0 and b == SUBLANE
    nblk = nch // HY_CB
    smem = pl.BlockSpec(memory_space=pltpu.SMEM)
    part = lambda k: pl.BlockSpec((b, HY_CB, l), lambda c: (0, k * nblk + c, 0))
    return pl.pallas_call(
        functools.partial(_hyena_kernel, length=l, nch=nch),
        grid=(nblk,),
        in_specs=[part(0), part(1), part(2),
                  pl.BlockSpec((HY_ORDER, HY_CB, 2 * l), lambda c: (0, c, 0)),
                  smem, smem, smem],
        out_specs=pl.BlockSpec((b, HY_CB, l), lambda c: (0, c, 0)),
        out_shape=jax.ShapeDtypeStruct((b, nch, l), F32),
        scratch_shapes=[pltpu.VMEM((HY_T, 2 * l), BF16),
                        pltpu.VMEM((2, l // HY_T * b + 2 * b, HY_T), BF16),
                        pltpu.VMEM((l // HY_T * b, HY_T), F32)],
        compiler_params=_cparams("parallel"),
        name="hyena_mix",
    )(pt, pt, pt, filt, short_w, short_b, skip)


DN_HALO = SUBLANE
DN_BETA_F, DN_BETA_B, DN_G_F, DN_G_B = 0, DN_HEADS, 2 * DN_HEADS, 3 * DN_HEADS


def _dn_feat_kernel(c_ref, p_ref, n_ref, s_ref, w_ref, al_ref, dt_ref, q_ref, k_ref, v_ref, f_ref, u_ref, *, tl):
    i = pl.program_id(1)
    last = pl.num_programs(1) - 1
    left = (DN_CONV_W - 1) // 2
    nc = c_ref.shape[2]
    u_ref[DN_HALO:DN_HALO + tl, :] = c_ref[0]
    u_ref[0:DN_HALO, :] = jnp.where(i > 0, p_ref[0], 0.0)
    u_ref[DN_HALO + tl:DN_HALO + tl + DN_HALO, :] = jnp.where(i < last, n_ref[0], 0.0)
    rows = min(tl, 128)
    outs = (q_ref, k_ref, v_ref)
    per = nc // len(outs)
    for r0 in range(0, tl, rows):
        for c0 in range(0, nc, LANE):
            acc = jnp.zeros((rows, LANE), F32)
            for k in range(DN_CONV_W):
                s = DN_HALO - left + k + r0
                acc = acc + w_ref[k:k + 1, c0:c0 + LANE] * u_ref[s:s + rows, c0:c0 + LANE]
            y = _silu(acc)
            which, off = divmod(c0, per)
            if which < 2:
                y = y * lax.rsqrt(jnp.sum(y * y, axis=-1, keepdims=True) + 1e-6)
            outs[which][0, r0:r0 + rows, off:off + LANE] = y
    s = s_ref[0]
    lane = lax.broadcasted_iota(jnp.int32, s.shape, 1)
    gate = -jnp.exp(al_ref[...]) * jax.nn.softplus(s + dt_ref[...])
    f_ref[0] = jnp.where(lane < DN_G_F, jax.nn.sigmoid(s), gate)


def dn_features(qkv, small, conv_w, alog_row, dtb_row):
    b, l, nc = qkv.shape
    tl = _tile(l, 256, DN_HALO)
    nh = tl // DN_HALO
    nblk = l // DN_HALO
    w = nc // 3
    row = lambda bi, i: (bi, i, 0)
    return pl.pallas_call(
        functools.partial(_dn_feat_kernel, tl=tl),
        grid=(b, l // tl),
        in_specs=[pl.BlockSpec((1, tl, nc), row),
                  pl.BlockSpec((1, DN_HALO, nc), lambda bi, i: (bi, jnp.maximum(i * nh - 1, 0), 0)),
                  pl.BlockSpec((1, DN_HALO, nc), lambda bi, i: (bi, jnp.minimum((i + 1) * nh, nblk - 1), 0)),
                  pl.BlockSpec((1, tl, LANE), row),
                  _const_spec((DN_CONV_W, nc)), _const_spec((1, LANE)), _const_spec((1, LANE))],
        out_specs=[pl.BlockSpec((1, tl, w), row)] * 3 + [pl.BlockSpec((1, tl, LANE), row)],
        out_shape=[jax.ShapeDtypeStruct((b, l, w), F32)] * 3 + [jax.ShapeDtypeStruct((b, l, LANE), F32)],
        scratch_shapes=[pltpu.VMEM((tl + 2 * DN_HALO, nc), F32)],
        compiler_params=_cparams("parallel", "parallel"),
        name="dn_features",
    )(qkv, qkv, qkv, small, conv_w, alog_row, dtb_row)


def _bdot(a, b):
    return jnp.dot(a.astype(BF16), b.astype(BF16), preferred_element_type=F32)


def _bdot_nt(a, b):
    return lax.dot_general(a.astype(BF16), b.astype(BF16), (((1,), (1,)), ((), ())), preferred_element_type=F32)


def _bdot_tn(a, b):
    return lax.dot_general(a.astype(BF16), b.astype(BF16), (((0,), (0,)), ((), ())), preferred_element_type=F32)


def _cumsum_rows(x, reverse):
    n = x.shape[0]
    row = lax.broadcasted_iota(jnp.int32, x.shape, 0)
    s = 1
    while s < n:
        if reverse:
            x = x + jnp.where(row < n - s, pltpu.roll(x, n - s, 0), 0.0)
        else:
            x = x + jnp.where(row >= s, pltpu.roll(x, s, 0), 0.0)
        s *= 2
    return x


def _dn_chunks(chains):
    c = chains[0]["k"].shape[0]
    ri = lax.broadcasted_iota(jnp.int32, (c, c), 0)
    ci = lax.broadcasted_iota(jnp.int32, (c, c), 1)
    work = []
    for ch in chains:
        rev = ch["reverse"]
        incl = (ri <= ci) if rev else (ri >= ci)
        strict = (ri < ci) if rev else (ri > ci)
        decay = jnp.where(incl, jnp.exp(jnp.where(incl, ch["gcol"] - ch["grow"], 0.0)), 0.0)
        kb = ch["k"] * ch["beta"]
        qs = ch["q"] * (ch["k"].shape[1] ** -0.5)
        work.append(dict(ch, strict=strict, decay=decay, kb=kb, qs=qs))
    grams = [_bdot_nt(jnp.concatenate([w["kb"], w["qs"]], axis=0), w["k"]) for w in work]
    lowers = [jnp.where(w["strict"], g[:c] * w["decay"], 0.0) for w, g in zip(work, grams)]
    same2 = (ri // 2) == (ci // 2)
    ymats = [-jnp.where(same2, lo, 0.0) for lo in lowers]
    m = 2
    while m < c:
        pair = ((ri // (2 * m)) == (ci // (2 * m))) & ((ri // m) != (ci // m))
        offs = [jnp.where(pair, lo, 0.0) for lo in lowers]
        zs = [off + _bdot(ym, off) for ym, off in zip(ymats, offs)]
        ymats = [ym - z - _bdot(z, ym) for ym, z in zip(ymats, zs)]
        m *= 2
    outs = []
    uws, e_gs = [], []
    for w, ym in zip(work, ymats):
        e_g = jnp.exp(w["gcol"])
        rhs = jnp.concatenate([w["v"] * w["beta"], w["kb"] * e_g], axis=1)
        uws.append(rhs + _bdot(ym, rhs))
        e_gs.append(e_g)
    dv = work[0]["v"].shape[1]
    wss = [_bdot(jnp.concatenate([uw[:, dv:], w["qs"] * e_g], axis=0), w["state"])
           for w, uw, e_g in zip(work, uws, e_gs)]
    v_news = [uw[:, :dv] - ws[:c] for uw, ws in zip(uws, wss)]
    for w, g, ws, v_new in zip(work, grams, wss, v_news):
        gcol = w["gcol"]
        g_last = gcol[0:1, :] if w["reverse"] else gcol[c - 1:c, :]
        k_dec = w["k"] * jnp.exp(g_last - gcol)
        o = ws[c:] + _bdot(g[c:] * w["decay"], v_new)
        new_state = w["state"] * jnp.exp(g_last) + _bdot_tn(k_dec, v_new)
        outs.append((o, new_state))
    return outs


def _dn_scan_kernel(qf_ref, kf_ref, vf_ref, ff_ref, qb_ref, kb_ref, vb_ref, fb_ref, s0f_ref, s0b_ref,
                    of_ref, ob_ref, sf_ref, sb_ref, st_ref):
    n = pl.program_id(1)

    @pl.when(n == 0)
    def _():
        st_ref[0] = s0f_ref[0]
        st_ref[1] = s0b_ref[0]

    dirs = ((qf_ref, kf_ref, vf_ref, ff_ref, of_ref, False), (qb_ref, kb_ref, vb_ref, fb_ref, ob_ref, True))
    chains, dests = [], []
    for di, (q_ref, k_ref, v_ref, f_ref, o_ref, reverse) in enumerate(dirs):
        feats = f_ref[0]
        gc = _cumsum_rows(feats, reverse)
        gct = gc.T
        for h in range(DN_HEADS):
            cols = slice(h * DN_DK, (h + 1) * DN_DK)
            bl = (DN_BETA_B if reverse else DN_BETA_F) + h
            gl = (DN_G_B if reverse else DN_G_F) + h
            chains.append(dict(q=q_ref[0, :, cols], k=k_ref[0, :, cols], v=v_ref[0, :, cols],
                               beta=feats[:, bl:bl + 1], gcol=gc[:, gl:gl + 1], grow=gct[gl:gl + 1, :],
                               state=st_ref[di, h], reverse=reverse))
            dests.append((o_ref, cols, di, h))
    for (o_ref, cols, di, h), (o, s_new) in zip(dests, _dn_chunks(chains)):
        o_ref[0, :, cols] = o
        st_ref[di, h] = s_new

    @pl.when(n == pl.num_programs(1) - 1)
    def _():
        sf_ref[0] = st_ref[0]
        sb_ref[0] = st_ref[1]


def dn_scan(q, k, v, feats, s0f, s0b):
    b, l, w = q.shape
    nchunk = l // DN_CHUNK
    fwd = lambda bi, n: (bi, n, 0)
    bwd = lambda bi, n: (bi, nchunk - 1 - n, 0)
    st = lambda bi, n: (bi, 0, 0, 0)
    blk = lambda im, width: pl.BlockSpec((1, DN_CHUNK, width), im)
    st_spec = pl.BlockSpec((1, DN_HEADS, DN_DK, DN_DK), st)
    st_shape = jax.ShapeDtypeStruct((b, DN_HEADS, DN_DK, DN_DK), F32)
    return pl.pallas_call(
        _dn_scan_kernel,
        grid=(b, nchunk),
        in_specs=[blk(fwd, w), blk(fwd, w), blk(fwd, w), blk(fwd, LANE),
                  blk(bwd, w), blk(bwd, w), blk(bwd, w), blk(bwd, LANE), st_spec, st_spec],
        out_specs=[blk(fwd, w), blk(bwd, w), st_spec, st_spec],
        out_shape=[jax.ShapeDtypeStruct((b, l, w), F32)] * 2 + [st_shape, st_shape],
        scratch_shapes=[pltpu.VMEM((2, DN_HEADS, DN_DK, DN_DK), F32)],
        compiler_params=_cparams("parallel", "arbitrary"),
        name="dn_scan",
    )(q, k, v, feats, q, k, v, feats, s0f, s0b)


def _split_mods(mods, b, d):
    lat = [mods[:b, k * d:(k + 1) * d].reshape(b, 1, d) for k in range(N_MOD)]
    ctx = [jnp.broadcast_to(mods[b, k * d:(k + 1) * d].reshape(1, 1, d), (b, 1, d)) for k in range(N_MOD)]
    return lat, ctx


def _even_layer(x, ctx, lat_m, ctx_m, n1g, n2g, w_in, conv_w, conv_b, ln_g, ln_b, lam_vecs, subln_g, w_out,
                w1, w2, lambda_init):
    b, l, d = x.shape
    lc = ctx.shape[1]
    ch = conv_w.shape[1]
    qk = DIFF_HEADS * 2 * DIFF_DQK
    q0, k0, v0, n_in = 2 * ch, 2 * ch + qk, 2 * ch + 2 * qk, 2 * ch + 2 * qk + DIFF_HEADS * DIFF_DV
    scale = DIFF_DQK ** -0.5 * math.log2(math.e)
    splits = [(0, q0), (q0, n_in)]
    w_in = w_in.astype(BF16)
    rope_lat = _rope_tables(l, scale) + _rope_tables(l, 1.0)
    rope_ctx = _flat_tables(lc, scale) + _flat_tables(lc, 1.0)
    pc_lat, qkv_lat = norm_mod_proj(x, n1g, lat_m[0], lat_m[1], w_in, splits, (F32, BF16), rope_lat, (q0, k0, v0))
    pc_ctx, qkv_ctx = norm_mod_proj(ctx, n1g, ctx_m[0], ctx_m[1], w_in, splits, (F32, BF16), rope_ctx, (q0, k0, v0))
    conv_lat = conformer_conv(pc_lat, conv_w, conv_b, ln_g, ln_b)
    conv_ctx = conformer_conv(pc_ctx, conv_w, conv_b, ln_g, ln_b)
    k_all = jnp.concatenate([qkv_lat[:, :, qk:2 * qk], qkv_ctx[:, :, qk:2 * qk]], axis=1)
    def heads_t(v):
        vt = jnp.swapaxes(v, 1, 2).reshape(b, DIFF_HEADS, DIFF_DV, v.shape[1])
        return jnp.concatenate([vt, jnp.ones((b, DIFF_HEADS, ATTN_ONES, v.shape[1]), vt.dtype)], axis=2)

    vt_ctx = heads_t(qkv_ctx[:, :, 2 * qk:])
    vt_all = jnp.concatenate([heads_t(qkv_lat[:, :, 2 * qk:]), vt_ctx], axis=3)
    qt_lat = jnp.swapaxes(qkv_lat[:, :, :qk], 1, 2)
    qt_ctx = jnp.swapaxes(qkv_ctx[:, :, :qk], 1, 2)
    o_lat = diff_attention(qt_lat, k_all, vt_all, lam_vecs, subln_g, lambda_init, 0)
    o_ctx = diff_attention(qt_ctx, qkv_ctx, vt_ctx, lam_vecs, subln_g, lambda_init, qk // LANE)
    w_out, w1, w2 = w_out.astype(BF16), w1.astype(BF16), w2.astype(BF16)
    x = out_proj_mlp(x, conv_lat, o_lat, w_out, lat_m[2], n2g, lat_m[3], lat_m[4], lat_m[5], w1, w2)
    ctx = out_proj_mlp(ctx, conv_ctx, o_ctx, w_out, ctx_m[2], n2g, ctx_m[3], ctx_m[4], ctx_m[5], w1, w2)
    return x, ctx


def _hyena_position_tables(length):
    f32 = np.float32
    t = np.linspace(0.0, 1.0, length, dtype=f32)[:, None]
    bands = (HY_EMB - 1) // 2
    omega = (f32(2.0 * math.pi) * np.arange(length, dtype=f32)[:, None] / f32(length)).astype(f32)
    ang = omega * np.linspace(1e-4, bands - 1, bands, dtype=f32)
    z = np.concatenate([t, np.cos(ang), -np.sin(ang)], axis=-1).astype(f32)
    pos = np.minimum(np.abs(np.arange(2 * length) - length), length - 1)
    pad = -HY_EMB % SUBLANE
    zt = np.pad(z[pos].T, ((0, pad), (0, 0)))
    return np.ascontiguousarray(zt), np.ascontiguousarray(t[pos].T)


def _odd_layer_last(x, ctx, lat_m, ctx_m, n1g, n2g, w_in, hy_short_w, hy_short_b, hy_w1, hy_b1, hy_f1, hy_w2, hy_b2,
                    hy_f2, hy_w3, hy_skip, dn_conv_w, alog_f, alog_b, dtb_f, dtb_b, dn_norm_g, w_out, w1, w2,
                    final_g):
    b, l, d = x.shape
    hy_in = hy_short_w.shape[1]
    hy_ch = hy_in // (HY_ORDER + 1)
    dn_w = DN_HEADS * DN_DK
    n_rest = w_in.shape[1] - hy_in
    n_pad = -n_rest % LANE
    w_rest = jnp.pad(w_in[:, hy_in:], ((0, 0), (0, n_pad))).astype(BF16)
    splits = [(0, dn_w), (dn_w, 4 * dn_w), (4 * dn_w, n_rest + n_pad)]
    f32x3 = (F32, F32, F32)
    gate_lat, qkv_lat, small_lat = norm_mod_proj(x, n1g, lat_m[0], lat_m[1], w_rest, splits, f32x3)
    _, qkv_ctx, small_ctx = norm_mod_proj(ctx, n1g, ctx_m[0], ctx_m[1], w_rest, splits, f32x3)

    pt = norm_mod_proj_t(x, n1g, lat_m[0], lat_m[1], w_in[:, :hy_in].T.astype(BF16))
    zt, t_row = _hyena_position_tables(l)
    w1t = jnp.pad(hy_w1.T, ((0, 0), (0, zt.shape[0] - hy_w1.shape[0])))
    hid = hyena_hidden(zt, w1t, hy_b1, hy_f1, hy_w2.T, hy_b2, hy_f2)
    deltas = jnp.abs(jnp.linspace(HY_MIN_DECAY, HY_MAX_DECAY, hy_ch, dtype=F32))
    filt = hyena_filters(hid, t_row, hy_w3.T, deltas, l)
    hy = hyena_mix(pt, filt, hy_short_w, hy_short_b, hy_skip)

    lanes = jnp.zeros((LANE,), F32)
    alog_row = lanes.at[DN_G_F:DN_G_F + DN_HEADS].set(alog_f).at[DN_G_B:DN_G_B + DN_HEADS].set(alog_b).reshape(1, LANE)
    dtb_row = lanes.at[DN_G_F:DN_G_F + DN_HEADS].set(dtb_f).at[DN_G_B:DN_G_B + DN_HEADS].set(dtb_b).reshape(1, LANE)
    q_c, k_c, v_c, f_c = dn_features(qkv_ctx, small_ctx, dn_conv_w, alog_row, dtb_row)
    q_l, k_l, v_l, f_l = dn_features(qkv_lat, small_lat, dn_conv_w, alog_row, dtb_row)
    s0 = jnp.zeros((b, DN_HEADS, DN_DK, DN_DK), F32)
    _, _, s_cf, s_cb = dn_scan(q_c, k_c, v_c, f_c, s0, s0)
    o_f, o_b, _, _ = dn_scan(q_l, k_l, v_l, f_l, s_cf, s_cb)
    dn = (o_f, o_b, gate_lat, dn_norm_g)

    return out_proj_mlp(x, hy, dn, w_out.astype(BF16), lat_m[2], n2g, lat_m[3], lat_m[4], lat_m[5],
                        w1.astype(BF16), w2.astype(BF16), final_g=final_g, a_transposed=True)


def kernel(x, c, ctx, c_ctx, ada_w, ada_b, norm1_g, norm2_g, mlp_w1, mlp_w2, ev_w_in, ev_conv_w, ev_conv_b, ev_ln_g,
           ev_ln_b, ev_lq1, ev_lk1, ev_lq2, ev_lk2, ev_subln_g, ev_w_out, od_w_in, od_hy_short_w, od_hy_short_b,
           od_hy_w1, od_hy_b1, od_hy_freq1, od_hy_w2, od_hy_b2, od_hy_freq2, od_hy_w3, od_hy_skip, od_dn_conv_w,
           od_dn_alog_f, od_dn_alog_b, od_dn_dtb_f, od_dn_dtb_b, od_dn_norm_g, od_w_out, final_g):
    b, _, d = x.shape
    assert ada_w.shape[0] == 2, "layer 0 = conformer/diff-attention, layer 1 (last) = Hyena/DeltaNet"
    rows = -(-(b + 1) // SUBLANE) * SUBLANE
    cvec = jnp.zeros((rows, d), F32).at[:b].set(c).at[b].set(c_ctx)

    lat_m, ctx_m = _split_mods(ada_mods(cvec, ada_w[0], ada_b[0]), b, d)
    lam_vecs = jnp.stack([ev_lq1[0], ev_lk1[0], ev_lq2[0], ev_lk2[0]])
    x, ctx = _even_layer(x, ctx, lat_m, ctx_m, norm1_g[0], norm2_g[0], ev_w_in[0], ev_conv_w[0], ev_conv_b[0],
                         ev_ln_g[0], ev_ln_b[0], lam_vecs, ev_subln_g[0], ev_w_out[0], mlp_w1[0], mlp_w2[0],
                         0.8 - 0.6 * math.exp(-0.3 * 0))

    lat_m, ctx_m = _split_mods(ada_mods(cvec, ada_w[1], ada_b[1]), b, d)
    return _odd_layer_last(x, ctx, lat_m, ctx_m, norm1_g[1], norm2_g[1], od_w_in[0], od_hy_short_w[0],
                           od_hy_short_b[0], od_hy_w1[0], od_hy_b1[0], od_hy_freq1[0], od_hy_w2[0], od_hy_b2[0],
                           od_hy_freq2[0], od_hy_w3[0], od_hy_skip[0], od_dn_conv_w[0], od_dn_alog_f[0],
                           od_dn_alog_b[0], od_dn_dtb_f[0], od_dn_dtb_b[0], od_dn_norm_g[0], od_w_out[0],
                           mlp_w1[1], mlp_w2[1], final_g)
```

```python
import functools
import math

import jax
import jax.numpy as jnp
import numpy as np
from jax import lax
from jax.experimental import pallas as pl
from jax.experimental.pallas import tpu as pltpu

F32 = jnp.float32
BF16 = jnp.bfloat16

EPS = 1e-6
N_MOD = 6
GRID_W = 64
ROPE_BASE = 10000.0
CONV_W = 31
DIFF_HEADS = 4
DIFF_DQK = 64
DIFF_DV = 128
HY_ORDER = 2
HY_EMB = 33
HY_FFN = 64
HY_MAX_DECAY = math.log(1e-2) / 0.3
HY_MIN_DECAY = math.log(1e-2) / 1.5
DN_HEADS = 4
DN_DK = 128
DN_CONV_W = 5
DN_BATCH = 2
DN_CHUNK = 128

LANE = 128
SUBLANE = 8
VMEM_LIMIT = 56 * 1024 * 1024


def _cparams(*sem):
    return pltpu.CompilerParams(dimension_semantics=sem, vmem_limit_bytes=VMEM_LIMIT)


def _tile(n, pref, mult=SUBLANE):
    if n <= pref:
        return n
    t = (pref // mult) * mult
    while t > mult and n % t:
        t -= mult
    assert n % t == 0, (n, pref, mult)
    return t


def _const_spec(shape):
    nd = len(shape)
    return pl.BlockSpec(shape, lambda *_: (0,) * nd, pipeline_mode=pl.Buffered(1))


def _silu(x):
    return x * jax.nn.sigmoid(x)


def _norm_mod(x, g, sh, sc):
    y = x * lax.rsqrt(jnp.mean(x * x, axis=-1, keepdims=True) + EPS)
    return (y * g) * (1.0 + sc) + sh


def _mods_kernel(c_ref, w_ref, b_ref, o_ref):
    cond = _silu(c_ref[...])
    o_ref[...] = jnp.dot(cond, w_ref[...], preferred_element_type=F32,
                         precision=lax.Precision.HIGHEST) + b_ref[...]


def ada_mods(cvec, w, b):
    r, d = cvec.shape
    n = w.shape[1]
    tn = _tile(n, 512, LANE)
    return pl.pallas_call(
        _mods_kernel,
        grid=(n // tn,),
        in_specs=[pl.BlockSpec((r, d), lambda j: (0, 0)),
                  pl.BlockSpec((d, tn), lambda j: (0, j)),
                  pl.BlockSpec((1, tn), lambda j: (0, j))],
        out_specs=pl.BlockSpec((r, tn), lambda j: (0, j)),
        out_shape=jax.ShapeDtypeStruct((r, n), F32),
        compiler_params=_cparams("arbitrary"),
        name="ada_mods",
    )(cvec, w, b.reshape(1, n))


def _swap16(y):
    n = y.shape[-1]
    lane = lax.broadcasted_iota(jnp.int32, y.shape, y.ndim - 1)
    fwd = pltpu.roll(y, n - 16, y.ndim - 1)
    bwd = pltpu.roll(y, 16, y.ndim - 1)
    return jnp.where((lane % 32) < 16, fwd, bwd)


PROJ_CHUNK = 512


def _proj_kernel(x_ref, g_ref, sh_ref, sc_ref, w_ref, *rest, splits, rope_cols):
    n_out = len(splits)
    if rope_cols:
        cq_ref, sq_ref, ck_ref, sk_ref = rest[:4]
        rest = rest[4:]
    o_refs = rest[:n_out]
    h = _norm_mod(x_ref[0], g_ref[...], sh_ref[0], sc_ref[0]).astype(BF16)
    for o_ref, (s0, s1) in zip(o_refs, splits):
        tc = PROJ_CHUNK if (s1 - s0) % PROJ_CHUNK == 0 and s0 % PROJ_CHUNK == 0 else LANE
        for c0 in range(s0, s1, tc):
            y = jnp.dot(h, w_ref[:, c0:c0 + tc], preferred_element_type=F32)
            if rope_cols and rope_cols[0] <= c0 < rope_cols[2]:
                is_q = c0 < rope_cols[1]
                cos = (cq_ref if is_q else ck_ref)[...]
                sin = (sq_ref if is_q else sk_ref)[...]
                reps = tc // LANE
                cos = jnp.concatenate([cos] * reps, axis=1)
                sin = jnp.concatenate([sin] * reps, axis=1)
                y = y * cos + _swap16(y) * sin
            o_ref[0, :, c0 - s0:c0 - s0 + tc] = y.astype(o_ref.dtype)


def norm_mod_proj(x, g, sh, sc, w, splits, dtypes, rope=None, rope_cols=None):
    b, l, d = x.shape
    n = w.shape[1]
    tm = _tile(l, 512)
    assert all(s % LANE == 0 for sp in splits for s in sp)
    if rope_cols:
        assert all(c % PROJ_CHUNK == 0 for c in rope_cols)
    row = lambda bi, i: (bi, i, 0)
    in_specs = [pl.BlockSpec((1, tm, d), row),
                _const_spec((1, d)),
                pl.BlockSpec((1, 1, d), lambda bi, i: (bi, 0, 0)),
                pl.BlockSpec((1, 1, d), lambda bi, i: (bi, 0, 0)),
                _const_spec((d, n))]
    args = [x, g.reshape(1, d), sh, sc, w]
    if rope_cols:
        in_specs += [pl.BlockSpec((tm, LANE), lambda bi, i: (i, 0))] * 4
        args += list(rope)
    return pl.pallas_call(
        functools.partial(_proj_kernel, splits=tuple(splits), rope_cols=rope_cols),
        grid=(b, l // tm),
        in_specs=in_specs,
        out_specs=[pl.BlockSpec((1, tm, s1 - s0), row) for s0, s1 in splits],
        out_shape=[jax.ShapeDtypeStruct((b, l, s1 - s0), dt) for (s0, s1), dt in zip(splits, dtypes)],
        compiler_params=_cparams("parallel", "parallel"),
        name="norm_mod_proj",
    )(*args)


CONV_HALO = 16


def _conformer_kernel(ac_ref, gc_ref, ap_ref, gp_ref, an_ref, gn_ref, w_ref, b_ref, lg_ref, lb_ref, o_ref,
                      u_ref, sh_ref, y_ref, *, tl, ch):
    i = pl.program_id(1)
    last = pl.num_programs(1) - 1
    left = (CONV_W - 1) // 2
    glu = lambda a, g: a * jax.nn.sigmoid(g)
    u_ref[CONV_HALO:CONV_HALO + tl, :] = glu(ac_ref[0], gc_ref[0])
    prev = glu(ap_ref[0], gp_ref[0])
    nxt = glu(an_ref[0], gn_ref[0])
    u_ref[0:CONV_HALO, :] = jnp.where(i > 0, prev, 0.0)
    u_ref[CONV_HALO + tl:CONV_HALO + tl + CONV_HALO, :] = jnp.where(i < last, nxt, 0.0)
    n = sh_ref.shape[1]
    for p in range(SUBLANE):
        sh_ref[p] = u_ref[p:p + n, :]
    rows = min(tl, 128)
    for r0 in range(0, tl, rows):
        for c0 in range(0, ch, LANE):
            acc = jnp.zeros((rows, LANE), F32) + b_ref[:, c0:c0 + LANE]
            for k in range(CONV_W):
                s = CONV_HALO - left + k + r0
                p, base = s % SUBLANE, s - s % SUBLANE
                acc = acc + w_ref[k:k + 1, c0:c0 + LANE] * sh_ref[p, base:base + rows, c0:c0 + LANE]
            y_ref[r0:r0 + rows, c0:c0 + LANE] = acc
    y = y_ref[...]
    mu = jnp.mean(y, axis=-1, keepdims=True)
    yc = y - mu
    var = jnp.mean(yc * yc, axis=-1, keepdims=True)
    z = yc * lax.rsqrt(var + EPS) * lg_ref[...] + lb_ref[...]
    o_ref[0] = _silu(z).astype(o_ref.dtype)


def conformer_conv(p, conv_w, conv_b, ln_g, ln_b):
    b, l, _ = p.shape
    ch = conv_w.shape[1]
    tl = _tile(l, 256, CONV_HALO)
    nh = tl // CONV_HALO
    nblk = l // CONV_HALO
    cur = lambda col: pl.BlockSpec((1, tl, ch), lambda bi, i: (bi, i, col))
    prv = lambda col: pl.BlockSpec((1, CONV_HALO, ch), lambda bi, i: (bi, jnp.maximum(i * nh - 1, 0), col))
    nxt = lambda col: pl.BlockSpec((1, CONV_HALO, ch), lambda bi, i: (bi, jnp.minimum((i + 1) * nh, nblk - 1), col))
    return pl.pallas_call(
        functools.partial(_conformer_kernel, tl=tl, ch=ch),
        grid=(b, l // tl),
        in_specs=[cur(0), cur(1), prv(0), prv(1), nxt(0), nxt(1),
                  _const_spec((CONV_W, ch)), _const_spec((1, ch)), _const_spec((1, ch)), _const_spec((1, ch))],
        out_specs=pl.BlockSpec((1, tl, ch), lambda bi, i: (bi, i, 0)),
        out_shape=jax.ShapeDtypeStruct((b, l, ch), BF16),
        scratch_shapes=[pltpu.VMEM((tl + 2 * CONV_HALO, ch), F32),
                        pltpu.VMEM((SUBLANE, tl + 2 * CONV_HALO - SUBLANE, ch), F32),
                        pltpu.VMEM((tl, ch), F32)],
        compiler_params=_cparams("parallel", "parallel"),
        name="conformer_conv",
    )(p, p, p, p, p, p, conv_w, conv_b.reshape(1, ch), ln_g.reshape(1, ch), ln_b.reshape(1, ch))


ATTN_QBLK = 256


ATTN_ROWS = 128
ATTN_ONES = 16
ATTN_REDO = 120.0
ATTN_REF_ROW = (DIFF_DQK, 0)


def _round_bf16(x):
    return x.astype(BF16).astype(F32)


def _diff_attn_kernel(qt_ref, k_ref, vt_ref, lam_ref, g_ref, o_ref, qm_ref, m_ref, acc_ref, *, lambda_init):
    kv = pl.program_id(3)
    tq = qm_ref.shape[2]
    qblk = min(ATTN_QBLK, tq)
    chains = [(mi, slice(q0, q0 + qblk)) for q0 in range(0, tq, qblk) for mi in range(2)]

    k = k_ref[0]
    lane = lax.broadcasted_iota(jnp.int32, k.shape, 1)
    kx = [jnp.where(lane == r, jnp.ones_like(k), k) for r in ATTN_REF_ROW]
    vtx = vt_ref[0, 0]

    def scores(mi, qs, rows=None):
        keys = kx[mi] if rows is None else kx[mi][:rows]
        return jnp.dot(keys, qm_ref[mi, :, qs].astype(BF16), preferred_element_type=F32)

    def move_reference(mi, qs, m_new):
        m_ref[mi, :, qs] = m_new
        r = ATTN_REF_ROW[mi]
        qm_ref[mi, r:r + 1, qs] = -m_new

    @pl.when(kv == 0)
    def _():
        qt = qt_ref[0].astype(F32)
        row = lax.broadcasted_iota(jnp.int32, qt.shape, 0)
        qm_ref[0] = jnp.where(row < DIFF_DQK, qt, 0.0)
        qm_ref[1] = jnp.where(row < DIFF_DQK, 0.0, qt)
        acc_ref[...] = jnp.zeros(acc_ref.shape, F32)
        first = [scores(mi, qs, ATTN_ROWS) for mi, qs in chains]
        for (mi, qs), s in zip(chains, first):
            move_reference(mi, qs, _round_bf16(jnp.max(s, axis=0, keepdims=True)))

    sc = [scores(mi, qs) for mi, qs in chains]
    mxs, pvs = [], []
    for s in sc:
        es, mx = [], None
        for r0 in range(0, s.shape[0], ATTN_ROWS):
            grp = s[r0:r0 + ATTN_ROWS]
            es.append(jnp.exp2(grp).astype(BF16))
            top = jnp.max(grp, axis=0, keepdims=True)
            mx = top if mx is None else jnp.maximum(mx, top)
        mxs.append(mx)
        pvs.append(jnp.dot(vtx, jnp.concatenate(es, axis=0), preferred_element_type=F32))
    redo = jnp.max(functools.reduce(jnp.maximum, mxs)) > ATTN_REDO

    @pl.when(jnp.logical_not(redo))
    def _():
        for (mi, qs), mx, pv in zip(chains, mxs, pvs):
            m_old = m_ref[mi, :, qs]
            m_new = _round_bf16(m_old + jnp.maximum(mx, 0.0))
            acc_ref[mi, :, qs] = (acc_ref[mi, :, qs] + pv) * jnp.exp2(m_old - m_new)
            move_reference(mi, qs, m_new)

    @pl.when(redo)
    def _():
        for (mi, qs), s in zip(chains, sc):
            m_old = m_ref[mi, :, qs]
            m_new = _round_bf16(m_old + jnp.maximum(jnp.max(s, axis=0, keepdims=True), 0.0))
            delta = m_new - m_old
            e = jnp.exp2(s - delta)
            acc_ref[mi, :, qs] = jnp.exp2(-delta) * acc_ref[mi, :, qs] + jnp.dot(vtx, e.astype(BF16),
                                                                                 preferred_element_type=F32)
            move_reference(mi, qs, m_new)

    @pl.when(kv == pl.num_programs(3) - 1)
    def _():
        lam = (jnp.exp(jnp.sum(lam_ref[0:1, :] * lam_ref[1:2, :], axis=-1, keepdims=True))
               - jnp.exp(jnp.sum(lam_ref[2:3, :] * lam_ref[3:4, :], axis=-1, keepdims=True)) + lambda_init)
        dv = DIFF_DV
        ot = (acc_ref[0, :dv, :] / acc_ref[0, dv:dv + 1, :]
              - lam * (acc_ref[1, :dv, :] / acc_ref[1, dv:dv + 1, :]))
        ot = ot * lax.rsqrt(jnp.mean(ot * ot, axis=0, keepdims=True) + EPS)
        o = ot.T * g_ref[...]
        o_ref[0] = (o * (1.0 - lambda_init)).astype(o_ref.dtype)


def diff_attention(qt, k, vt, lam_vecs, subln_g, lambda_init, k_blk):
    b, _, lq = qt.shape
    lk = k.shape[1]
    tq = _tile(lq, 2048, LANE)
    tk = _tile(lk, 1536, LANE)
    rows = vt.shape[2]
    return pl.pallas_call(
        functools.partial(_diff_attn_kernel, lambda_init=lambda_init),
        grid=(b, DIFF_HEADS, lq // tq, lk // tk),
        in_specs=[pl.BlockSpec((1, LANE, tq), lambda bi, h, i, j: (bi, h, i)),
                  pl.BlockSpec((1, tk, LANE), lambda bi, h, i, j: (bi, j, k_blk + h)),
                  pl.BlockSpec((1, 1, rows, tk), lambda bi, h, i, j: (bi, h, 0, j)),
                  _const_spec((4, DIFF_DQK)), _const_spec((1, DIFF_DV))],
        out_specs=pl.BlockSpec((1, tq, LANE), lambda bi, h, i, j: (bi, i, h)),
        out_shape=jax.ShapeDtypeStruct((b, lq, DIFF_HEADS * DIFF_DV), BF16),
        scratch_shapes=[pltpu.VMEM((2, LANE, tq), F32), pltpu.VMEM((2, 1, tq), F32),
                        pltpu.VMEM((2, rows, tq), F32)],
        compiler_params=_cparams("parallel", "parallel", "parallel", "arbitrary"),
        name="diff_attention",
    )(qt, k, vt, lam_vecs, subln_g.reshape(1, DIFF_DV))


def _delta_gate(of_ref, ob_ref, gate_ref, g_ref):
    o = of_ref[0] + ob_ref[0]
    parts = []
    for c0 in range(0, o.shape[1], DN_DK):
        t = o[:, c0:c0 + DN_DK]
        t = t * lax.rsqrt(jnp.mean(t * t, axis=-1, keepdims=True) + EPS) * g_ref[...]
        parts.append((t * _silu(gate_ref[0, :, c0:c0 + DN_DK])).astype(BF16))
    return jnp.concatenate(parts, axis=1)


def _out_mlp_kernel(x_ref, a_ref, *refs, a_transposed, hidden_chunk, final_norm, delta_b):
    if delta_b:
        mix_b = _delta_gate(*refs[:4])
        refs = refs[4:]
    else:
        mix_b = refs[0][0]
        refs = refs[1:]
    wo_ref, gt1_ref, g2_ref, sh2_ref, sc2_ref, gt2_ref, w1_ref, w2_ref, fg_ref, o_ref = refs
    half = wo_ref.shape[0] // 2
    if a_transposed:
        ya = lax.dot_general(a_ref[0].astype(BF16), wo_ref[0:half, :], (((0,), (0,)), ((), ())),
                             preferred_element_type=F32)
    else:
        ya = jnp.dot(a_ref[0], wo_ref[0:half, :], preferred_element_type=F32)
    y = ya + jnp.dot(mix_b, wo_ref[half:, :], preferred_element_type=F32)
    x1 = x_ref[0] + gt1_ref[0] * y
    h = _norm_mod(x1, g2_ref[...], sh2_ref[0], sc2_ref[0]).astype(BF16)
    acc = jnp.zeros_like(x1)
    for c0 in range(0, w1_ref.shape[1], hidden_chunk):
        t = jnp.maximum(jnp.dot(h, w1_ref[:, c0:c0 + hidden_chunk], preferred_element_type=F32), 0.0)
        acc = acc + jnp.dot((t * t).astype(BF16), w2_ref[c0:c0 + hidden_chunk, :], preferred_element_type=F32)
    x2 = x1 + gt2_ref[0] * acc
    if final_norm:
        x2 = x2 * lax.rsqrt(jnp.mean(x2 * x2, axis=-1, keepdims=True) + EPS) * fg_ref[...]
    o_ref[0] = x2


def out_proj_mlp(x, mix_a, mix_b, w_out, gt1, g2, sh2, sc2, gt2, w1, w2, final_g=None, a_transposed=False):
    b, l, d = x.shape
    delta_b = isinstance(mix_b, tuple)
    c = (mix_b[0] if delta_b else mix_b).shape[2]
    hid = w1.shape[1]
    tm = _tile(l, 512, LANE)
    row = lambda bi, i: (bi, i, 0)
    vec = lambda bi, i: (bi, 0, 0)
    a_spec = (pl.BlockSpec((1, c, tm), lambda bi, i: (bi, 0, i)) if a_transposed
              else pl.BlockSpec((1, tm, c), row))
    if delta_b:
        b_args = list(mix_b[:3]) + [mix_b[3].reshape(1, DN_DK)]
        b_specs = [pl.BlockSpec((1, tm, c), row)] * 3 + [_const_spec((1, DN_DK))]
    else:
        b_args, b_specs = [mix_b], [pl.BlockSpec((1, tm, c), row)]
    fg = jnp.ones((1, d), F32) if final_g is None else final_g.reshape(1, d)
    return pl.pallas_call(
        functools.partial(_out_mlp_kernel, a_transposed=a_transposed, hidden_chunk=min(hid, 1024),
                          final_norm=final_g is not None, delta_b=delta_b),
        grid=(b, l // tm),
        in_specs=[pl.BlockSpec((1, tm, d), row), a_spec] + b_specs + [
                  _const_spec((2 * c, d)),
                  pl.BlockSpec((1, 1, d), vec), _const_spec((1, d)),
                  pl.BlockSpec((1, 1, d), vec), pl.BlockSpec((1, 1, d), vec), pl.BlockSpec((1, 1, d), vec),
                  _const_spec((d, hid)), _const_spec((hid, d)), _const_spec((1, d))],
        out_specs=pl.BlockSpec((1, tm, d), row),
        out_shape=jax.ShapeDtypeStruct((b, l, d), F32),
        compiler_params=_cparams("parallel", "parallel"),
        name="out_proj_mlp",
    )(x, mix_a, *b_args, w_out, gt1, g2.reshape(1, d), sh2, sc2, gt2, w1, w2, fg)


def _rope_tables(length, scale):
    f32 = np.float32
    rows = length // GRID_W
    row = np.repeat(np.arange(rows, dtype=f32), GRID_W)
    col = np.tile(np.arange(GRID_W, dtype=f32), rows)
    n_freq = DIFF_DQK // 4
    inv = (f32(ROPE_BASE) ** (-np.arange(n_freq, dtype=f32) / f32(n_freq))).astype(f32)
    ang_r = row[:, None] * inv
    ang_c = col[:, None] * inv
    cos = np.concatenate([np.cos(ang_r)] * 2 + [np.cos(ang_c)] * 2, axis=-1)
    sin = np.concatenate([-np.sin(ang_r), np.sin(ang_r), -np.sin(ang_c), np.sin(ang_c)], axis=-1)
    cos = np.concatenate([cos, cos], axis=-1) * f32(scale)
    sin = np.concatenate([sin, sin], axis=-1) * f32(scale)
    return cos.astype(f32), sin.astype(f32)


def _flat_tables(length, scale):
    return np.full((length, LANE), scale, np.float32), np.zeros((length, LANE), np.float32)


def _proj_t_kernel(x_ref, g_ref, sh_ref, sc_ref, wt_ref, o_ref, *, rc):
    h = _norm_mod(x_ref[0], g_ref[...], sh_ref[0], sc_ref[0]).astype(BF16)
    for r0 in range(0, wt_ref.shape[0], rc):
        o_ref[0, r0:r0 + rc, :] = lax.dot_general(wt_ref[r0:r0 + rc, :], h, (((1,), (1,)), ((), ())),
                                                  preferred_element_type=F32)


def norm_mod_proj_t(x, g, sh, sc, wt):
    b, l, d = x.shape
    c = wt.shape[0]
    tl = _tile(l, 512, LANE)
    return pl.pallas_call(
        functools.partial(_proj_t_kernel, rc=_tile(c, 512)),
        grid=(b, l // tl),
        in_specs=[pl.BlockSpec((1, tl, d), lambda bi, i: (bi, i, 0)),
                  _const_spec((1, d)),
                  pl.BlockSpec((1, 1, d), lambda bi, i: (bi, 0, 0)),
                  pl.BlockSpec((1, 1, d), lambda bi, i: (bi, 0, 0)),
                  _const_spec((c, d))],
        out_specs=pl.BlockSpec((1, c, tl), lambda bi, i: (bi, 0, i)),
        out_shape=jax.ShapeDtypeStruct((b, c, l), F32),
        compiler_params=_cparams("parallel", "parallel"),
        name="norm_mod_proj_t",
    )(x, g.reshape(1, d), sh, sc, wt)


def _dot_hi(a, b):
    return jnp.dot(a, b, preferred_element_type=F32, precision=lax.Precision.HIGHEST)


def _hy_hidden_kernel(z_ref, w1_ref, b1_ref, f1_ref, w2_ref, b2_ref, f2_ref, o_ref):
    hid = jnp.sin(f1_ref[...] * (_dot_hi(w1_ref[...], z_ref[...]) + b1_ref[...]))
    o_ref[...] = jnp.sin(f2_ref[...] * (_dot_hi(w2_ref[...], hid) + b2_ref[...]))


def hyena_hidden(zt, w1t, b1, f1, w2t, b2, f2):
    e, n = zt.shape
    f = w1t.shape[0]
    tn = _tile(n, 2048, LANE)
    col = lambda v: v.reshape(f, 1)
    return pl.pallas_call(
        _hy_hidden_kernel,
        grid=(n // tn,),
        in_specs=[pl.BlockSpec((e, tn), lambda j: (0, j)),
                  _const_spec((f, e)), _const_spec((f, 1)), _const_spec((f, 1)),
                  _const_spec((f, f)), _const_spec((f, 1)), _const_spec((f, 1))],
        out_specs=pl.BlockSpec((f, tn), lambda j: (0, j)),
        out_shape=jax.ShapeDtypeStruct((f, n), F32),
        compiler_params=_cparams("parallel"),
        name="hyena_hidden",
    )(zt, w1t, col(b1), col(f1), w2t, col(b2), col(f2))


def _hy_filter_kernel(hid_ref, t_ref, wf_ref, wb_ref, dl_ref, o_ref, *, length, tn):
    n2 = hid_ref.shape[1]
    delta = dl_ref[...]

    def piece(c0):
        hid = hid_ref[:, c0:c0 + tn]
        dec = jnp.exp(-t_ref[:, c0:c0 + tn] * delta)
        return _dot_hi(wf_ref[...], hid) * dec, _dot_hi(wb_ref[...], hid) * dec

    norm = jnp.zeros((wf_ref.shape[0], 1), F32)
    for c0 in range(length, n2, tn):
        ff, fb = piece(c0)
        norm = norm + jnp.sum(jnp.abs(ff) + jnp.abs(fb), axis=-1, keepdims=True)
    inv = 1.0 / norm
    for c0 in range(0, n2, tn):
        ff, fb = piece(c0)
        if c0 >= length:
            o_ref[0, :, c0:c0 + tn] = ff * inv
        else:
            col = lax.broadcasted_iota(jnp.int32, fb.shape, 1) + c0
            o_ref[0, :, c0:c0 + tn] = jnp.where(col == 0, 0.0, fb * inv)


def hyena_filters(hid, t_row, w3t, deltas, length):
    f, n2 = hid.shape
    ch = deltas.shape[0]
    cb = LANE
    nblk = ch // cb
    tn = _tile(length, 2048, LANE)
    return pl.pallas_call(
        functools.partial(_hy_filter_kernel, length=length, tn=tn),
        grid=(HY_ORDER, nblk),
        in_specs=[_const_spec((f, n2)), _const_spec((1, n2)),
                  pl.BlockSpec((cb, f), lambda o, c: (o * 2 * nblk + c, 0)),
                  pl.BlockSpec((cb, f), lambda o, c: (o * 2 * nblk + nblk + c, 0)),
                  pl.BlockSpec((cb, 1), lambda o, c: (c, 0))],
        out_specs=pl.BlockSpec((1, cb, n2), lambda o, c: (o, c, 0)),
        out_shape=jax.ShapeDtypeStruct((HY_ORDER, ch, n2), F32),
        compiler_params=_cparams("parallel", "parallel"),
        name="hyena_filters",
    )(hid, t_row, w3t, w3t, deltas.reshape(ch, 1))


HY_T = 256
HY_CB = SUBLANE

def _hyena_kernel(v_ref, x1_ref, x2_ref, g_ref, sw_ref, sb_ref, sk_ref_s, o_ref, sk_ref, u_ref, acc_ref,
                  *, length, nch):
    nb = v_ref.shape[0]
    nj = length // HY_T
    cblk = pl.program_id(0)
    lane = lax.broadcasted_iota(jnp.int32, (nb, length), 1)

    def short(ref, cc, part):
        x = ref[:, cc, :]
        ch = part * nch + cblk * HY_CB + cc
        prev = jnp.where(lane == 0, 0.0, pltpu.roll(x, 1, 1))
        nxt = jnp.where(lane == length - 1, 0.0, pltpu.roll(x, length - 1, 1))
        return sw_ref[0, ch] * prev + sw_ref[1, ch] * x + sw_ref[2, ch] * nxt + sb_ref[ch]

    def long_conv(u, order, cc):
        g = jnp.broadcast_to(g_ref[order, pl.ds(cc, 1), :], (2 * SUBLANE, 2 * length))
        half = HY_T // 2
        for r0 in range(0, half, 2 * SUBLANE):
            pair = pltpu.roll(g, r0, 1, stride=1, stride_axis=0).astype(BF16)
            sk_ref[r0:r0 + 2 * SUBLANE, :] = pair
            sk_ref[half + r0:half + r0 + 2 * SUBLANE, half:] = pair[:, :2 * length - half]
        rows = nj * nb
        x2 = jnp.concatenate([u[:, j * HY_T:(j + 1) * HY_T] for j in range(nj)]
                             + [jnp.zeros((2 * nb, HY_T), F32)], axis=0)
        u_ref[0] = x2.astype(BF16)
        u_ref[1] = jnp.concatenate([x2[nb:, :], x2[rows:rows + nb, :]], axis=0).astype(BF16)
        acc_ref[...] = jnp.zeros(acc_ref.shape, F32)
        for d in range(-(nj - 1), nj):
            m = (nj - abs(d)) * nb
            lo_in, lo_out = max(0, -d) * nb, max(0, d) * nb
            which, off = (0, lo_in) if lo_in % (2 * nb) == 0 else (1, lo_in - nb)
            mm = -(-m // (2 * nb)) * (2 * nb)
            tile = sk_ref[:, length + d * HY_T:length + (d + 1) * HY_T]
            part = jnp.dot(u_ref[which, off:off + mm, :], tile, preferred_element_type=F32)
            acc_ref[lo_out:lo_out + m, :] += part[:m]
        return jnp.concatenate([acc_ref[i * nb:(i + 1) * nb, :] for i in range(nj)], axis=1)

    def channel(cc, carry):
        v = short(v_ref, cc, 0)
        x1 = short(x1_ref, cc, 1)
        x2 = short(x2_ref, cc, 2)
        ch = cblk * HY_CB + cc
        z = x1 * (long_conv(v, 0, cc) + sk_ref_s[0, ch] * v)
        z = x2 * (long_conv(z, 1, cc) + sk_ref_s[1, ch] * z)
        o_ref[:, cc, :] = z
        return carry

    lax.fori_loop(0, HY_CB, channel, 0)


def hyena_mix(pt, filt, short_w, short_b, skip):
    b, c3, l = pt.shape
    nch = c3 // 3
    assert l % HY_T == 0 and nch % HY_CB == 0 and b == SUBLANE
    nblk = nch // HY_CB
    smem = pl.BlockSpec(memory_space=pltpu.SMEM)
    part = lambda k: pl.BlockSpec((b, HY_CB, l), lambda c: (0, k * nblk + c, 0))
    return pl.pallas_call(
        functools.partial(_hyena_kernel, length=l, nch=nch),
        grid=(nblk,),
        in_specs=[part(0), part(1), part(2),
                  pl.BlockSpec((HY_ORDER, HY_CB, 2 * l), lambda c: (0, c, 0)),
                  smem, smem, smem],
        out_specs=pl.BlockSpec((b, HY_CB, l), lambda c: (0, c, 0)),
        out_shape=jax.ShapeDtypeStruct((b, nch, l), F32),
        scratch_shapes=[pltpu.VMEM((HY_T, 2 * l), BF16),
                        pltpu.VMEM((2, l // HY_T * b + 2 * b, HY_T), BF16),
                        pltpu.VMEM((l // HY_T * b, HY_T), F32)],
        compiler_params=_cparams("parallel"),
        name="hyena_mix",
    )(pt, pt, pt, filt, short_w, short_b, skip)


DN_HALO = SUBLANE
DN_BETA_F, DN_BETA_B, DN_G_F, DN_G_B = 0, DN_HEADS, 2 * DN_HEADS, 3 * DN_HEADS


def _dn_feat_kernel(c_ref, p_ref, n_ref, s_ref, w_ref, al_ref, dt_ref, q_ref, k_ref, v_ref, f_ref, u_ref, *, tl):
    i = pl.program_id(1)
    last = pl.num_programs(1) - 1
    left = (DN_CONV_W - 1) // 2
    nc = c_ref.shape[2]
    u_ref[DN_HALO:DN_HALO + tl, :] = c_ref[0]
    u_ref[0:DN_HALO, :] = jnp.where(i > 0, p_ref[0], 0.0)
    u_ref[DN_HALO + tl:DN_HALO + tl + DN_HALO, :] = jnp.where(i < last, n_ref[0], 0.0)
    rows = min(tl, 128)
    outs = (q_ref, k_ref, v_ref)
    per = nc // len(outs)
    for r0 in range(0, tl, rows):
        for c0 in range(0, nc, LANE):
            acc = jnp.zeros((rows, LANE), F32)
            for k in range(DN_CONV_W):
                s = DN_HALO - left + k + r0
                acc = acc + w_ref[k:k + 1, c0:c0 + LANE] * u_ref[s:s + rows, c0:c0 + LANE]
            y = _silu(acc)
            which, off = divmod(c0, per)
            if which < 2:
                y = y * lax.rsqrt(jnp.sum(y * y, axis=-1, keepdims=True) + 1e-6)
            outs[which][0, r0:r0 + rows, off:off + LANE] = y
    s = s_ref[0]
    lane = lax.broadcasted_iota(jnp.int32, s.shape, 1)
    gate = -jnp.exp(al_ref[...]) * jax.nn.softplus(s + dt_ref[...])
    f_ref[0] = jnp.where(lane < DN_G_F, jax.nn.sigmoid(s), gate)


def dn_features(qkv, small, conv_w, alog_row, dtb_row):
    b, l, nc = qkv.shape
    tl = _tile(l, 256, DN_HALO)
    nh = tl // DN_HALO
    nblk = l // DN_HALO
    w = nc // 3
    row = lambda bi, i: (bi, i, 0)
    return pl.pallas_call(
        functools.partial(_dn_feat_kernel, tl=tl),
        grid=(b, l // tl),
        in_specs=[pl.BlockSpec((1, tl, nc), row),
                  pl.BlockSpec((1, DN_HALO, nc), lambda bi, i: (bi, jnp.maximum(i * nh - 1, 0), 0)),
                  pl.BlockSpec((1, DN_HALO, nc), lambda bi, i: (bi, jnp.minimum((i + 1) * nh, nblk - 1), 0)),
                  pl.BlockSpec((1, tl, LANE), row),
                  _const_spec((DN_CONV_W, nc)), _const_spec((1, LANE)), _const_spec((1, LANE))],
        out_specs=[pl.BlockSpec((1, tl, w), row)] * 3 + [pl.BlockSpec((1, tl, LANE), row)],
        out_shape=[jax.ShapeDtypeStruct((b, l, w), F32)] * 3 + [jax.ShapeDtypeStruct((b, l, LANE), F32)],
        scratch_shapes=[pltpu.VMEM((tl + 2 * DN_HALO, nc), F32)],
        compiler_params=_cparams("parallel", "parallel"),
        name="dn_features",
    )(qkv, qkv, qkv, small, conv_w, alog_row, dtb_row)


def _bdot(a, b):
    return jnp.dot(a.astype(BF16), b.astype(BF16), preferred_element_type=F32)


def _bdot_nt(a, b):
    return lax.dot_general(a.astype(BF16), b.astype(BF16), (((1,), (1,)), ((), ())), preferred_element_type=F32)


def _bdot_tn(a, b):
    return lax.dot_general(a.astype(BF16), b.astype(BF16), (((0,), (0,)), ((), ())), preferred_element_type=F32)


def _cumsum_rows(x, reverse):
    n = x.shape[0]
    row = lax.broadcasted_iota(jnp.int32, x.shape, 0)
    s = 1
    while s < n:
        if reverse:
            x = x + jnp.where(row < n - s, pltpu.roll(x, n - s, 0), 0.0)
        else:
            x = x + jnp.where(row >= s, pltpu.roll(x, s, 0), 0.0)
        s *= 2
    return x


def _dn_chunks(chains):
    c = chains[0]["k"].shape[0]
    ri = lax.broadcasted_iota(jnp.int32, (c, c), 0)
    ci = lax.broadcasted_iota(jnp.int32, (c, c), 1)
    work = []
    for ch in chains:
        rev = ch["reverse"]
        incl = (ri <= ci) if rev else (ri >= ci)
        strict = (ri < ci) if rev else (ri > ci)
        decay = jnp.where(incl, jnp.exp(jnp.where(incl, ch["gcol"] - ch["grow"], 0.0)), 0.0)
        kb = ch["k"] * ch["beta"]
        qs = ch["q"] * (ch["k"].shape[1] ** -0.5)
        work.append(dict(ch, strict=strict, decay=decay, kb=kb, qs=qs))
    grams = [_bdot_nt(jnp.concatenate([w["kb"], w["qs"]], axis=0), w["k"]) for w in work]
    lowers = [jnp.where(w["strict"], g[:c] * w["decay"], 0.0) for w, g in zip(work, grams)]
    same2 = (ri // 2) == (ci // 2)
    ymats = [-jnp.where(same2, lo, 0.0) for lo in lowers]
    m = 2
    while m < c:
        pair = ((ri // (2 * m)) == (ci // (2 * m))) & ((ri // m) != (ci // m))
        offs = [jnp.where(pair, lo, 0.0) for lo in lowers]
        zs = [off + _bdot(ym, off) for ym, off in zip(ymats, offs)]
        ymats = [ym - z - _bdot(z, ym) for ym, z in zip(ymats, zs)]
        m *= 2
    outs = []
    uws, e_gs = [], []
    for w, ym in zip(work, ymats):
        e_g = jnp.exp(w["gcol"])
        rhs = jnp.concatenate([w["v"] * w["beta"], w["kb"] * e_g], axis=1)
        uws.append(rhs + _bdot(ym, rhs))
        e_gs.append(e_g)
    dv = work[0]["v"].shape[1]
    wss = [_bdot(jnp.concatenate([uw[:, dv:], w["qs"] * e_g], axis=0), w["state"])
           for w, uw, e_g in zip(work, uws, e_gs)]
    v_news = [uw[:, :dv] - ws[:c] for uw, ws in zip(uws, wss)]
    for w, g, ws, v_new in zip(work, grams, wss, v_news):
        gcol = w["gcol"]
        g_last = gcol[0:1, :] if w["reverse"] else gcol[c - 1:c, :]
        k_dec = w["k"] * jnp.exp(g_last - gcol)
        o = ws[c:] + _bdot(g[c:] * w["decay"], v_new)
        new_state = w["state"] * jnp.exp(g_last) + _bdot_tn(k_dec, v_new)
        outs.append((o, new_state))
    return outs


def _dn_scan_kernel(qf_ref, kf_ref, vf_ref, ff_ref, qb_ref, kb_ref, vb_ref, fb_ref, s0f_ref, s0b_ref,
                    of_ref, ob_ref, sf_ref, sb_ref, st_ref):
    n = pl.program_id(1)

    @pl.when(n == 0)
    def _():
        st_ref[0] = s0f_ref[...]
        st_ref[1] = s0b_ref[...]

    dirs = ((qf_ref, kf_ref, vf_ref, ff_ref, of_ref, False), (qb_ref, kb_ref, vb_ref, fb_ref, ob_ref, True))
    chains, dests = [], []
    for bi in range(qf_ref.shape[0]):
        for di, (q_ref, k_ref, v_ref, f_ref, o_ref, reverse) in enumerate(dirs):
            feats = f_ref[bi]
            gc = _cumsum_rows(feats, reverse)
            gct = gc.T
            for h in range(DN_HEADS):
                cols = slice(h * DN_DK, (h + 1) * DN_DK)
                bl = (DN_BETA_B if reverse else DN_BETA_F) + h
                gl = (DN_G_B if reverse else DN_G_F) + h
                chains.append(dict(q=q_ref[bi, :, cols], k=k_ref[bi, :, cols], v=v_ref[bi, :, cols],
                                   beta=feats[:, bl:bl + 1], gcol=gc[:, gl:gl + 1], grow=gct[gl:gl + 1, :],
                                   state=st_ref[di, bi, h], reverse=reverse))
                dests.append((o_ref, cols, di, bi, h))
    for (o_ref, cols, di, bi, h), (o, s_new) in zip(dests, _dn_chunks(chains)):
        o_ref[bi, :, cols] = o
        st_ref[di, bi, h] = s_new

    @pl.when(n == pl.num_programs(1) - 1)
    def _():
        sf_ref[...] = st_ref[0]
        sb_ref[...] = st_ref[1]


def dn_scan(q, k, v, feats, s0f, s0b):
    b, l, w = q.shape
    nchunk = l // DN_CHUNK
    nb = DN_BATCH if b % DN_BATCH == 0 else 1
    fwd = lambda bi, n: (bi, n, 0)
    bwd = lambda bi, n: (bi, nchunk - 1 - n, 0)
    st = lambda bi, n: (bi, 0, 0, 0)
    blk = lambda im, width: pl.BlockSpec((nb, DN_CHUNK, width), im)
    st_spec = pl.BlockSpec((nb, DN_HEADS, DN_DK, DN_DK), st)
    st_shape = jax.ShapeDtypeStruct((b, DN_HEADS, DN_DK, DN_DK), F32)
    return pl.pallas_call(
        _dn_scan_kernel,
        grid=(b // nb, nchunk),
        in_specs=[blk(fwd, w), blk(fwd, w), blk(fwd, w), blk(fwd, LANE),
                  blk(bwd, w), blk(bwd, w), blk(bwd, w), blk(bwd, LANE), st_spec, st_spec],
        out_specs=[blk(fwd, w), blk(bwd, w), st_spec, st_spec],
        out_shape=[jax.ShapeDtypeStruct((b, l, w), F32)] * 2 + [st_shape, st_shape],
        scratch_shapes=[pltpu.VMEM((2, nb, DN_HEADS, DN_DK, DN_DK), F32)],
        compiler_params=_cparams("parallel", "arbitrary"),
        name="dn_scan",
    )(q, k, v, feats, q, k, v, feats, s0f, s0b)


def _split_mods(mods, b, d):
    lat = [mods[:b, k * d:(k + 1) * d].reshape(b, 1, d) for k in range(N_MOD)]
    ctx = [jnp.broadcast_to(mods[b, k * d:(k + 1) * d].reshape(1, 1, d), (b, 1, d)) for k in range(N_MOD)]
    return lat, ctx


def _even_layer(x, ctx, lat_m, ctx_m, n1g, n2g, w_in, conv_w, conv_b, ln_g, ln_b, lam_vecs, subln_g, w_out,
                w1, w2, lambda_init):
    b, l, d = x.shape
    lc = ctx.shape[1]
    ch = conv_w.shape[1]
    qk = DIFF_HEADS * 2 * DIFF_DQK
    q0, k0, v0, n_in = 2 * ch, 2 * ch + qk, 2 * ch + 2 * qk, 2 * ch + 2 * qk + DIFF_HEADS * DIFF_DV
    scale = DIFF_DQK ** -0.5 * math.log2(math.e)
    splits = [(0, q0), (q0, n_in)]
    w_in = w_in.astype(BF16)
    rope_lat = _rope_tables(l, scale) + _rope_tables(l, 1.0)
    rope_ctx = _flat_tables(lc, scale) + _flat_tables(lc, 1.0)
    pc_lat, qkv_lat = norm_mod_proj(x, n1g, lat_m[0], lat_m[1], w_in, splits, (F32, BF16), rope_lat, (q0, k0, v0))
    pc_ctx, qkv_ctx = norm_mod_proj(ctx, n1g, ctx_m[0], ctx_m[1], w_in, splits, (F32, BF16), rope_ctx, (q0, k0, v0))
    conv_lat = conformer_conv(pc_lat, conv_w, conv_b, ln_g, ln_b)
    conv_ctx = conformer_conv(pc_ctx, conv_w, conv_b, ln_g, ln_b)
    k_all = jnp.concatenate([qkv_lat[:, :, qk:2 * qk], qkv_ctx[:, :, qk:2 * qk]], axis=1)
    def heads_t(v):
        vt = jnp.swapaxes(v, 1, 2).reshape(b, DIFF_HEADS, DIFF_DV, v.shape[1])
        return jnp.concatenate([vt, jnp.ones((b, DIFF_HEADS, ATTN_ONES, v.shape[1]), vt.dtype)], axis=2)

    vt_ctx = heads_t(qkv_ctx[:, :, 2 * qk:])
    vt_all = jnp.concatenate([heads_t(qkv_lat[:, :, 2 * qk:]), vt_ctx], axis=3)
    qt_lat = jnp.swapaxes(qkv_lat[:, :, :qk], 1, 2)
    qt_ctx = jnp.swapaxes(qkv_ctx[:, :, :qk], 1, 2)
    o_lat = diff_attention(qt_lat, k_all, vt_all, lam_vecs, subln_g, lambda_init, 0)
    o_ctx = diff_attention(qt_ctx, qkv_ctx, vt_ctx, lam_vecs, subln_g, lambda_init, qk // LANE)
    w_out, w1, w2 = w_out.astype(BF16), w1.astype(BF16), w2.astype(BF16)
    x = out_proj_mlp(x, conv_lat, o_lat, w_out, lat_m[2], n2g, lat_m[3], lat_m[4], lat_m[5], w1, w2)
    ctx = out_proj_mlp(ctx, conv_ctx, o_ctx, w_out, ctx_m[2], n2g, ctx_m[3], ctx_m[4], ctx_m[5], w1, w2)
    return x, ctx


def _hyena_position_tables(length):
    f32 = np.float32
    t = np.linspace(0.0, 1.0, length, dtype=f32)[:, None]
    bands = (HY_EMB - 1) // 2
    omega = (f32(2.0 * math.pi) * np.arange(length, dtype=f32)[:, None] / f32(length)).astype(f32)
    ang = omega * np.linspace(1e-4, bands - 1, bands, dtype=f32)
    z = np.concatenate([t, np.cos(ang), -np.sin(ang)], axis=-1).astype(f32)
    pos = np.minimum(np.abs(np.arange(2 * length) - length), length - 1)
    pad = -HY_EMB % SUBLANE
    zt = np.pad(z[pos].T, ((0, pad), (0, 0)))
    return np.ascontiguousarray(zt), np.ascontiguousarray(t[pos].T)


def _odd_layer_last(x, ctx, lat_m, ctx_m, n1g, n2g, w_in, hy_short_w, hy_short_b, hy_w1, hy_b1, hy_f1, hy_w2, hy_b2,
                    hy_f2, hy_w3, hy_skip, dn_conv_w, alog_f, alog_b, dtb_f, dtb_b, dn_norm_g, w_out, w1, w2,
                    final_g):
    b, l, d = x.shape
    hy_in = hy_short_w.shape[1]
    hy_ch = hy_in // (HY_ORDER + 1)
    dn_w = DN_HEADS * DN_DK
    n_rest = w_in.shape[1] - hy_in
    n_pad = -n_rest % LANE
    w_rest = jnp.pad(w_in[:, hy_in:], ((0, 0), (0, n_pad))).astype(BF16)
    splits = [(0, dn_w), (dn_w, 4 * dn_w), (4 * dn_w, n_rest + n_pad)]
    f32x3 = (F32, F32, F32)
    gate_lat, qkv_lat, small_lat = norm_mod_proj(x, n1g, lat_m[0], lat_m[1], w_rest, splits, f32x3)
    _, qkv_ctx, small_ctx = norm_mod_proj(ctx, n1g, ctx_m[0], ctx_m[1], w_rest, splits, f32x3)

    pt = norm_mod_proj_t(x, n1g, lat_m[0], lat_m[1], w_in[:, :hy_in].T.astype(BF16))
    zt, t_row = _hyena_position_tables(l)
    w1t = jnp.pad(hy_w1.T, ((0, 0), (0, zt.shape[0] - hy_w1.shape[0])))
    hid = hyena_hidden(zt, w1t, hy_b1, hy_f1, hy_w2.T, hy_b2, hy_f2)
    deltas = jnp.abs(jnp.linspace(HY_MIN_DECAY, HY_MAX_DECAY, hy_ch, dtype=F32))
    filt = hyena_filters(hid, t_row, hy_w3.T, deltas, l)
    hy = hyena_mix(pt, filt, hy_short_w, hy_short_b, hy_skip)

    lanes = jnp.zeros((LANE,), F32)
    alog_row = lanes.at[DN_G_F:DN_G_F + DN_HEADS].set(alog_f).at[DN_G_B:DN_G_B + DN_HEADS].set(alog_b).reshape(1, LANE)
    dtb_row = lanes.at[DN_G_F:DN_G_F + DN_HEADS].set(dtb_f).at[DN_G_B:DN_G_B + DN_HEADS].set(dtb_b).reshape(1, LANE)
    q_c, k_c, v_c, f_c = dn_features(qkv_ctx, small_ctx, dn_conv_w, alog_row, dtb_row)
    q_l, k_l, v_l, f_l = dn_features(qkv_lat, small_lat, dn_conv_w, alog_row, dtb_row)
    s0 = jnp.zeros((b, DN_HEADS, DN_DK, DN_DK), F32)
    _, _, s_cf, s_cb = dn_scan(q_c, k_c, v_c, f_c, s0, s0)
    o_f, o_b, _, _ = dn_scan(q_l, k_l, v_l, f_l, s_cf, s_cb)
    dn = (o_f, o_b, gate_lat, dn_norm_g)

    return out_proj_mlp(x, hy, dn, w_out.astype(BF16), lat_m[2], n2g, lat_m[3], lat_m[4], lat_m[5],
                        w1.astype(BF16), w2.astype(BF16), final_g=final_g, a_transposed=True)


def kernel(x, c, ctx, c_ctx, ada_w, ada_b, norm1_g, norm2_g, mlp_w1, mlp_w2, ev_w_in, ev_conv_w, ev_conv_b, ev_ln_g,
           ev_ln_b, ev_lq1, ev_lk1, ev_lq2, ev_lk2, ev_subln_g, ev_w_out, od_w_in, od_hy_short_w, od_hy_short_b,
           od_hy_w1, od_hy_b1, od_hy_freq1, od_hy_w2, od_hy_b2, od_hy_freq2, od_hy_w3, od_hy_skip, od_dn_conv_w,
           od_dn_alog_f, od_dn_alog_b, od_dn_dtb_f, od_dn_dtb_b, od_dn_norm_g, od_w_out, final_g):
    b, _, d = x.shape
    assert ada_w.shape[0] == 2, "layer 0 = conformer/diff-attention, layer 1 (last) = Hyena/DeltaNet"
    rows = -(-(b + 1) // SUBLANE) * SUBLANE
    cvec = jnp.zeros((rows, d), F32).at[:b].set(c).at[b].set(c_ctx)

    lat_m, ctx_m = _split_mods(ada_mods(cvec, ada_w[0], ada_b[0]), b, d)
    lam_vecs = jnp.stack([ev_lq1[0], ev_lk1[0], ev_lq2[0], ev_lk2[0]])
    x, ctx = _even_layer(x, ctx, lat_m, ctx_m, norm1_g[0], norm2_g[0], ev_w_in[0], ev_conv_w[0], ev_conv_b[0],
                         ev_ln_g[0], ev_ln_b[0], lam_vecs, ev_subln_g[0], ev_w_out[0], mlp_w1[0], mlp_w2[0],
                         0.8 - 0.6 * math.exp(-0.3 * 0))

    lat_m, ctx_m = _split_mods(ada_mods(cvec, ada_w[1], ada_b[1]), b, d)
    return _odd_layer_last(x, ctx, lat_m, ctx_m, norm1_g[1], norm2_g[1], od_w_in[0], od_hy_short_w[0],
                           od_hy_short_b[0], od_hy_w1[0], od_hy_b1[0], od_hy_freq1[0], od_hy_w2[0], od_hy_b2[0],
                           od_hy_freq2[0], od_hy_w3[0], od_hy_skip[0], od_dn_conv_w[0], od_dn_alog_f[0],
                           od_dn_alog_b[0], od_dn_dtb_f[0], od_dn_dtb_b[0], od_dn_norm_g[0], od_w_out[0],
                           mlp_w1[1], mlp_w2[1], final_g)
```

```python
import functools
import math

import jax
import jax.numpy as jnp
import numpy as np
from jax import lax
from jax.experimental import pallas as pl
from jax.experimental.pallas import tpu as pltpu

F32 = jnp.float32
BF16 = jnp.bfloat16

EPS = 1e-6
N_MOD = 6
GRID_W = 64
ROPE_BASE = 10000.0
CONV_W = 31
DIFF_HEADS = 4
DIFF_DQK = 64
DIFF_DV = 128
HY_ORDER = 2
HY_EMB = 33
HY_FFN = 64
HY_MAX_DECAY = math.log(1e-2) / 0.3
HY_MIN_DECAY = math.log(1e-2) / 1.5
DN_HEADS = 4
DN_DK = 128
DN_CONV_W = 5
DN_BATCH = 2
DN_CHUNK = 128

LANE = 128
SUBLANE = 8
VMEM_LIMIT = 56 * 1024 * 1024


def _cparams(*sem):
    return pltpu.CompilerParams(dimension_semantics=sem, vmem_limit_bytes=VMEM_LIMIT)


def _tile(n, pref, mult=SUBLANE):
    if n <= pref:
        return n
    t = (pref // mult) * mult
    while t > mult and n % t:
        t -= mult
    assert n % t == 0, (n, pref, mult)
    return t


def _const_spec(shape):
    nd = len(shape)
    return pl.BlockSpec(shape, lambda *_: (0,) * nd, pipeline_mode=pl.Buffered(1))


def _silu(x):
    return x * jax.nn.sigmoid(x)


def _norm_mod(x, g, sh, sc):
    y = x * lax.rsqrt(jnp.mean(x * x, axis=-1, keepdims=True) + EPS)
    return (y * g) * (1.0 + sc) + sh


def _mods_kernel(c_ref, w_ref, b_ref, o_ref):
    cond = _silu(c_ref[...])
    o_ref[...] = jnp.dot(cond, w_ref[...], preferred_element_type=F32,
                         precision=lax.Precision.HIGHEST) + b_ref[...]


def ada_mods(cvec, w, b):
    r, d = cvec.shape
    n = w.shape[1]
    tn = _tile(n, 512, LANE)
    return pl.pallas_call(
        _mods_kernel,
        grid=(n // tn,),
        in_specs=[pl.BlockSpec((r, d), lambda j: (0, 0)),
                  pl.BlockSpec((d, tn), lambda j: (0, j)),
                  pl.BlockSpec((1, tn), lambda j: (0, j))],
        out_specs=pl.BlockSpec((r, tn), lambda j: (0, j)),
        out_shape=jax.ShapeDtypeStruct((r, n), F32),
        compiler_params=_cparams("arbitrary"),
        name="ada_mods",
    )(cvec, w, b.reshape(1, n))


def _swap16(y):
    n = y.shape[-1]
    lane = lax.broadcasted_iota(jnp.int32, y.shape, y.ndim - 1)
    fwd = pltpu.roll(y, n - 16, y.ndim - 1)
    bwd = pltpu.roll(y, 16, y.ndim - 1)
    return jnp.where((lane % 32) < 16, fwd, bwd)


PROJ_CHUNK = 512


def _proj_kernel(x_ref, g_ref, sh_ref, sc_ref, w_ref, *rest, splits, rope_cols):
    n_out = len(splits)
    if rope_cols:
        cq_ref, sq_ref, ck_ref, sk_ref = rest[:4]
        rest = rest[4:]
    o_refs = rest[:n_out]
    h = _norm_mod(x_ref[0], g_ref[...], sh_ref[0], sc_ref[0]).astype(BF16)
    for o_ref, (s0, s1) in zip(o_refs, splits):
        tc = PROJ_CHUNK if (s1 - s0) % PROJ_CHUNK == 0 and s0 % PROJ_CHUNK == 0 else LANE
        for c0 in range(s0, s1, tc):
            y = jnp.dot(h, w_ref[:, c0:c0 + tc], preferred_element_type=F32)
            if rope_cols and rope_cols[0] <= c0 < rope_cols[2]:
                is_q = c0 < rope_cols[1]
                cos = (cq_ref if is_q else ck_ref)[...]
                sin = (sq_ref if is_q else sk_ref)[...]
                reps = tc // LANE
                cos = jnp.concatenate([cos] * reps, axis=1)
                sin = jnp.concatenate([sin] * reps, axis=1)
                y = y * cos + _swap16(y) * sin
            o_ref[0, :, c0 - s0:c0 - s0 + tc] = y.astype(o_ref.dtype)


def norm_mod_proj(x, g, sh, sc, w, splits, dtypes, rope=None, rope_cols=None):
    b, l, d = x.shape
    n = w.shape[1]
    tm = _tile(l, 512)
    assert all(s % LANE == 0 for sp in splits for s in sp)
    if rope_cols:
        assert all(c % PROJ_CHUNK == 0 for c in rope_cols)
    row = lambda bi, i: (bi, i, 0)
    in_specs = [pl.BlockSpec((1, tm, d), row),
                _const_spec((1, d)),
                pl.BlockSpec((1, 1, d), lambda bi, i: (bi, 0, 0)),
                pl.BlockSpec((1, 1, d), lambda bi, i: (bi, 0, 0)),
                _const_spec((d, n))]
    args = [x, g.reshape(1, d), sh, sc, w]
    if rope_cols:
        in_specs += [pl.BlockSpec((tm, LANE), lambda bi, i: (i, 0))] * 4
        args += list(rope)
    return pl.pallas_call(
        functools.partial(_proj_kernel, splits=tuple(splits), rope_cols=rope_cols),
        grid=(b, l // tm),
        in_specs=in_specs,
        out_specs=[pl.BlockSpec((1, tm, s1 - s0), row) for s0, s1 in splits],
        out_shape=[jax.ShapeDtypeStruct((b, l, s1 - s0), dt) for (s0, s1), dt in zip(splits, dtypes)],
        compiler_params=_cparams("parallel", "parallel"),
        name="norm_mod_proj",
    )(*args)


CONV_HALO = 16


def _conformer_kernel(ac_ref, gc_ref, ap_ref, gp_ref, an_ref, gn_ref, w_ref, b_ref, lg_ref, lb_ref, o_ref,
                      u_ref, sh_ref, y_ref, *, tl, ch):
    i = pl.program_id(1)
    last = pl.num_programs(1) - 1
    left = (CONV_W - 1) // 2
    glu = lambda a, g: a * jax.nn.sigmoid(g)
    u_ref[CONV_HALO:CONV_HALO + tl, :] = glu(ac_ref[0], gc_ref[0])
    prev = glu(ap_ref[0], gp_ref[0])
    nxt = glu(an_ref[0], gn_ref[0])
    u_ref[0:CONV_HALO, :] = jnp.where(i > 0, prev, 0.0)
    u_ref[CONV_HALO + tl:CONV_HALO + tl + CONV_HALO, :] = jnp.where(i < last, nxt, 0.0)
    n = sh_ref.shape[1]
    for p in range(SUBLANE):
        sh_ref[p] = u_ref[p:p + n, :]
    rows = min(tl, 128)
    for r0 in range(0, tl, rows):
        for c0 in range(0, ch, LANE):
            acc = jnp.zeros((rows, LANE), F32) + b_ref[:, c0:c0 + LANE]
            for k in range(CONV_W):
                s = CONV_HALO - left + k + r0
                p, base = s % SUBLANE, s - s % SUBLANE
                acc = acc + w_ref[k:k + 1, c0:c0 + LANE] * sh_ref[p, base:base + rows, c0:c0 + LANE]
            y_ref[r0:r0 + rows, c0:c0 + LANE] = acc
    y = y_ref[...]
    mu = jnp.mean(y, axis=-1, keepdims=True)
    yc = y - mu
    var = jnp.mean(yc * yc, axis=-1, keepdims=True)
    z = yc * lax.rsqrt(var + EPS) * lg_ref[...] + lb_ref[...]
    o_ref[0] = _silu(z).astype(o_ref.dtype)


def conformer_conv(p, conv_w, conv_b, ln_g, ln_b):
    b, l, _ = p.shape
    ch = conv_w.shape[1]
    tl = _tile(l, 256, CONV_HALO)
    nh = tl // CONV_HALO
    nblk = l // CONV_HALO
    cur = lambda col: pl.BlockSpec((1, tl, ch), lambda bi, i: (bi, i, col))
    prv = lambda col: pl.BlockSpec((1, CONV_HALO, ch), lambda bi, i: (bi, jnp.maximum(i * nh - 1, 0), col))
    nxt = lambda col: pl.BlockSpec((1, CONV_HALO, ch), lambda bi, i: (bi, jnp.minimum((i + 1) * nh, nblk - 1), col))
    return pl.pallas_call(
        functools.partial(_conformer_kernel, tl=tl, ch=ch),
        grid=(b, l // tl),
        in_specs=[cur(0), cur(1), prv(0), prv(1), nxt(0), nxt(1),
                  _const_spec((CONV_W, ch)), _const_spec((1, ch)), _const_spec((1, ch)), _const_spec((1, ch))],
        out_specs=pl.BlockSpec((1, tl, ch), lambda bi, i: (bi, i, 0)),
        out_shape=jax.ShapeDtypeStruct((b, l, ch), BF16),
        scratch_shapes=[pltpu.VMEM((tl + 2 * CONV_HALO, ch), F32),
                        pltpu.VMEM((SUBLANE, tl + 2 * CONV_HALO - SUBLANE, ch), F32),
                        pltpu.VMEM((tl, ch), F32)],
        compiler_params=_cparams("parallel", "parallel"),
        name="conformer_conv",
    )(p, p, p, p, p, p, conv_w, conv_b.reshape(1, ch), ln_g.reshape(1, ch), ln_b.reshape(1, ch))


ATTN_QBLK = 256


ATTN_ROWS = 128
ATTN_ONES = 16
ATTN_REDO = 120.0
ATTN_REF_ROW = (DIFF_DQK, 0)


def _round_bf16(x):
    return x.astype(BF16).astype(F32)


def _diff_attn_kernel(qt_ref, k_ref, vt_ref, lam_ref, g_ref, o_ref, qm_ref, m_ref, acc_ref, *, lambda_init):
    kv = pl.program_id(3)
    tq = qm_ref.shape[2]
    qblk = min(ATTN_QBLK, tq)
    chains = [(mi, slice(q0, q0 + qblk)) for q0 in range(0, tq, qblk) for mi in range(2)]

    k = k_ref[0]
    lane = lax.broadcasted_iota(jnp.int32, k.shape, 1)
    kx = [jnp.where(lane == r, jnp.ones_like(k), k) for r in ATTN_REF_ROW]
    vtx = vt_ref[0, 0]

    def scores(mi, qs, rows=None):
        keys = kx[mi] if rows is None else kx[mi][:rows]
        return jnp.dot(keys, qm_ref[mi, :, qs].astype(BF16), preferred_element_type=F32)

    def move_reference(mi, qs, m_new):
        m_ref[mi, :, qs] = m_new
        r = ATTN_REF_ROW[mi]
        qm_ref[mi, r:r + 1, qs] = -m_new

    @pl.when(kv == 0)
    def _():
        qt = qt_ref[0].astype(F32)
        row = lax.broadcasted_iota(jnp.int32, qt.shape, 0)
        qm_ref[0] = jnp.where(row < DIFF_DQK, qt, 0.0)
        qm_ref[1] = jnp.where(row < DIFF_DQK, 0.0, qt)
        acc_ref[...] = jnp.zeros(acc_ref.shape, F32)
        first = [scores(mi, qs, ATTN_ROWS) for mi, qs in chains]
        for (mi, qs), s in zip(chains, first):
            move_reference(mi, qs, _round_bf16(jnp.max(s, axis=0, keepdims=True)))

    sc = [scores(mi, qs) for mi, qs in chains]
    mxs, pvs = [], []
    for s in sc:
        es, mx = [], None
        for r0 in range(0, s.shape[0], ATTN_ROWS):
            grp = s[r0:r0 + ATTN_ROWS]
            es.append(jnp.exp2(grp).astype(BF16))
            top = jnp.max(grp, axis=0, keepdims=True)
            mx = top if mx is None else jnp.maximum(mx, top)
        mxs.append(mx)
        pvs.append(jnp.dot(vtx, jnp.concatenate(es, axis=0), preferred_element_type=F32))
    redo = jnp.max(functools.reduce(jnp.maximum, mxs)) > ATTN_REDO

    @pl.when(jnp.logical_not(redo))
    def _():
        for (mi, qs), mx, pv in zip(chains, mxs, pvs):
            m_old = m_ref[mi, :, qs]
            m_new = _round_bf16(m_old + jnp.maximum(mx, 0.0))
            acc_ref[mi, :, qs] = (acc_ref[mi, :, qs] + pv) * jnp.exp2(m_old - m_new)
            move_reference(mi, qs, m_new)

    @pl.when(redo)
    def _():
        for (mi, qs), s in zip(chains, sc):
            m_old = m_ref[mi, :, qs]
            m_new = _round_bf16(m_old + jnp.maximum(jnp.max(s, axis=0, keepdims=True), 0.0))
            delta = m_new - m_old
            e = jnp.exp2(s - delta)
            acc_ref[mi, :, qs] = jnp.exp2(-delta) * acc_ref[mi, :, qs] + jnp.dot(vtx, e.astype(BF16),
                                                                                 preferred_element_type=F32)
            move_reference(mi, qs, m_new)

    @pl.when(kv == pl.num_programs(3) - 1)
    def _():
        lam = (jnp.exp(jnp.sum(lam_ref[0:1, :] * lam_ref[1:2, :], axis=-1, keepdims=True))
               - jnp.exp(jnp.sum(lam_ref[2:3, :] * lam_ref[3:4, :], axis=-1, keepdims=True)) + lambda_init)
        dv = DIFF_DV
        ot = (acc_ref[0, :dv, :] / acc_ref[0, dv:dv + 1, :]
              - lam * (acc_ref[1, :dv, :] / acc_ref[1, dv:dv + 1, :]))
        ot = ot * lax.rsqrt(jnp.mean(ot * ot, axis=0, keepdims=True) + EPS)
        o = ot.T * g_ref[...]
        o_ref[0] = (o * (1.0 - lambda_init)).astype(o_ref.dtype)


def diff_attention(qt, k, vt, lam_vecs, subln_g, lambda_init, k_blk):
    b, _, lq = qt.shape
    lk = k.shape[1]
    tq = _tile(lq, 2048, LANE)
    tk = _tile(lk, 1536, LANE)
    rows = vt.shape[2]
    return pl.pallas_call(
        functools.partial(_diff_attn_kernel, lambda_init=lambda_init),
        grid=(b, DIFF_HEADS, lq // tq, lk // tk),
        in_specs=[pl.BlockSpec((1, LANE, tq), lambda bi, h, i, j: (bi, h, i)),
                  pl.BlockSpec((1, tk, LANE), lambda bi, h, i, j: (bi, j, k_blk + h)),
                  pl.BlockSpec((1, 1, rows, tk), lambda bi, h, i, j: (bi, h, 0, j)),
                  _const_spec((4, DIFF_DQK)), _const_spec((1, DIFF_DV))],
        out_specs=pl.BlockSpec((1, tq, LANE), lambda bi, h, i, j: (bi, i, h)),
        out_shape=jax.ShapeDtypeStruct((b, lq, DIFF_HEADS * DIFF_DV), BF16),
        scratch_shapes=[pltpu.VMEM((2, LANE, tq), F32), pltpu.VMEM((2, 1, tq), F32),
                        pltpu.VMEM((2, rows, tq), F32)],
        compiler_params=_cparams("parallel", "parallel", "parallel", "arbitrary"),
        name="diff_attention",
    )(qt, k, vt, lam_vecs, subln_g.reshape(1, DIFF_DV))


def _delta_gate(of_ref, ob_ref, gate_ref, g_ref):
    o = of_ref[0] + ob_ref[0]
    parts = []
    for c0 in range(0, o.shape[1], DN_DK):
        t = o[:, c0:c0 + DN_DK]
        t = t * lax.rsqrt(jnp.mean(t * t, axis=-1, keepdims=True) + EPS) * g_ref[...]
        parts.append((t * _silu(gate_ref[0, :, c0:c0 + DN_DK])).astype(BF16))
    return jnp.concatenate(parts, axis=1)


def _out_mlp_kernel(x_ref, a_ref, *refs, a_transposed, hidden_chunk, final_norm, delta_b):
    if delta_b:
        mix_b = _delta_gate(*refs[:4])
        refs = refs[4:]
    else:
        mix_b = refs[0][0]
        refs = refs[1:]
    wo_ref, gt1_ref, g2_ref, sh2_ref, sc2_ref, gt2_ref, w1_ref, w2_ref, fg_ref, o_ref = refs
    half = wo_ref.shape[0] // 2
    if a_transposed:
        ya = lax.dot_general(a_ref[0].astype(BF16), wo_ref[0:half, :], (((0,), (0,)), ((), ())),
                             preferred_element_type=F32)
    else:
        ya = jnp.dot(a_ref[0], wo_ref[0:half, :], preferred_element_type=F32)
    y = ya + jnp.dot(mix_b, wo_ref[half:, :], preferred_element_type=F32)
    x1 = x_ref[0] + gt1_ref[0] * y
    h = _norm_mod(x1, g2_ref[...], sh2_ref[0], sc2_ref[0]).astype(BF16)
    acc = jnp.zeros_like(x1)
    for c0 in range(0, w1_ref.shape[1], hidden_chunk):
        t = jnp.maximum(jnp.dot(h, w1_ref[:, c0:c0 + hidden_chunk], preferred_element_type=F32), 0.0)
        acc = acc + jnp.dot((t * t).astype(BF16), w2_ref[c0:c0 + hidden_chunk, :], preferred_element_type=F32)
    x2 = x1 + gt2_ref[0] * acc
    if final_norm:
        x2 = x2 * lax.rsqrt(jnp.mean(x2 * x2, axis=-1, keepdims=True) + EPS) * fg_ref[...]
    o_ref[0] = x2


def out_proj_mlp(x, mix_a, mix_b, w_out, gt1, g2, sh2, sc2, gt2, w1, w2, final_g=None, a_transposed=False):
    b, l, d = x.shape
    delta_b = isinstance(mix_b, tuple)
    c = (mix_b[0] if delta_b else mix_b).shape[2]
    hid = w1.shape[1]
    tm = _tile(l, 512, LANE)
    row = lambda bi, i: (bi, i, 0)
    vec = lambda bi, i: (bi, 0, 0)
    a_spec = (pl.BlockSpec((1, c, tm), lambda bi, i: (bi, 0, i)) if a_transposed
              else pl.BlockSpec((1, tm, c), row))
    if delta_b:
        b_args = list(mix_b[:3]) + [mix_b[3].reshape(1, DN_DK)]
        b_specs = [pl.BlockSpec((1, tm, c), row)] * 3 + [_const_spec((1, DN_DK))]
    else:
        b_args, b_specs = [mix_b], [pl.BlockSpec((1, tm, c), row)]
    fg = jnp.ones((1, d), F32) if final_g is None else final_g.reshape(1, d)
    return pl.pallas_call(
        functools.partial(_out_mlp_kernel, a_transposed=a_transposed, hidden_chunk=min(hid, 1024),
                          final_norm=final_g is not None, delta_b=delta_b),
        grid=(b, l // tm),
        in_specs=[pl.BlockSpec((1, tm, d), row), a_spec] + b_specs + [
                  _const_spec((2 * c, d)),
                  pl.BlockSpec((1, 1, d), vec), _const_spec((1, d)),
                  pl.BlockSpec((1, 1, d), vec), pl.BlockSpec((1, 1, d), vec), pl.BlockSpec((1, 1, d), vec),
                  _const_spec((d, hid)), _const_spec((hid, d)), _const_spec((1, d))],
        out_specs=pl.BlockSpec((1, tm, d), row),
        out_shape=jax.ShapeDtypeStruct((b, l, d), F32),
        compiler_params=_cparams("parallel", "parallel"),
        name="out_proj_mlp",
    )(x, mix_a, *b_args, w_out, gt1, g2.reshape(1, d), sh2, sc2, gt2, w1, w2, fg)


def _rope_tables(length, scale):
    f32 = np.float32
    rows = length // GRID_W
    row = np.repeat(np.arange(rows, dtype=f32), GRID_W)
    col = np.tile(np.arange(GRID_W, dtype=f32), rows)
    n_freq = DIFF_DQK // 4
    inv = (f32(ROPE_BASE) ** (-np.arange(n_freq, dtype=f32) / f32(n_freq))).astype(f32)
    ang_r = row[:, None] * inv
    ang_c = col[:, None] * inv
    cos = np.concatenate([np.cos(ang_r)] * 2 + [np.cos(ang_c)] * 2, axis=-1)
    sin = np.concatenate([-np.sin(ang_r), np.sin(ang_r), -np.sin(ang_c), np.sin(ang_c)], axis=-1)
    cos = np.concatenate([cos, cos], axis=-1) * f32(scale)
    sin = np.concatenate([sin, sin], axis=-1) * f32(scale)
    return cos.astype(f32), sin.astype(f32)


def _flat_tables(length, scale):
    return np.full((length, LANE), scale, np.float32), np.zeros((length, LANE), np.float32)


def _proj_t_kernel(x_ref, g_ref, sh_ref, sc_ref, wt_ref, o_ref, *, rc):
    h = _norm_mod(x_ref[0], g_ref[...], sh_ref[0], sc_ref[0]).astype(BF16)
    for r0 in range(0, wt_ref.shape[0], rc):
        o_ref[0, r0:r0 + rc, :] = lax.dot_general(wt_ref[r0:r0 + rc, :], h, (((1,), (1,)), ((), ())),
                                                  preferred_element_type=F32)


def norm_mod_proj_t(x, g, sh, sc, wt):
    b, l, d = x.shape
    c = wt.shape[0]
    tl = _tile(l, 512, LANE)
    return pl.pallas_call(
        functools.partial(_proj_t_kernel, rc=_tile(c, 512)),
        grid=(b, l // tl),
        in_specs=[pl.BlockSpec((1, tl, d), lambda bi, i: (bi, i, 0)),
                  _const_spec((1, d)),
                  pl.BlockSpec((1, 1, d), lambda bi, i: (bi, 0, 0)),
                  pl.BlockSpec((1, 1, d), lambda bi, i: (bi, 0, 0)),
                  _const_spec((c, d))],
        out_specs=pl.BlockSpec((1, c, tl), lambda bi, i: (bi, 0, i)),
        out_shape=jax.ShapeDtypeStruct((b, c, l), F32),
        compiler_params=_cparams("parallel", "parallel"),
        name="norm_mod_proj_t",
    )(x, g.reshape(1, d), sh, sc, wt)


def _dot_hi(a, b):
    return jnp.dot(a, b, preferred_element_type=F32, precision=lax.Precision.HIGHEST)


def _hy_hidden_kernel(z_ref, w1_ref, b1_ref, f1_ref, w2_ref, b2_ref, f2_ref, o_ref):
    hid = jnp.sin(f1_ref[...] * (_dot_hi(w1_ref[...], z_ref[...]) + b1_ref[...]))
    o_ref[...] = jnp.sin(f2_ref[...] * (_dot_hi(w2_ref[...], hid) + b2_ref[...]))


def hyena_hidden(zt, w1t, b1, f1, w2t, b2, f2):
    e, n = zt.shape
    f = w1t.shape[0]
    tn = _tile(n, 2048, LANE)
    col = lambda v: v.reshape(f, 1)
    return pl.pallas_call(
        _hy_hidden_kernel,
        grid=(n // tn,),
        in_specs=[pl.BlockSpec((e, tn), lambda j: (0, j)),
                  _const_spec((f, e)), _const_spec((f, 1)), _const_spec((f, 1)),
                  _const_spec((f, f)), _const_spec((f, 1)), _const_spec((f, 1))],
        out_specs=pl.BlockSpec((f, tn), lambda j: (0, j)),
        out_shape=jax.ShapeDtypeStruct((f, n), F32),
        compiler_params=_cparams("parallel"),
        name="hyena_hidden",
    )(zt, w1t, col(b1), col(f1), w2t, col(b2), col(f2))


def _hy_filter_kernel(hid_ref, t_ref, wf_ref, wb_ref, dl_ref, o_ref, *, length, tn):
    n2 = hid_ref.shape[1]
    delta = dl_ref[...]

    def piece(c0):
        hid = hid_ref[:, c0:c0 + tn]
        dec = jnp.exp(-t_ref[:, c0:c0 + tn] * delta)
        return _dot_hi(wf_ref[...], hid) * dec, _dot_hi(wb_ref[...], hid) * dec

    norm = jnp.zeros((wf_ref.shape[0], 1), F32)
    for c0 in range(length, n2, tn):
        ff, fb = piece(c0)
        norm = norm + jnp.sum(jnp.abs(ff) + jnp.abs(fb), axis=-1, keepdims=True)
    inv = 1.0 / norm
    for c0 in range(0, n2, tn):
        ff, fb = piece(c0)
        if c0 >= length:
            o_ref[0, :, c0:c0 + tn] = ff * inv
        else:
            col = lax.broadcasted_iota(jnp.int32, fb.shape, 1) + c0
            o_ref[0, :, c0:c0 + tn] = jnp.where(col == 0, 0.0, fb * inv)


def hyena_filters(hid, t_row, w3t, deltas, length):
    f, n2 = hid.shape
    ch = deltas.shape[0]
    cb = LANE
    nblk = ch // cb
    tn = _tile(length, 2048, LANE)
    return pl.pallas_call(
        functools.partial(_hy_filter_kernel, length=length, tn=tn),
        grid=(HY_ORDER, nblk),
        in_specs=[_const_spec((f, n2)), _const_spec((1, n2)),
                  pl.BlockSpec((cb, f), lambda o, c: (o * 2 * nblk + c, 0)),
                  pl.BlockSpec((cb, f), lambda o, c: (o * 2 * nblk + nblk + c, 0)),
                  pl.BlockSpec((cb, 1), lambda o, c: (c, 0))],
        out_specs=pl.BlockSpec((1, cb, n2), lambda o, c: (o, c, 0)),
        out_shape=jax.ShapeDtypeStruct((HY_ORDER, ch, n2), F32),
        compiler_params=_cparams("parallel", "parallel"),
        name="hyena_filters",
    )(hid, t_row, w3t, w3t, deltas.reshape(ch, 1))


HY_T = 256
HY_CB = SUBLANE
HY_PAIR = 2

def _hyena_kernel(v_ref, x1_ref, x2_ref, g_ref, sw_ref, sb_ref, sk_ref_s, o_ref, sk_ref, u_ref, acc_ref,
                  *, length, nch):
    nb = v_ref.shape[0]
    nj = length // HY_T
    cblk = pl.program_id(0)
    lane = lax.broadcasted_iota(jnp.int32, (nb, length), 1)

    def short(ref, cc, part):
        x = ref[:, cc, :]
        ch = part * nch + cblk * HY_CB + cc
        prev = jnp.where(lane == 0, 0.0, pltpu.roll(x, 1, 1))
        nxt = jnp.where(lane == length - 1, 0.0, pltpu.roll(x, length - 1, 1))
        return sw_ref[0, ch] * prev + sw_ref[1, ch] * x + sw_ref[2, ch] * nxt + sb_ref[ch]

    def build_skew(order, cc, slot):
        g = jnp.broadcast_to(g_ref[order, pl.ds(cc, 1), :], (2 * SUBLANE, 2 * length))
        half = HY_T // 2
        for r0 in range(0, half, 2 * SUBLANE):
            pair = pltpu.roll(g, r0, 1, stride=1, stride_axis=0).astype(BF16)
            sk_ref[slot, r0:r0 + 2 * SUBLANE, :] = pair
            sk_ref[slot, half + r0:half + r0 + 2 * SUBLANE, half:] = pair[:, :2 * length - half]

    def long_conv(u, slot):
        rows = nj * nb
        x2 = jnp.concatenate([u[:, j * HY_T:(j + 1) * HY_T] for j in range(nj)]
                             + [jnp.zeros((2 * nb, HY_T), F32)], axis=0)
        u_ref[slot, 0] = x2.astype(BF16)
        u_ref[slot, 1] = jnp.concatenate([x2[nb:, :], x2[rows:rows + nb, :]], axis=0).astype(BF16)
        acc_ref[slot] = jnp.zeros(acc_ref.shape[1:], F32)
        for d in range(-(nj - 1), nj):
            m = (nj - abs(d)) * nb
            lo_in, lo_out = max(0, -d) * nb, max(0, d) * nb
            which, off = (0, lo_in) if lo_in % (2 * nb) == 0 else (1, lo_in - nb)
            mm = -(-m // (2 * nb)) * (2 * nb)
            tile = sk_ref[slot, :, length + d * HY_T:length + (d + 1) * HY_T]
            part = jnp.dot(u_ref[slot, which, off:off + mm, :], tile, preferred_element_type=F32)
            acc_ref[slot, lo_out:lo_out + m, :] += part[:m]
        return jnp.concatenate([acc_ref[slot, i * nb:(i + 1) * nb, :] for i in range(nj)], axis=1)

    def channels(p, carry):
        ccs = [p * HY_PAIR + s for s in range(HY_PAIR)]
        vs = [short(v_ref, cc, 0) for cc in ccs]
        x1s = [short(x1_ref, cc, 1) for cc in ccs]
        x2s = [short(x2_ref, cc, 2) for cc in ccs]
        zs = vs
        for order, gates in enumerate((x1s, x2s)):
            for slot, cc in enumerate(ccs):
                build_skew(order, cc, slot)
            ys = [long_conv(z, slot) for slot, z in enumerate(zs)]
            zs = [g * (y + sk_ref_s[order, cblk * HY_CB + cc] * z) for g, y, z, cc in zip(gates, ys, zs, ccs)]
        for cc, z in zip(ccs, zs):
            o_ref[:, cc, :] = z
        return carry

    lax.fori_loop(0, HY_CB // HY_PAIR, channels, 0)


def hyena_mix(pt, filt, short_w, short_b, skip):
    b, c3, l = pt.shape
    nch = c3 // 3
    assert l % HY_T == 0 and nch % HY_CB == 0 and b == SUBLANE
    nblk = nch // HY_CB
    smem = pl.BlockSpec(memory_space=pltpu.SMEM)
    part = lambda k: pl.BlockSpec((b, HY_CB, l), lambda c: (0, k * nblk + c, 0))
    return pl.pallas_call(
        functools.partial(_hyena_kernel, length=l, nch=nch),
        grid=(nblk,),
        in_specs=[part(0), part(1), part(2),
                  pl.BlockSpec((HY_ORDER, HY_CB, 2 * l), lambda c: (0, c, 0)),
                  smem, smem, smem],
        out_specs=pl.BlockSpec((b, HY_CB, l), lambda c: (0, c, 0)),
        out_shape=jax.ShapeDtypeStruct((b, nch, l), F32),
        scratch_shapes=[pltpu.VMEM((HY_PAIR, HY_T, 2 * l), BF16),
                        pltpu.VMEM((HY_PAIR, 2, l // HY_T * b + 2 * b, HY_T), BF16),
                        pltpu.VMEM((HY_PAIR, l // HY_T * b, HY_T), F32)],
        compiler_params=_cparams("parallel"),
        name="hyena_mix",
    )(pt, pt, pt, filt, short_w, short_b, skip)


DN_HALO = SUBLANE
DN_BETA_F, DN_BETA_B, DN_G_F, DN_G_B = 0, DN_HEADS, 2 * DN_HEADS, 3 * DN_HEADS


def _dn_feat_kernel(c_ref, p_ref, n_ref, s_ref, w_ref, al_ref, dt_ref, q_ref, k_ref, v_ref, f_ref, u_ref, *, tl):
    i = pl.program_id(1)
    last = pl.num_programs(1) - 1
    left = (DN_CONV_W - 1) // 2
    nc = c_ref.shape[2]
    u_ref[DN_HALO:DN_HALO + tl, :] = c_ref[0]
    u_ref[0:DN_HALO, :] = jnp.where(i > 0, p_ref[0], 0.0)
    u_ref[DN_HALO + tl:DN_HALO + tl + DN_HALO, :] = jnp.where(i < last, n_ref[0], 0.0)
    rows = min(tl, 128)
    outs = (q_ref, k_ref, v_ref)
    per = nc // len(outs)
    for r0 in range(0, tl, rows):
        for c0 in range(0, nc, LANE):
            acc = jnp.zeros((rows, LANE), F32)
            for k in range(DN_CONV_W):
                s = DN_HALO - left + k + r0
                acc = acc + w_ref[k:k + 1, c0:c0 + LANE] * u_ref[s:s + rows, c0:c0 + LANE]
            y = _silu(acc)
            which, off = divmod(c0, per)
            if which < 2:
                y = y * lax.rsqrt(jnp.sum(y * y, axis=-1, keepdims=True) + 1e-6)
            outs[which][0, r0:r0 + rows, off:off + LANE] = y
    s = s_ref[0]
    lane = lax.broadcasted_iota(jnp.int32, s.shape, 1)
    gate = -jnp.exp(al_ref[...]) * jax.nn.softplus(s + dt_ref[...])
    f_ref[0] = jnp.where(lane < DN_G_F, jax.nn.sigmoid(s), gate)


def dn_features(qkv, small, conv_w, alog_row, dtb_row):
    b, l, nc = qkv.shape
    tl = _tile(l, 256, DN_HALO)
    nh = tl // DN_HALO
    nblk = l // DN_HALO
    w = nc // 3
    row = lambda bi, i: (bi, i, 0)
    return pl.pallas_call(
        functools.partial(_dn_feat_kernel, tl=tl),
        grid=(b, l // tl),
        in_specs=[pl.BlockSpec((1, tl, nc), row),
                  pl.BlockSpec((1, DN_HALO, nc), lambda bi, i: (bi, jnp.maximum(i * nh - 1, 0), 0)),
                  pl.BlockSpec((1, DN_HALO, nc), lambda bi, i: (bi, jnp.minimum((i + 1) * nh, nblk - 1), 0)),
                  pl.BlockSpec((1, tl, LANE), row),
                  _const_spec((DN_CONV_W, nc)), _const_spec((1, LANE)), _const_spec((1, LANE))],
        out_specs=[pl.BlockSpec((1, tl, w), row)] * 3 + [pl.BlockSpec((1, tl, LANE), row)],
        out_shape=[jax.ShapeDtypeStruct((b, l, w), F32)] * 3 + [jax.ShapeDtypeStruct((b, l, LANE), F32)],
        scratch_shapes=[pltpu.VMEM((tl + 2 * DN_HALO, nc), F32)],
        compiler_params=_cparams("parallel", "parallel"),
        name="dn_features",
    )(qkv, qkv, qkv, small, conv_w, alog_row, dtb_row)


def _bdot(a, b):
    return jnp.dot(a.astype(BF16), b.astype(BF16), preferred_element_type=F32)


def _bdot_nt(a, b):
    return lax.dot_general(a.astype(BF16), b.astype(BF16), (((1,), (1,)), ((), ())), preferred_element_type=F32)


def _bdot_tn(a, b):
    return lax.dot_general(a.astype(BF16), b.astype(BF16), (((0,), (0,)), ((), ())), preferred_element_type=F32)


def _cumsum_rows(x, reverse):
    n = x.shape[0]
    row = lax.broadcasted_iota(jnp.int32, x.shape, 0)
    s = 1
    while s < n:
        if reverse:
            x = x + jnp.where(row < n - s, pltpu.roll(x, n - s, 0), 0.0)
        else:
            x = x + jnp.where(row >= s, pltpu.roll(x, s, 0), 0.0)
        s *= 2
    return x


def _dn_chunks(chains):
    c = chains[0]["k"].shape[0]
    ri = lax.broadcasted_iota(jnp.int32, (c, c), 0)
    ci = lax.broadcasted_iota(jnp.int32, (c, c), 1)
    work = []
    for ch in chains:
        rev = ch["reverse"]
        incl = (ri <= ci) if rev else (ri >= ci)
        strict = (ri < ci) if rev else (ri > ci)
        decay = jnp.where(incl, jnp.exp(jnp.where(incl, ch["gcol"] - ch["grow"], 0.0)), 0.0)
        kb = ch["k"] * ch["beta"]
        qs = ch["q"] * (ch["k"].shape[1] ** -0.5)
        work.append(dict(ch, strict=strict, decay=decay, kb=kb, qs=qs))
    grams = [_bdot_nt(jnp.concatenate([w["kb"], w["qs"]], axis=0), w["k"]) for w in work]
    lowers = [jnp.where(w["strict"], g[:c] * w["decay"], 0.0) for w, g in zip(work, grams)]
    same2 = (ri // 2) == (ci // 2)
    ymats = [-jnp.where(same2, lo, 0.0) for lo in lowers]
    m = 2
    while m < c:
        pair = ((ri // (2 * m)) == (ci // (2 * m))) & ((ri // m) != (ci // m))
        offs = [jnp.where(pair, lo, 0.0) for lo in lowers]
        zs = [off + _bdot(ym, off) for ym, off in zip(ymats, offs)]
        ymats = [ym - z - _bdot(z, ym) for ym, z in zip(ymats, zs)]
        m *= 2
    outs = []
    uws, e_gs = [], []
    for w, ym in zip(work, ymats):
        e_g = jnp.exp(w["gcol"])
        rhs = jnp.concatenate([w["v"] * w["beta"], w["kb"] * e_g], axis=1)
        uws.append(rhs + _bdot(ym, rhs))
        e_gs.append(e_g)
    dv = work[0]["v"].shape[1]
    wss = [_bdot(jnp.concatenate([uw[:, dv:], w["qs"] * e_g], axis=0), w["state"])
           for w, uw, e_g in zip(work, uws, e_gs)]
    v_news = [uw[:, :dv] - ws[:c] for uw, ws in zip(uws, wss)]
    for w, g, ws, v_new in zip(work, grams, wss, v_news):
        gcol = w["gcol"]
        g_last = gcol[0:1, :] if w["reverse"] else gcol[c - 1:c, :]
        k_dec = w["k"] * jnp.exp(g_last - gcol)
        o = ws[c:] + _bdot(g[c:] * w["decay"], v_new)
        new_state = w["state"] * jnp.exp(g_last) + _bdot_tn(k_dec, v_new)
        outs.append((o, new_state))
    return outs


def _dn_scan_kernel(qf_ref, kf_ref, vf_ref, ff_ref, qb_ref, kb_ref, vb_ref, fb_ref, s0f_ref, s0b_ref,
                    of_ref, ob_ref, sf_ref, sb_ref, st_ref):
    n = pl.program_id(1)

    @pl.when(n == 0)
    def _():
        st_ref[0] = s0f_ref[...]
        st_ref[1] = s0b_ref[...]

    dirs = ((qf_ref, kf_ref, vf_ref, ff_ref, of_ref, False), (qb_ref, kb_ref, vb_ref, fb_ref, ob_ref, True))
    chains, dests = [], []
    for bi in range(qf_ref.shape[0]):
        for di, (q_ref, k_ref, v_ref, f_ref, o_ref, reverse) in enumerate(dirs):
            feats = f_ref[bi]
            gc = _cumsum_rows(feats, reverse)
            gct = gc.T
            for h in range(DN_HEADS):
                cols = slice(h * DN_DK, (h + 1) * DN_DK)
                bl = (DN_BETA_B if reverse else DN_BETA_F) + h
                gl = (DN_G_B if reverse else DN_G_F) + h
                chains.append(dict(q=q_ref[bi, :, cols], k=k_ref[bi, :, cols], v=v_ref[bi, :, cols],
                                   beta=feats[:, bl:bl + 1], gcol=gc[:, gl:gl + 1], grow=gct[gl:gl + 1, :],
                                   state=st_ref[di, bi, h], reverse=reverse))
                dests.append((o_ref, cols, di, bi, h))
    for (o_ref, cols, di, bi, h), (o, s_new) in zip(dests, _dn_chunks(chains)):
        o_ref[bi, :, cols] = o
        st_ref[di, bi, h] = s_new

    @pl.when(n == pl.num_programs(1) - 1)
    def _():
        sf_ref[...] = st_ref[0]
        sb_ref[...] = st_ref[1]


def dn_scan(q, k, v, feats, s0f, s0b):
    b, l, w = q.shape
    nchunk = l // DN_CHUNK
    nb = DN_BATCH if b % DN_BATCH == 0 else 1
    fwd = lambda bi, n: (bi, n, 0)
    bwd = lambda bi, n: (bi, nchunk - 1 - n, 0)
    st = lambda bi, n: (bi, 0, 0, 0)
    blk = lambda im, width: pl.BlockSpec((nb, DN_CHUNK, width), im)
    st_spec = pl.BlockSpec((nb, DN_HEADS, DN_DK, DN_DK), st)
    st_shape = jax.ShapeDtypeStruct((b, DN_HEADS, DN_DK, DN_DK), F32)
    return pl.pallas_call(
        _dn_scan_kernel,
        grid=(b // nb, nchunk),
        in_specs=[blk(fwd, w), blk(fwd, w), blk(fwd, w), blk(fwd, LANE),
                  blk(bwd, w), blk(bwd, w), blk(bwd, w), blk(bwd, LANE), st_spec, st_spec],
        out_specs=[blk(fwd, w), blk(bwd, w), st_spec, st_spec],
        out_shape=[jax.ShapeDtypeStruct((b, l, w), F32)] * 2 + [st_shape, st_shape],
        scratch_shapes=[pltpu.VMEM((2, nb, DN_HEADS, DN_DK, DN_DK), F32)],
        compiler_params=_cparams("parallel", "arbitrary"),
        name="dn_scan",
    )(q, k, v, feats, q, k, v, feats, s0f, s0b)


def _split_mods(mods, b, d):
    lat = [mods[:b, k * d:(k + 1) * d].reshape(b, 1, d) for k in range(N_MOD)]
    ctx = [jnp.broadcast_to(mods[b, k * d:(k + 1) * d].reshape(1, 1, d), (b, 1, d)) for k in range(N_MOD)]
    return lat, ctx


def _even_layer(x, ctx, lat_m, ctx_m, n1g, n2g, w_in, conv_w, conv_b, ln_g, ln_b, lam_vecs, subln_g, w_out,
                w1, w2, lambda_init):
    b, l, d = x.shape
    lc = ctx.shape[1]
    ch = conv_w.shape[1]
    qk = DIFF_HEADS * 2 * DIFF_DQK
    q0, k0, v0, n_in = 2 * ch, 2 * ch + qk, 2 * ch + 2 * qk, 2 * ch + 2 * qk + DIFF_HEADS * DIFF_DV
    scale = DIFF_DQK ** -0.5 * math.log2(math.e)
    splits = [(0, q0), (q0, n_in)]
    w_in = w_in.astype(BF16)
    rope_lat = _rope_tables(l, scale) + _rope_tables(l, 1.0)
    rope_ctx = _flat_tables(lc, scale) + _flat_tables(lc, 1.0)
    pc_lat, qkv_lat = norm_mod_proj(x, n1g, lat_m[0], lat_m[1], w_in, splits, (F32, BF16), rope_lat, (q0, k0, v0))
    pc_ctx, qkv_ctx = norm_mod_proj(ctx, n1g, ctx_m[0], ctx_m[1], w_in, splits, (F32, BF16), rope_ctx, (q0, k0, v0))
    conv_lat = conformer_conv(pc_lat, conv_w, conv_b, ln_g, ln_b)
    conv_ctx = conformer_conv(pc_ctx, conv_w, conv_b, ln_g, ln_b)
    k_all = jnp.concatenate([qkv_lat[:, :, qk:2 * qk], qkv_ctx[:, :, qk:2 * qk]], axis=1)
    def heads_t(v):
        vt = jnp.swapaxes(v, 1, 2).reshape(b, DIFF_HEADS, DIFF_DV, v.shape[1])
        return jnp.concatenate([vt, jnp.ones((b, DIFF_HEADS, ATTN_ONES, v.shape[1]), vt.dtype)], axis=2)

    vt_ctx = heads_t(qkv_ctx[:, :, 2 * qk:])
    vt_all = jnp.concatenate([heads_t(qkv_lat[:, :, 2 * qk:]), vt_ctx], axis=3)
    qt_lat = jnp.swapaxes(qkv_lat[:, :, :qk], 1, 2)
    qt_ctx = jnp.swapaxes(qkv_ctx[:, :, :qk], 1, 2)
    o_lat = diff_attention(qt_lat, k_all, vt_all, lam_vecs, subln_g, lambda_init, 0)
    o_ctx = diff_attention(qt_ctx, qkv_ctx, vt_ctx, lam_vecs, subln_g, lambda_init, qk // LANE)
    w_out, w1, w2 = w_out.astype(BF16), w1.astype(BF16), w2.astype(BF16)
    x = out_proj_mlp(x, conv_lat, o_lat, w_out, lat_m[2], n2g, lat_m[3], lat_m[4], lat_m[5], w1, w2)
    ctx = out_proj_mlp(ctx, conv_ctx, o_ctx, w_out, ctx_m[2], n2g, ctx_m[3], ctx_m[4], ctx_m[5], w1, w2)
    return x, ctx


def _hyena_position_tables(length):
    f32 = np.float32
    t = np.linspace(0.0, 1.0, length, dtype=f32)[:, None]
    bands = (HY_EMB - 1) // 2
    omega = (f32(2.0 * math.pi) * np.arange(length, dtype=f32)[:, None] / f32(length)).astype(f32)
    ang = omega * np.linspace(1e-4, bands - 1, bands, dtype=f32)
    z = np.concatenate([t, np.cos(ang), -np.sin(ang)], axis=-1).astype(f32)
    pos = np.minimum(np.abs(np.arange(2 * length) - length), length - 1)
    pad = -HY_EMB % SUBLANE
    zt = np.pad(z[pos].T, ((0, pad), (0, 0)))
    return np.ascontiguousarray(zt), np.ascontiguousarray(t[pos].T)


def _odd_layer_last(x, ctx, lat_m, ctx_m, n1g, n2g, w_in, hy_short_w, hy_short_b, hy_w1, hy_b1, hy_f1, hy_w2, hy_b2,
                    hy_f2, hy_w3, hy_skip, dn_conv_w, alog_f, alog_b, dtb_f, dtb_b, dn_norm_g, w_out, w1, w2,
                    final_g):
    b, l, d = x.shape
    hy_in = hy_short_w.shape[1]
    hy_ch = hy_in // (HY_ORDER + 1)
    dn_w = DN_HEADS * DN_DK
    n_rest = w_in.shape[1] - hy_in
    n_pad = -n_rest % LANE
    w_rest = jnp.pad(w_in[:, hy_in:], ((0, 0), (0, n_pad))).astype(BF16)
    splits = [(0, dn_w), (dn_w, 4 * dn_w), (4 * dn_w, n_rest + n_pad)]
    f32x3 = (F32, F32, F32)
    gate_lat, qkv_lat, small_lat = norm_mod_proj(x, n1g, lat_m[0], lat_m[1], w_rest, splits, f32x3)
    _, qkv_ctx, small_ctx = norm_mod_proj(ctx, n1g, ctx_m[0], ctx_m[1], w_rest, splits, f32x3)

    pt = norm_mod_proj_t(x, n1g, lat_m[0], lat_m[1], w_in[:, :hy_in].T.astype(BF16))
    zt, t_row = _hyena_position_tables(l)
    w1t = jnp.pad(hy_w1.T, ((0, 0), (0, zt.shape[0] - hy_w1.shape[0])))
    hid = hyena_hidden(zt, w1t, hy_b1, hy_f1, hy_w2.T, hy_b2, hy_f2)
    deltas = jnp.abs(jnp.linspace(HY_MIN_DECAY, HY_MAX_DECAY, hy_ch, dtype=F32))
    filt = hyena_filters(hid, t_row, hy_w3.T, deltas, l)
    hy = hyena_mix(pt, filt, hy_short_w, hy_short_b, hy_skip)

    lanes = jnp.zeros((LANE,), F32)
    alog_row = lanes.at[DN_G_F:DN_G_F + DN_HEADS].set(alog_f).at[DN_G_B:DN_G_B + DN_HEADS].set(alog_b).reshape(1, LANE)
    dtb_row = lanes.at[DN_G_F:DN_G_F + DN_HEADS].set(dtb_f).at[DN_G_B:DN_G_B + DN_HEADS].set(dtb_b).reshape(1, LANE)
    q_c, k_c, v_c, f_c = dn_features(qkv_ctx, small_ctx, dn_conv_w, alog_row, dtb_row)
    q_l, k_l, v_l, f_l = dn_features(qkv_lat, small_lat, dn_conv_w, alog_row, dtb_row)
    s0 = jnp.zeros((b, DN_HEADS, DN_DK, DN_DK), F32)
    _, _, s_cf, s_cb = dn_scan(q_c, k_c, v_c, f_c, s0, s0)
    o_f, o_b, _, _ = dn_scan(q_l, k_l, v_l, f_l, s_cf, s_cb)
    dn = (o_f, o_b, gate_lat, dn_norm_g)

    return out_proj_mlp(x, hy, dn, w_out.astype(BF16), lat_m[2], n2g, lat_m[3], lat_m[4], lat_m[5],
                        w1.astype(BF16), w2.astype(BF16), final_g=final_g, a_transposed=True)


def kernel(x, c, ctx, c_ctx, ada_w, ada_b, norm1_g, norm2_g, mlp_w1, mlp_w2, ev_w_in, ev_conv_w, ev_conv_b, ev_ln_g,
           ev_ln_b, ev_lq1, ev_lk1, ev_lq2, ev_lk2, ev_subln_g, ev_w_out, od_w_in, od_hy_short_w, od_hy_short_b,
           od_hy_w1, od_hy_b1, od_hy_freq1, od_hy_w2, od_hy_b2, od_hy_freq2, od_hy_w3, od_hy_skip, od_dn_conv_w,
           od_dn_alog_f, od_dn_alog_b, od_dn_dtb_f, od_dn_dtb_b, od_dn_norm_g, od_w_out, final_g):
    b, _, d = x.shape
    assert ada_w.shape[0] == 2, "layer 0 = conformer/diff-attention, layer 1 (last) = Hyena/DeltaNet"
    rows = -(-(b + 1) // SUBLANE) * SUBLANE
    cvec = jnp.zeros((rows, d), F32).at[:b].set(c).at[b].set(c_ctx)

    lat_m, ctx_m = _split_mods(ada_mods(cvec, ada_w[0], ada_b[0]), b, d)
    lam_vecs = jnp.stack([ev_lq1[0], ev_lk1[0], ev_lq2[0], ev_lk2[0]])
    x, ctx = _even_layer(x, ctx, lat_m, ctx_m, norm1_g[0], norm2_g[0], ev_w_in[0], ev_conv_w[0], ev_conv_b[0],
                         ev_ln_g[0], ev_ln_b[0], lam_vecs, ev_subln_g[0], ev_w_out[0], mlp_w1[0], mlp_w2[0],
                         0.8 - 0.6 * math.exp(-0.3 * 0))

    lat_m, ctx_m = _split_mods(ada_mods(cvec, ada_w[1], ada_b[1]), b, d)
    return _odd_layer_last(x, ctx, lat_m, ctx_m, norm1_g[1], norm2_g[1], od_w_in[0], od_hy_short_w[0],
                           od_hy_short_b[0], od_hy_w1[0], od_hy_b1[0], od_hy_freq1[0], od_hy_w2[0], od_hy_b2[0],
                           od_hy_freq2[0], od_hy_w3[0], od_hy_skip[0], od_dn_conv_w[0], od_dn_alog_f[0],
                           od_dn_alog_b[0], od_dn_dtb_f[0], od_dn_dtb_b[0], od_dn_norm_g[0], od_w_out[0],
                           mlp_w1[1], mlp_w2[1], final_g)
```

```python
import functools
import math

import jax
import jax.numpy as jnp
import numpy as np
from jax import lax
from jax.experimental import pallas as pl
from jax.experimental.pallas import tpu as pltpu

F32 = jnp.float32
BF16 = jnp.bfloat16

EPS = 1e-6
N_MOD = 6
GRID_W = 64
ROPE_BASE = 10000.0
CONV_W = 31
DIFF_HEADS = 4
DIFF_DQK = 64
DIFF_DV = 128
HY_ORDER = 2
HY_EMB = 33
HY_FFN = 64
HY_MAX_DECAY = math.log(1e-2) / 0.3
HY_MIN_DECAY = math.log(1e-2) / 1.5
DN_HEADS = 4
DN_DK = 128
DN_CONV_W = 5
DN_BATCH = 2
DN_CHUNK = 128

LANE = 128
SUBLANE = 8
VMEM_LIMIT = 56 * 1024 * 1024


def _cparams(*sem):
    return pltpu.CompilerParams(dimension_semantics=sem, vmem_limit_bytes=VMEM_LIMIT)


def _tile(n, pref, mult=SUBLANE):
    if n <= pref:
        return n
    t = (pref // mult) * mult
    while t > mult and n % t:
        t -= mult
    assert n % t == 0, (n, pref, mult)
    return t


def _const_spec(shape):
    nd = len(shape)
    return pl.BlockSpec(shape, lambda *_: (0,) * nd, pipeline_mode=pl.Buffered(1))


def _silu(x):
    return x * jax.nn.sigmoid(x)


def _norm_mod(x, g, sh, sc):
    y = x * lax.rsqrt(jnp.mean(x * x, axis=-1, keepdims=True) + EPS)
    return (y * g) * (1.0 + sc) + sh


def _mods_kernel(c_ref, w_ref, b_ref, o_ref):
    cond = _silu(c_ref[...])
    o_ref[...] = jnp.dot(cond, w_ref[...], preferred_element_type=F32,
                         precision=lax.Precision.HIGHEST) + b_ref[...]


def ada_mods(cvec, w, b):
    r, d = cvec.shape
    n = w.shape[1]
    tn = _tile(n, 512, LANE)
    return pl.pallas_call(
        _mods_kernel,
        grid=(n // tn,),
        in_specs=[pl.BlockSpec((r, d), lambda j: (0, 0)),
                  pl.BlockSpec((d, tn), lambda j: (0, j)),
                  pl.BlockSpec((1, tn), lambda j: (0, j))],
        out_specs=pl.BlockSpec((r, tn), lambda j: (0, j)),
        out_shape=jax.ShapeDtypeStruct((r, n), F32),
        compiler_params=_cparams("arbitrary"),
        name="ada_mods",
    )(cvec, w, b.reshape(1, n))


def _swap16(y):
    n = y.shape[-1]
    lane = lax.broadcasted_iota(jnp.int32, y.shape, y.ndim - 1)
    fwd = pltpu.roll(y, n - 16, y.ndim - 1)
    bwd = pltpu.roll(y, 16, y.ndim - 1)
    return jnp.where((lane % 32) < 16, fwd, bwd)


PROJ_CHUNK = 512


def _proj_kernel(x_ref, g_ref, sh_ref, sc_ref, w_ref, *rest, splits, rope_cols, with_t):
    n_out = len(splits)
    if rope_cols:
        cq_ref, sq_ref, ck_ref, sk_ref = rest[:4]
        rest = rest[4:]
    if with_t:
        wt_ref, rest = rest[0], rest[1:]
    o_refs = rest[:n_out]
    h = _norm_mod(x_ref[0], g_ref[...], sh_ref[0], sc_ref[0]).astype(BF16)
    if with_t:
        ot_ref = rest[n_out]
        rc = _tile(wt_ref.shape[0], PROJ_CHUNK)
        for r0 in range(0, wt_ref.shape[0], rc):
            ot_ref[0, r0:r0 + rc, :] = lax.dot_general(wt_ref[r0:r0 + rc, :], h, (((1,), (1,)), ((), ())),
                                                       preferred_element_type=F32)
    for o_ref, (s0, s1) in zip(o_refs, splits):
        tc = PROJ_CHUNK if (s1 - s0) % PROJ_CHUNK == 0 and s0 % PROJ_CHUNK == 0 else LANE
        for c0 in range(s0, s1, tc):
            y = jnp.dot(h, w_ref[:, c0:c0 + tc], preferred_element_type=F32)
            if rope_cols and rope_cols[0] <= c0 < rope_cols[2]:
                is_q = c0 < rope_cols[1]
                cos = (cq_ref if is_q else ck_ref)[...]
                sin = (sq_ref if is_q else sk_ref)[...]
                reps = tc // LANE
                cos = jnp.concatenate([cos] * reps, axis=1)
                sin = jnp.concatenate([sin] * reps, axis=1)
                y = y * cos + _swap16(y) * sin
            o_ref[0, :, c0 - s0:c0 - s0 + tc] = y.astype(o_ref.dtype)


def norm_mod_proj(x, g, sh, sc, w, splits, dtypes, rope=None, rope_cols=None, wt=None):
    b, l, d = x.shape
    n = w.shape[1]
    tm = _tile(l, 512)
    assert all(s % LANE == 0 for sp in splits for s in sp)
    if rope_cols:
        assert all(c % PROJ_CHUNK == 0 for c in rope_cols)
    row = lambda bi, i: (bi, i, 0)
    in_specs = [pl.BlockSpec((1, tm, d), row),
                _const_spec((1, d)),
                pl.BlockSpec((1, 1, d), lambda bi, i: (bi, 0, 0)),
                pl.BlockSpec((1, 1, d), lambda bi, i: (bi, 0, 0)),
                _const_spec((d, n))]
    args = [x, g.reshape(1, d), sh, sc, w]
    if rope_cols:
        in_specs += [pl.BlockSpec((tm, LANE), lambda bi, i: (i, 0))] * 4
        args += list(rope)
    out_specs = [pl.BlockSpec((1, tm, s1 - s0), row) for s0, s1 in splits]
    out_shape = [jax.ShapeDtypeStruct((b, l, s1 - s0), dt) for (s0, s1), dt in zip(splits, dtypes)]
    if wt is not None:
        in_specs.append(_const_spec(wt.shape))
        args.append(wt)
        out_specs.append(pl.BlockSpec((1, wt.shape[0], tm), lambda bi, i: (bi, 0, i)))
        out_shape.append(jax.ShapeDtypeStruct((b, wt.shape[0], l), F32))
    return pl.pallas_call(
        functools.partial(_proj_kernel, splits=tuple(splits), rope_cols=rope_cols, with_t=wt is not None),
        grid=(b, l // tm),
        in_specs=in_specs,
        out_specs=out_specs,
        out_shape=out_shape,
        compiler_params=_cparams("parallel", "parallel"),
        name="norm_mod_proj",
    )(*args)


CONV_HALO = 16


def _conformer_kernel(ac_ref, gc_ref, ap_ref, gp_ref, an_ref, gn_ref, w_ref, b_ref, lg_ref, lb_ref, o_ref,
                      u_ref, sh_ref, y_ref, *, tl, ch):
    i = pl.program_id(1)
    last = pl.num_programs(1) - 1
    left = (CONV_W - 1) // 2
    glu = lambda a, g: a * jax.nn.sigmoid(g)
    u_ref[CONV_HALO:CONV_HALO + tl, :] = glu(ac_ref[0], gc_ref[0])
    prev = glu(ap_ref[0], gp_ref[0])
    nxt = glu(an_ref[0], gn_ref[0])
    u_ref[0:CONV_HALO, :] = jnp.where(i > 0, prev, 0.0)
    u_ref[CONV_HALO + tl:CONV_HALO + tl + CONV_HALO, :] = jnp.where(i < last, nxt, 0.0)
    n = sh_ref.shape[1]
    for p in range(SUBLANE):
        sh_ref[p] = u_ref[p:p + n, :]
    rows = min(tl, 128)
    for r0 in range(0, tl, rows):
        for c0 in range(0, ch, LANE):
            acc = jnp.zeros((rows, LANE), F32) + b_ref[:, c0:c0 + LANE]
            for k in range(CONV_W):
                s = CONV_HALO - left + k + r0
                p, base = s % SUBLANE, s - s % SUBLANE
                acc = acc + w_ref[k:k + 1, c0:c0 + LANE] * sh_ref[p, base:base + rows, c0:c0 + LANE]
            y_ref[r0:r0 + rows, c0:c0 + LANE] = acc
    y = y_ref[...]
    mu = jnp.mean(y, axis=-1, keepdims=True)
    yc = y - mu
    var = jnp.mean(yc * yc, axis=-1, keepdims=True)
    z = yc * lax.rsqrt(var + EPS) * lg_ref[...] + lb_ref[...]
    o_ref[0] = _silu(z).astype(o_ref.dtype)


def conformer_conv(p, conv_w, conv_b, ln_g, ln_b):
    b, l, _ = p.shape
    ch = conv_w.shape[1]
    tl = _tile(l, 256, CONV_HALO)
    nh = tl // CONV_HALO
    nblk = l // CONV_HALO
    cur = lambda col: pl.BlockSpec((1, tl, ch), lambda bi, i: (bi, i, col))
    prv = lambda col: pl.BlockSpec((1, CONV_HALO, ch), lambda bi, i: (bi, jnp.maximum(i * nh - 1, 0), col))
    nxt = lambda col: pl.BlockSpec((1, CONV_HALO, ch), lambda bi, i: (bi, jnp.minimum((i + 1) * nh, nblk - 1), col))
    return pl.pallas_call(
        functools.partial(_conformer_kernel, tl=tl, ch=ch),
        grid=(b, l // tl),
        in_specs=[cur(0), cur(1), prv(0), prv(1), nxt(0), nxt(1),
                  _const_spec((CONV_W, ch)), _const_spec((1, ch)), _const_spec((1, ch)), _const_spec((1, ch))],
        out_specs=pl.BlockSpec((1, tl, ch), lambda bi, i: (bi, i, 0)),
        out_shape=jax.ShapeDtypeStruct((b, l, ch), BF16),
        scratch_shapes=[pltpu.VMEM((tl + 2 * CONV_HALO, ch), F32),
                        pltpu.VMEM((SUBLANE, tl + 2 * CONV_HALO - SUBLANE, ch), F32),
                        pltpu.VMEM((tl, ch), F32)],
        compiler_params=_cparams("parallel", "parallel"),
        name="conformer_conv",
    )(p, p, p, p, p, p, conv_w, conv_b.reshape(1, ch), ln_g.reshape(1, ch), ln_b.reshape(1, ch))


ATTN_QBLK = 256


ATTN_ROWS = 128
ATTN_ONES = 16
ATTN_REDO = 120.0
ATTN_REF_ROW = (DIFF_DQK, 0)


def _round_bf16(x):
    return x.astype(BF16).astype(F32)


def _diff_attn_kernel(qt_ref, k_ref, vt_ref, lam_ref, g_ref, o_ref, qm_ref, m_ref, acc_ref, *, lambda_init):
    kv = pl.program_id(3)
    tq = qm_ref.shape[2]
    qblk = min(ATTN_QBLK, tq)
    chains = [(mi, slice(q0, q0 + qblk)) for q0 in range(0, tq, qblk) for mi in range(2)]

    k = k_ref[0]
    lane = lax.broadcasted_iota(jnp.int32, k.shape, 1)
    kx = [jnp.where(lane == r, jnp.ones_like(k), k) for r in ATTN_REF_ROW]
    vtx = vt_ref[0, 0]

    def scores(mi, qs, rows=None):
        keys = kx[mi] if rows is None else kx[mi][:rows]
        return jnp.dot(keys, qm_ref[mi, :, qs].astype(BF16), preferred_element_type=F32)

    def move_reference(mi, qs, m_new):
        m_ref[mi, :, qs] = m_new
        r = ATTN_REF_ROW[mi]
        qm_ref[mi, r:r + 1, qs] = -m_new

    @pl.when(kv == 0)
    def _():
        qt = qt_ref[0].astype(F32)
        row = lax.broadcasted_iota(jnp.int32, qt.shape, 0)
        qm_ref[0] = jnp.where(row < DIFF_DQK, qt, 0.0)
        qm_ref[1] = jnp.where(row < DIFF_DQK, 0.0, qt)
        acc_ref[...] = jnp.zeros(acc_ref.shape, F32)
        first = [scores(mi, qs, ATTN_ROWS) for mi, qs in chains]
        for (mi, qs), s in zip(chains, first):
            move_reference(mi, qs, _round_bf16(jnp.max(s, axis=0, keepdims=True)))

    sc = [scores(mi, qs) for mi, qs in chains]
    mxs, pvs = [], []
    for s in sc:
        es, mx = [], None
        for r0 in range(0, s.shape[0], ATTN_ROWS):
            grp = s[r0:r0 + ATTN_ROWS]
            es.append(jnp.exp2(grp).astype(BF16))
            top = jnp.max(grp, axis=0, keepdims=True)
            mx = top if mx is None else jnp.maximum(mx, top)
        mxs.append(mx)
        pvs.append(jnp.dot(vtx, jnp.concatenate(es, axis=0), preferred_element_type=F32))
    redo = jnp.max(functools.reduce(jnp.maximum, mxs)) > ATTN_REDO

    @pl.when(jnp.logical_not(redo))
    def _():
        for (mi, qs), mx, pv in zip(chains, mxs, pvs):
            m_old = m_ref[mi, :, qs]
            m_new = _round_bf16(m_old + jnp.maximum(mx, 0.0))
            acc_ref[mi, :, qs] = (acc_ref[mi, :, qs] + pv) * jnp.exp2(m_old - m_new)
            move_reference(mi, qs, m_new)

    @pl.when(redo)
    def _():
        for (mi, qs), s in zip(chains, sc):
            m_old = m_ref[mi, :, qs]
            m_new = _round_bf16(m_old + jnp.maximum(jnp.max(s, axis=0, keepdims=True), 0.0))
            delta = m_new - m_old
            e = jnp.exp2(s - delta)
            acc_ref[mi, :, qs] = jnp.exp2(-delta) * acc_ref[mi, :, qs] + jnp.dot(vtx, e.astype(BF16),
                                                                                 preferred_element_type=F32)
            move_reference(mi, qs, m_new)

    @pl.when(kv == pl.num_programs(3) - 1)
    def _():
        lam = (jnp.exp(jnp.sum(lam_ref[0:1, :] * lam_ref[1:2, :], axis=-1, keepdims=True))
               - jnp.exp(jnp.sum(lam_ref[2:3, :] * lam_ref[3:4, :], axis=-1, keepdims=True)) + lambda_init)
        dv = DIFF_DV
        ot = (acc_ref[0, :dv, :] / acc_ref[0, dv:dv + 1, :]
              - lam * (acc_ref[1, :dv, :] / acc_ref[1, dv:dv + 1, :]))
        ot = ot * lax.rsqrt(jnp.mean(ot * ot, axis=0, keepdims=True) + EPS)
        o = ot.T * g_ref[...]
        o_ref[0] = (o * (1.0 - lambda_init)).astype(o_ref.dtype)


def diff_attention(qt, k, vt, lam_vecs, subln_g, lambda_init, k_blk):
    b, _, lq = qt.shape
    lk = k.shape[1]
    tq = _tile(lq, 2048, LANE)
    tk = _tile(lk, 1536, LANE)
    rows = vt.shape[2]
    return pl.pallas_call(
        functools.partial(_diff_attn_kernel, lambda_init=lambda_init),
        grid=(b, DIFF_HEADS, lq // tq, lk // tk),
        in_specs=[pl.BlockSpec((1, LANE, tq), lambda bi, h, i, j: (bi, h, i)),
                  pl.BlockSpec((1, tk, LANE), lambda bi, h, i, j: (bi, j, k_blk + h)),
                  pl.BlockSpec((1, 1, rows, tk), lambda bi, h, i, j: (bi, h, 0, j)),
                  _const_spec((4, DIFF_DQK)), _const_spec((1, DIFF_DV))],
        out_specs=pl.BlockSpec((1, tq, LANE), lambda bi, h, i, j: (bi, i, h)),
        out_shape=jax.ShapeDtypeStruct((b, lq, DIFF_HEADS * DIFF_DV), BF16),
        scratch_shapes=[pltpu.VMEM((2, LANE, tq), F32), pltpu.VMEM((2, 1, tq), F32),
                        pltpu.VMEM((2, rows, tq), F32)],
        compiler_params=_cparams("parallel", "parallel", "parallel", "arbitrary"),
        name="diff_attention",
    )(qt, k, vt, lam_vecs, subln_g.reshape(1, DIFF_DV))


def _delta_gate(of_ref, ob_ref, gate_ref, g_ref):
    o = of_ref[0] + ob_ref[0]
    parts = []
    for c0 in range(0, o.shape[1], DN_DK):
        t = o[:, c0:c0 + DN_DK]
        t = t * lax.rsqrt(jnp.mean(t * t, axis=-1, keepdims=True) + EPS) * g_ref[...]
        parts.append((t * _silu(gate_ref[0, :, c0:c0 + DN_DK])).astype(BF16))
    return jnp.concatenate(parts, axis=1)


def _out_mlp_kernel(x_ref, a_ref, *refs, a_transposed, hidden_chunk, final_norm, delta_b):
    if delta_b:
        mix_b = _delta_gate(*refs[:4])
        refs = refs[4:]
    else:
        mix_b = refs[0][0]
        refs = refs[1:]
    wo_ref, gt1_ref, g2_ref, sh2_ref, sc2_ref, gt2_ref, w1_ref, w2_ref, fg_ref, o_ref = refs
    half = wo_ref.shape[0] // 2
    if a_transposed:
        ya = lax.dot_general(a_ref[0].astype(BF16), wo_ref[0:half, :], (((0,), (0,)), ((), ())),
                             preferred_element_type=F32)
    else:
        ya = jnp.dot(a_ref[0], wo_ref[0:half, :], preferred_element_type=F32)
    y = ya + jnp.dot(mix_b, wo_ref[half:, :], preferred_element_type=F32)
    x1 = x_ref[0] + gt1_ref[0] * y
    h = _norm_mod(x1, g2_ref[...], sh2_ref[0], sc2_ref[0]).astype(BF16)
    acc = jnp.zeros_like(x1)
    for c0 in range(0, w1_ref.shape[1], hidden_chunk):
        t = jnp.maximum(jnp.dot(h, w1_ref[:, c0:c0 + hidden_chunk], preferred_element_type=F32), 0.0)
        acc = acc + jnp.dot((t * t).astype(BF16), w2_ref[c0:c0 + hidden_chunk, :], preferred_element_type=F32)
    x2 = x1 + gt2_ref[0] * acc
    if final_norm:
        x2 = x2 * lax.rsqrt(jnp.mean(x2 * x2, axis=-1, keepdims=True) + EPS) * fg_ref[...]
    o_ref[0] = x2


def out_proj_mlp(x, mix_a, mix_b, w_out, gt1, g2, sh2, sc2, gt2, w1, w2, final_g=None, a_transposed=False):
    b, l, d = x.shape
    delta_b = isinstance(mix_b, tuple)
    c = (mix_b[0] if delta_b else mix_b).shape[2]
    hid = w1.shape[1]
    tm = _tile(l, 512, LANE)
    row = lambda bi, i: (bi, i, 0)
    vec = lambda bi, i: (bi, 0, 0)
    a_spec = (pl.BlockSpec((1, c, tm), lambda bi, i: (bi, 0, i)) if a_transposed
              else pl.BlockSpec((1, tm, c), row))
    if delta_b:
        b_args = list(mix_b[:3]) + [mix_b[3].reshape(1, DN_DK)]
        b_specs = [pl.BlockSpec((1, tm, c), row)] * 3 + [_const_spec((1, DN_DK))]
    else:
        b_args, b_specs = [mix_b], [pl.BlockSpec((1, tm, c), row)]
    fg = jnp.ones((1, d), F32) if final_g is None else final_g.reshape(1, d)
    return pl.pallas_call(
        functools.partial(_out_mlp_kernel, a_transposed=a_transposed, hidden_chunk=min(hid, 1024),
                          final_norm=final_g is not None, delta_b=delta_b),
        grid=(b, l // tm),
        in_specs=[pl.BlockSpec((1, tm, d), row), a_spec] + b_specs + [
                  _const_spec((2 * c, d)),
                  pl.BlockSpec((1, 1, d), vec), _const_spec((1, d)),
                  pl.BlockSpec((1, 1, d), vec), pl.BlockSpec((1, 1, d), vec), pl.BlockSpec((1, 1, d), vec),
                  _const_spec((d, hid)), _const_spec((hid, d)), _const_spec((1, d))],
        out_specs=pl.BlockSpec((1, tm, d), row),
        out_shape=jax.ShapeDtypeStruct((b, l, d), F32),
        compiler_params=_cparams("parallel", "parallel"),
        name="out_proj_mlp",
    )(x, mix_a, *b_args, w_out, gt1, g2.reshape(1, d), sh2, sc2, gt2, w1, w2, fg)


def _rope_tables(length, scale):
    f32 = np.float32
    rows = length // GRID_W
    row = np.repeat(np.arange(rows, dtype=f32), GRID_W)
    col = np.tile(np.arange(GRID_W, dtype=f32), rows)
    n_freq = DIFF_DQK // 4
    inv = (f32(ROPE_BASE) ** (-np.arange(n_freq, dtype=f32) / f32(n_freq))).astype(f32)
    ang_r = row[:, None] * inv
    ang_c = col[:, None] * inv
    cos = np.concatenate([np.cos(ang_r)] * 2 + [np.cos(ang_c)] * 2, axis=-1)
    sin = np.concatenate([-np.sin(ang_r), np.sin(ang_r), -np.sin(ang_c), np.sin(ang_c)], axis=-1)
    cos = np.concatenate([cos, cos], axis=-1) * f32(scale)
    sin = np.concatenate([sin, sin], axis=-1) * f32(scale)
    return cos.astype(f32), sin.astype(f32)


def _flat_tables(length, scale):
    return np.full((length, LANE), scale, np.float32), np.zeros((length, LANE), np.float32)


def _dot_hi(a, b):
    return jnp.dot(a, b, preferred_element_type=F32, precision=lax.Precision.HIGHEST)


def _hy_hidden_kernel(z_ref, w1_ref, b1_ref, f1_ref, w2_ref, b2_ref, f2_ref, o_ref):
    hid = jnp.sin(f1_ref[...] * (_dot_hi(w1_ref[...], z_ref[...]) + b1_ref[...]))
    o_ref[...] = jnp.sin(f2_ref[...] * (_dot_hi(w2_ref[...], hid) + b2_ref[...]))


def hyena_hidden(zt, w1t, b1, f1, w2t, b2, f2):
    e, n = zt.shape
    f = w1t.shape[0]
    tn = _tile(n, 2048, LANE)
    col = lambda v: v.reshape(f, 1)
    return pl.pallas_call(
        _hy_hidden_kernel,
        grid=(n // tn,),
        in_specs=[pl.BlockSpec((e, tn), lambda j: (0, j)),
                  _const_spec((f, e)), _const_spec((f, 1)), _const_spec((f, 1)),
                  _const_spec((f, f)), _const_spec((f, 1)), _const_spec((f, 1))],
        out_specs=pl.BlockSpec((f, tn), lambda j: (0, j)),
        out_shape=jax.ShapeDtypeStruct((f, n), F32),
        compiler_params=_cparams("parallel"),
        name="hyena_hidden",
    )(zt, w1t, col(b1), col(f1), w2t, col(b2), col(f2))


def _hy_filter_kernel(hid_ref, t_ref, wf_ref, wb_ref, dl_ref, o_ref, *, length, tn):
    n2 = hid_ref.shape[1]
    delta = dl_ref[...]

    def piece(c0):
        hid = hid_ref[:, c0:c0 + tn]
        dec = jnp.exp(-t_ref[:, c0:c0 + tn] * delta)
        return _dot_hi(wf_ref[...], hid) * dec, _dot_hi(wb_ref[...], hid) * dec

    norm = jnp.zeros((wf_ref.shape[0], 1), F32)
    for c0 in range(length, n2, tn):
        ff, fb = piece(c0)
        norm = norm + jnp.sum(jnp.abs(ff) + jnp.abs(fb), axis=-1, keepdims=True)
    inv = 1.0 / norm
    for c0 in range(0, n2, tn):
        ff, fb = piece(c0)
        if c0 >= length:
            o_ref[0, :, c0:c0 + tn] = ff * inv
        else:
            col = lax.broadcasted_iota(jnp.int32, fb.shape, 1) + c0
            o_ref[0, :, c0:c0 + tn] = jnp.where(col == 0, 0.0, fb * inv)


def hyena_filters(hid, t_row, w3t, deltas, length):
    f, n2 = hid.shape
    ch = deltas.shape[0]
    cb = LANE
    nblk = ch // cb
    tn = _tile(length, 2048, LANE)
    return pl.pallas_call(
        functools.partial(_hy_filter_kernel, length=length, tn=tn),
        grid=(HY_ORDER, nblk),
        in_specs=[_const_spec((f, n2)), _const_spec((1, n2)),
                  pl.BlockSpec((cb, f), lambda o, c: (o * 2 * nblk + c, 0)),
                  pl.BlockSpec((cb, f), lambda o, c: (o * 2 * nblk + nblk + c, 0)),
                  pl.BlockSpec((cb, 1), lambda o, c: (c, 0))],
        out_specs=pl.BlockSpec((1, cb, n2), lambda o, c: (o, c, 0)),
        out_shape=jax.ShapeDtypeStruct((HY_ORDER, ch, n2), F32),
        compiler_params=_cparams("parallel", "parallel"),
        name="hyena_filters",
    )(hid, t_row, w3t, w3t, deltas.reshape(ch, 1))


HY_T = 256
HY_CB = SUBLANE
HY_PAIR = 2

def _hyena_kernel(v_ref, x1_ref, x2_ref, g_ref, sw_ref, sb_ref, sk_ref_s, o_ref, sk_ref, u_ref, acc_ref,
                  *, length, nch):
    nb = v_ref.shape[0]
    nj = length // HY_T
    cblk = pl.program_id(0)
    lane = lax.broadcasted_iota(jnp.int32, (nb, length), 1)

    def short(ref, cc, part):
        x = ref[:, cc, :]
        ch = part * nch + cblk * HY_CB + cc
        prev = jnp.where(lane == 0, 0.0, pltpu.roll(x, 1, 1))
        nxt = jnp.where(lane == length - 1, 0.0, pltpu.roll(x, length - 1, 1))
        return sw_ref[0, ch] * prev + sw_ref[1, ch] * x + sw_ref[2, ch] * nxt + sb_ref[ch]

    def build_skew(order, cc, slot):
        g = jnp.broadcast_to(g_ref[order, pl.ds(cc, 1), :], (2 * SUBLANE, 2 * length))
        half = HY_T // 2
        for r0 in range(0, half, 2 * SUBLANE):
            pair = pltpu.roll(g, r0, 1, stride=1, stride_axis=0).astype(BF16)
            sk_ref[slot, r0:r0 + 2 * SUBLANE, :] = pair
            sk_ref[slot, half + r0:half + r0 + 2 * SUBLANE, half:] = pair[:, :2 * length - half]

    def long_conv(u, slot):
        rows = nj * nb
        x2 = jnp.concatenate([u[:, j * HY_T:(j + 1) * HY_T] for j in range(nj)]
                             + [jnp.zeros((2 * nb, HY_T), F32)], axis=0)
        u_ref[slot, 0] = x2.astype(BF16)
        u_ref[slot, 1] = jnp.concatenate([x2[nb:, :], x2[rows:rows + nb, :]], axis=0).astype(BF16)
        acc_ref[slot] = jnp.zeros(acc_ref.shape[1:], F32)
        for d in range(-(nj - 1), nj):
            m = (nj - abs(d)) * nb
            lo_in, lo_out = max(0, -d) * nb, max(0, d) * nb
            which, off = (0, lo_in) if lo_in % (2 * nb) == 0 else (1, lo_in - nb)
            mm = -(-m // (2 * nb)) * (2 * nb)
            tile = sk_ref[slot, :, length + d * HY_T:length + (d + 1) * HY_T]
            part = jnp.dot(u_ref[slot, which, off:off + mm, :], tile, preferred_element_type=F32)
            acc_ref[slot, lo_out:lo_out + m, :] += part[:m]
        return jnp.concatenate([acc_ref[slot, i * nb:(i + 1) * nb, :] for i in range(nj)], axis=1)

    def channels(p, carry):
        ccs = [p * HY_PAIR + s for s in range(HY_PAIR)]
        vs = [short(v_ref, cc, 0) for cc in ccs]
        x1s = [short(x1_ref, cc, 1) for cc in ccs]
        x2s = [short(x2_ref, cc, 2) for cc in ccs]
        zs = vs
        for order, gates in enumerate((x1s, x2s)):
            for slot, cc in enumerate(ccs):
                build_skew(order, cc, slot)
            ys = [long_conv(z, slot) for slot, z in enumerate(zs)]
            zs = [g * (y + sk_ref_s[order, cblk * HY_CB + cc] * z) for g, y, z, cc in zip(gates, ys, zs, ccs)]
        for cc, z in zip(ccs, zs):
            o_ref[:, cc, :] = z
        return carry

    lax.fori_loop(0, HY_CB // HY_PAIR, channels, 0)


def hyena_mix(pt, filt, short_w, short_b, skip):
    b, c3, l = pt.shape
    nch = c3 // 3
    assert l % HY_T == 0 and nch % HY_CB == 0 and b == SUBLANE
    nblk = nch // HY_CB
    smem = pl.BlockSpec(memory_space=pltpu.SMEM)
    part = lambda k: pl.BlockSpec((b, HY_CB, l), lambda c: (0, k * nblk + c, 0))
    return pl.pallas_call(
        functools.partial(_hyena_kernel, length=l, nch=nch),
        grid=(nblk,),
        in_specs=[part(0), part(1), part(2),
                  pl.BlockSpec((HY_ORDER, HY_CB, 2 * l), lambda c: (0, c, 0)),
                  smem, smem, smem],
        out_specs=pl.BlockSpec((b, HY_CB, l), lambda c: (0, c, 0)),
        out_shape=jax.ShapeDtypeStruct((b, nch, l), F32),
        scratch_shapes=[pltpu.VMEM((HY_PAIR, HY_T, 2 * l), BF16),
                        pltpu.VMEM((HY_PAIR, 2, l // HY_T * b + 2 * b, HY_T), BF16),
                        pltpu.VMEM((HY_PAIR, l // HY_T * b, HY_T), F32)],
        compiler_params=_cparams("parallel"),
        name="hyena_mix",
    )(pt, pt, pt, filt, short_w, short_b, skip)


DN_HALO = SUBLANE
DN_BETA_F, DN_BETA_B, DN_G_F, DN_G_B = 0, DN_HEADS, 2 * DN_HEADS, 3 * DN_HEADS


def _dn_feat_kernel(c_ref, p_ref, n_ref, s_ref, w_ref, al_ref, dt_ref, q_ref, k_ref, v_ref, f_ref, u_ref, *, tl):
    i = pl.program_id(1)
    last = pl.num_programs(1) - 1
    left = (DN_CONV_W - 1) // 2
    nc = c_ref.shape[2]
    u_ref[DN_HALO:DN_HALO + tl, :] = c_ref[0]
    u_ref[0:DN_HALO, :] = jnp.where(i > 0, p_ref[0], 0.0)
    u_ref[DN_HALO + tl:DN_HALO + tl + DN_HALO, :] = jnp.where(i < last, n_ref[0], 0.0)
    rows = min(tl, 128)
    outs = (q_ref, k_ref, v_ref)
    per = nc // len(outs)
    for r0 in range(0, tl, rows):
        for c0 in range(0, nc, LANE):
            acc = jnp.zeros((rows, LANE), F32)
            for k in range(DN_CONV_W):
                s = DN_HALO - left + k + r0
                acc = acc + w_ref[k:k + 1, c0:c0 + LANE] * u_ref[s:s + rows, c0:c0 + LANE]
            y = _silu(acc)
            which, off = divmod(c0, per)
            if which < 2:
                y = y * lax.rsqrt(jnp.sum(y * y, axis=-1, keepdims=True) + 1e-6)
            outs[which][0, r0:r0 + rows, off:off + LANE] = y
    s = s_ref[0]
    lane = lax.broadcasted_iota(jnp.int32, s.shape, 1)
    gate = -jnp.exp(al_ref[...]) * jax.nn.softplus(s + dt_ref[...])
    f_ref[0] = jnp.where(lane < DN_G_F, jax.nn.sigmoid(s), gate)


def dn_features(qkv, small, conv_w, alog_row, dtb_row):
    b, l, nc = qkv.shape
    tl = _tile(l, 256, DN_HALO)
    nh = tl // DN_HALO
    nblk = l // DN_HALO
    w = nc // 3
    row = lambda bi, i: (bi, i, 0)
    return pl.pallas_call(
        functools.partial(_dn_feat_kernel, tl=tl),
        grid=(b, l // tl),
        in_specs=[pl.BlockSpec((1, tl, nc), row),
                  pl.BlockSpec((1, DN_HALO, nc), lambda bi, i: (bi, jnp.maximum(i * nh - 1, 0), 0)),
                  pl.BlockSpec((1, DN_HALO, nc), lambda bi, i: (bi, jnp.minimum((i + 1) * nh, nblk - 1), 0)),
                  pl.BlockSpec((1, tl, LANE), row),
                  _const_spec((DN_CONV_W, nc)), _const_spec((1, LANE)), _const_spec((1, LANE))],
        out_specs=[pl.BlockSpec((1, tl, w), row)] * 3 + [pl.BlockSpec((1, tl, LANE), row)],
        out_shape=[jax.ShapeDtypeStruct((b, l, w), F32)] * 3 + [jax.ShapeDtypeStruct((b, l, LANE), F32)],
        scratch_shapes=[pltpu.VMEM((tl + 2 * DN_HALO, nc), F32)],
        compiler_params=_cparams("parallel", "parallel"),
        name="dn_features",
    )(qkv, qkv, qkv, small, conv_w, alog_row, dtb_row)


def _bdot(a, b):
    return jnp.dot(a.astype(BF16), b.astype(BF16), preferred_element_type=F32)


def _bdot_nt(a, b):
    return lax.dot_general(a.astype(BF16), b.astype(BF16), (((1,), (1,)), ((), ())), preferred_element_type=F32)


def _bdot_tn(a, b):
    return lax.dot_general(a.astype(BF16), b.astype(BF16), (((0,), (0,)), ((), ())), preferred_element_type=F32)


def _cumsum_rows(x, reverse):
    n = x.shape[0]
    row = lax.broadcasted_iota(jnp.int32, x.shape, 0)
    s = 1
    while s < n:
        if reverse:
            x = x + jnp.where(row < n - s, pltpu.roll(x, n - s, 0), 0.0)
        else:
            x = x + jnp.where(row >= s, pltpu.roll(x, s, 0), 0.0)
        s *= 2
    return x


def _dn_chunks(chains):
    c = chains[0]["k"].shape[0]
    ri = lax.broadcasted_iota(jnp.int32, (c, c), 0)
    ci = lax.broadcasted_iota(jnp.int32, (c, c), 1)
    work = []
    for ch in chains:
        rev = ch["reverse"]
        incl = (ri <= ci) if rev else (ri >= ci)
        strict = (ri < ci) if rev else (ri > ci)
        decay = jnp.where(incl, jnp.exp(jnp.where(incl, ch["gcol"] - ch["grow"], 0.0)), 0.0)
        kb = ch["k"] * ch["beta"]
        qs = ch["q"] * (ch["k"].shape[1] ** -0.5)
        work.append(dict(ch, strict=strict, decay=decay, kb=kb, qs=qs))
    grams = [_bdot_nt(jnp.concatenate([w["kb"], w["qs"]], axis=0), w["k"]) for w in work]
    lowers = [jnp.where(w["strict"], g[:c] * w["decay"], 0.0) for w, g in zip(work, grams)]
    same2 = (ri // 2) == (ci // 2)
    ymats = [-jnp.where(same2, lo, 0.0) for lo in lowers]
    m = 2
    while m < c:
        pair = ((ri // (2 * m)) == (ci // (2 * m))) & ((ri // m) != (ci // m))
        offs = [jnp.where(pair, lo, 0.0) for lo in lowers]
        zs = [off + _bdot(ym, off) for ym, off in zip(ymats, offs)]
        ymats = [ym - z - _bdot(z, ym) for ym, z in zip(ymats, zs)]
        m *= 2
    outs = []
    uws, e_gs = [], []
    for w, ym in zip(work, ymats):
        e_g = jnp.exp(w["gcol"])
        rhs = jnp.concatenate([w["v"] * w["beta"], w["kb"] * e_g], axis=1)
        uws.append(rhs + _bdot(ym, rhs))
        e_gs.append(e_g)
    dv = work[0]["v"].shape[1]
    wss = [_bdot(jnp.concatenate([uw[:, dv:], w["qs"] * e_g], axis=0), w["state"])
           for w, uw, e_g in zip(work, uws, e_gs)]
    v_news = [uw[:, :dv] - ws[:c] for uw, ws in zip(uws, wss)]
    for w, g, ws, v_new in zip(work, grams, wss, v_news):
        gcol = w["gcol"]
        g_last = gcol[0:1, :] if w["reverse"] else gcol[c - 1:c, :]
        k_dec = w["k"] * jnp.exp(g_last - gcol)
        o = ws[c:] + _bdot(g[c:] * w["decay"], v_new)
        new_state = w["state"] * jnp.exp(g_last) + _bdot_tn(k_dec, v_new)
        outs.append((o, new_state))
    return outs


def _dn_scan_kernel(qf_ref, kf_ref, vf_ref, ff_ref, qb_ref, kb_ref, vb_ref, fb_ref, s0f_ref, s0b_ref,
                    of_ref, ob_ref, sf_ref, sb_ref, st_ref):
    n = pl.program_id(1)

    @pl.when(n == 0)
    def _():
        st_ref[0] = s0f_ref[...]
        st_ref[1] = s0b_ref[...]

    dirs = ((qf_ref, kf_ref, vf_ref, ff_ref, of_ref, False), (qb_ref, kb_ref, vb_ref, fb_ref, ob_ref, True))
    chains, dests = [], []
    for bi in range(qf_ref.shape[0]):
        for di, (q_ref, k_ref, v_ref, f_ref, o_ref, reverse) in enumerate(dirs):
            feats = f_ref[bi]
            gc = _cumsum_rows(feats, reverse)
            gct = gc.T
            for h in range(DN_HEADS):
                cols = slice(h * DN_DK, (h + 1) * DN_DK)
                bl = (DN_BETA_B if reverse else DN_BETA_F) + h
                gl = (DN_G_B if reverse else DN_G_F) + h
                chains.append(dict(q=q_ref[bi, :, cols], k=k_ref[bi, :, cols], v=v_ref[bi, :, cols],
                                   beta=feats[:, bl:bl + 1], gcol=gc[:, gl:gl + 1], grow=gct[gl:gl + 1, :],
                                   state=st_ref[di, bi, h], reverse=reverse))
                dests.append((o_ref, cols, di, bi, h))
    for (o_ref, cols, di, bi, h), (o, s_new) in zip(dests, _dn_chunks(chains)):
        o_ref[bi, :, cols] = o
        st_ref[di, bi, h] = s_new

    @pl.when(n == pl.num_programs(1) - 1)
    def _():
        sf_ref[...] = st_ref[0]
        sb_ref[...] = st_ref[1]


def dn_scan(q, k, v, feats, s0f, s0b):
    b, l, w = q.shape
    nchunk = l // DN_CHUNK
    nb = DN_BATCH if b % DN_BATCH == 0 else 1
    fwd = lambda bi, n: (bi, n, 0)
    bwd = lambda bi, n: (bi, nchunk - 1 - n, 0)
    st = lambda bi, n: (bi, 0, 0, 0)
    blk = lambda im, width: pl.BlockSpec((nb, DN_CHUNK, width), im)
    st_spec = pl.BlockSpec((nb, DN_HEADS, DN_DK, DN_DK), st)
    st_shape = jax.ShapeDtypeStruct((b, DN_HEADS, DN_DK, DN_DK), F32)
    return pl.pallas_call(
        _dn_scan_kernel,
        grid=(b // nb, nchunk),
        in_specs=[blk(fwd, w), blk(fwd, w), blk(fwd, w), blk(fwd, LANE),
                  blk(bwd, w), blk(bwd, w), blk(bwd, w), blk(bwd, LANE), st_spec, st_spec],
        out_specs=[blk(fwd, w), blk(bwd, w), st_spec, st_spec],
        out_shape=[jax.ShapeDtypeStruct((b, l, w), F32)] * 2 + [st_shape, st_shape],
        scratch_shapes=[pltpu.VMEM((2, nb, DN_HEADS, DN_DK, DN_DK), F32)],
        compiler_params=_cparams("parallel", "arbitrary"),
        name="dn_scan",
    )(q, k, v, feats, q, k, v, feats, s0f, s0b)


def _split_mods(mods, b, d):
    lat = [mods[:b, k * d:(k + 1) * d].reshape(b, 1, d) for k in range(N_MOD)]
    ctx = [jnp.broadcast_to(mods[b, k * d:(k + 1) * d].reshape(1, 1, d), (b, 1, d)) for k in range(N_MOD)]
    return lat, ctx


def _even_layer(x, ctx, lat_m, ctx_m, n1g, n2g, w_in, conv_w, conv_b, ln_g, ln_b, lam_vecs, subln_g, w_out,
                w1, w2, lambda_init):
    b, l, d = x.shape
    lc = ctx.shape[1]
    ch = conv_w.shape[1]
    qk = DIFF_HEADS * 2 * DIFF_DQK
    q0, k0, v0, n_in = 2 * ch, 2 * ch + qk, 2 * ch + 2 * qk, 2 * ch + 2 * qk + DIFF_HEADS * DIFF_DV
    scale = DIFF_DQK ** -0.5 * math.log2(math.e)
    splits = [(0, q0), (q0, n_in)]
    w_in = w_in.astype(BF16)
    rope_lat = _rope_tables(l, scale) + _rope_tables(l, 1.0)
    rope_ctx = _flat_tables(lc, scale) + _flat_tables(lc, 1.0)
    pc_lat, qkv_lat = norm_mod_proj(x, n1g, lat_m[0], lat_m[1], w_in, splits, (F32, BF16), rope_lat, (q0, k0, v0))
    pc_ctx, qkv_ctx = norm_mod_proj(ctx, n1g, ctx_m[0], ctx_m[1], w_in, splits, (F32, BF16), rope_ctx, (q0, k0, v0))
    conv_lat = conformer_conv(pc_lat, conv_w, conv_b, ln_g, ln_b)
    conv_ctx = conformer_conv(pc_ctx, conv_w, conv_b, ln_g, ln_b)
    k_all = jnp.concatenate([qkv_lat[:, :, qk:2 * qk], qkv_ctx[:, :, qk:2 * qk]], axis=1)
    def heads_t(v):
        vt = jnp.swapaxes(v, 1, 2).reshape(b, DIFF_HEADS, DIFF_DV, v.shape[1])
        return jnp.concatenate([vt, jnp.ones((b, DIFF_HEADS, ATTN_ONES, v.shape[1]), vt.dtype)], axis=2)

    vt_ctx = heads_t(qkv_ctx[:, :, 2 * qk:])
    vt_all = jnp.concatenate([heads_t(qkv_lat[:, :, 2 * qk:]), vt_ctx], axis=3)
    qt_lat = jnp.swapaxes(qkv_lat[:, :, :qk], 1, 2)
    qt_ctx = jnp.swapaxes(qkv_ctx[:, :, :qk], 1, 2)
    o_lat = diff_attention(qt_lat, k_all, vt_all, lam_vecs, subln_g, lambda_init, 0)
    o_ctx = diff_attention(qt_ctx, qkv_ctx, vt_ctx, lam_vecs, subln_g, lambda_init, qk // LANE)
    w_out, w1, w2 = w_out.astype(BF16), w1.astype(BF16), w2.astype(BF16)
    x = out_proj_mlp(x, conv_lat, o_lat, w_out, lat_m[2], n2g, lat_m[3], lat_m[4], lat_m[5], w1, w2)
    ctx = out_proj_mlp(ctx, conv_ctx, o_ctx, w_out, ctx_m[2], n2g, ctx_m[3], ctx_m[4], ctx_m[5], w1, w2)
    return x, ctx


def _hyena_position_tables(length):
    f32 = np.float32
    t = np.linspace(0.0, 1.0, length, dtype=f32)[:, None]
    bands = (HY_EMB - 1) // 2
    omega = (f32(2.0 * math.pi) * np.arange(length, dtype=f32)[:, None] / f32(length)).astype(f32)
    ang = omega * np.linspace(1e-4, bands - 1, bands, dtype=f32)
    z = np.concatenate([t, np.cos(ang), -np.sin(ang)], axis=-1).astype(f32)
    pos = np.minimum(np.abs(np.arange(2 * length) - length), length - 1)
    pad = -HY_EMB % SUBLANE
    zt = np.pad(z[pos].T, ((0, pad), (0, 0)))
    return np.ascontiguousarray(zt), np.ascontiguousarray(t[pos].T)


def _odd_layer_last(x, ctx, lat_m, ctx_m, n1g, n2g, w_in, hy_short_w, hy_short_b, hy_w1, hy_b1, hy_f1, hy_w2, hy_b2,
                    hy_f2, hy_w3, hy_skip, dn_conv_w, alog_f, alog_b, dtb_f, dtb_b, dn_norm_g, w_out, w1, w2,
                    final_g):
    b, l, d = x.shape
    hy_in = hy_short_w.shape[1]
    hy_ch = hy_in // (HY_ORDER + 1)
    dn_w = DN_HEADS * DN_DK
    n_rest = w_in.shape[1] - hy_in
    n_pad = -n_rest % LANE
    w_rest = jnp.pad(w_in[:, hy_in:], ((0, 0), (0, n_pad))).astype(BF16)
    splits = [(0, dn_w), (dn_w, 4 * dn_w), (4 * dn_w, n_rest + n_pad)]
    f32x3 = (F32, F32, F32)
    gate_lat, qkv_lat, small_lat, pt = norm_mod_proj(x, n1g, lat_m[0], lat_m[1], w_rest, splits, f32x3,
                                                     wt=w_in[:, :hy_in].T.astype(BF16))
    _, qkv_ctx, small_ctx = norm_mod_proj(ctx, n1g, ctx_m[0], ctx_m[1], w_rest, splits, f32x3)

    zt, t_row = _hyena_position_tables(l)
    w1t = jnp.pad(hy_w1.T, ((0, 0), (0, zt.shape[0] - hy_w1.shape[0])))
    hid = hyena_hidden(zt, w1t, hy_b1, hy_f1, hy_w2.T, hy_b2, hy_f2)
    deltas = jnp.abs(jnp.linspace(HY_MIN_DECAY, HY_MAX_DECAY, hy_ch, dtype=F32))
    filt = hyena_filters(hid, t_row, hy_w3.T, deltas, l)
    hy = hyena_mix(pt, filt, hy_short_w, hy_short_b, hy_skip)

    lanes = jnp.zeros((LANE,), F32)
    alog_row = lanes.at[DN_G_F:DN_G_F + DN_HEADS].set(alog_f).at[DN_G_B:DN_G_B + DN_HEADS].set(alog_b).reshape(1, LANE)
    dtb_row = lanes.at[DN_G_F:DN_G_F + DN_HEADS].set(dtb_f).at[DN_G_B:DN_G_B + DN_HEADS].set(dtb_b).reshape(1, LANE)
    q_c, k_c, v_c, f_c = dn_features(qkv_ctx, small_ctx, dn_conv_w, alog_row, dtb_row)
    q_l, k_l, v_l, f_l = dn_features(qkv_lat, small_lat, dn_conv_w, alog_row, dtb_row)
    s0 = jnp.zeros((b, DN_HEADS, DN_DK, DN_DK), F32)
    _, _, s_cf, s_cb = dn_scan(q_c, k_c, v_c, f_c, s0, s0)
    o_f, o_b, _, _ = dn_scan(q_l, k_l, v_l, f_l, s_cf, s_cb)
    dn = (o_f, o_b, gate_lat, dn_norm_g)

    return out_proj_mlp(x, hy, dn, w_out.astype(BF16), lat_m[2], n2g, lat_m[3], lat_m[4], lat_m[5],
                        w1.astype(BF16), w2.astype(BF16), final_g=final_g, a_transposed=True)


def kernel(x, c, ctx, c_ctx, ada_w, ada_b, norm1_g, norm2_g, mlp_w1, mlp_w2, ev_w_in, ev_conv_w, ev_conv_b, ev_ln_g,
           ev_ln_b, ev_lq1, ev_lk1, ev_lq2, ev_lk2, ev_subln_g, ev_w_out, od_w_in, od_hy_short_w, od_hy_short_b,
           od_hy_w1, od_hy_b1, od_hy_freq1, od_hy_w2, od_hy_b2, od_hy_freq2, od_hy_w3, od_hy_skip, od_dn_conv_w,
           od_dn_alog_f, od_dn_alog_b, od_dn_dtb_f, od_dn_dtb_b, od_dn_norm_g, od_w_out, final_g):
    b, _, d = x.shape
    assert ada_w.shape[0] == 2, "layer 0 = conformer/diff-attention, layer 1 (last) = Hyena/DeltaNet"
    rows = -(-(b + 1) // SUBLANE) * SUBLANE
    cvec = jnp.zeros((rows, d), F32).at[:b].set(c).at[b].set(c_ctx)

    lat_m, ctx_m = _split_mods(ada_mods(cvec, ada_w[0], ada_b[0]), b, d)
    lam_vecs = jnp.stack([ev_lq1[0], ev_lk1[0], ev_lq2[0], ev_lk2[0]])
    x, ctx = _even_layer(x, ctx, lat_m, ctx_m, norm1_g[0], norm2_g[0], ev_w_in[0], ev_conv_w[0], ev_conv_b[0],
                         ev_ln_g[0], ev_ln_b[0], lam_vecs, ev_subln_g[0], ev_w_out[0], mlp_w1[0], mlp_w2[0],
                         0.8 - 0.6 * math.exp(-0.3 * 0))

    lat_m, ctx_m = _split_mods(ada_mods(cvec, ada_w[1], ada_b[1]), b, d)
    return _odd_layer_last(x, ctx, lat_m, ctx_m, norm1_g[1], norm2_g[1], od_w_in[0], od_hy_short_w[0],
                           od_hy_short_b[0], od_hy_w1[0], od_hy_b1[0], od_hy_freq1[0], od_hy_w2[0], od_hy_b2[0],
                           od_hy_freq2[0], od_hy_w3[0], od_hy_skip[0], od_dn_conv_w[0], od_dn_alog_f[0],
                           od_dn_alog_b[0], od_dn_dtb_f[0], od_dn_dtb_b[0], od_dn_norm_g[0], od_w_out[0],
                           mlp_w1[1], mlp_w2[1], final_g)
```
